```python
import math
import jax, jax.numpy as jnp
from jax import lax
import numpy as np

D_MODEL = 1024
BATCH = 8
SEQ = 2048
DEPTH = 4

GRID_W = 64
CTX_LEN = 256
HEAD_DIM = 64
ROPE_THETA = 10000.0
Q_BLOCK = 128
WINDOW = 128
A_HEADS = 4
B_HEADS = 8
B_KV = 2
C_HEADS = 8
C_KV = 2
F_GROUPS = 8
F_GROUP_DIM = 64
BRANCH_W = 512
N_BRANCH = 4
D_FF = 2816
N_EXPERTS = 8
TOP_K = 2
D_FF_EXPERT = 3584
EPS = 1e-6
SUBLN_EPS = 1e-5
NEG_INF = -1e30
IN_WIDTHS = (
    A_HEADS * 2 * HEAD_DIM, A_HEADS * 2 * HEAD_DIM, A_HEADS * 2 * HEAD_DIM,
    B_HEADS * HEAD_DIM, B_KV * HEAD_DIM, B_KV * HEAD_DIM,
    C_HEADS * HEAD_DIM, C_KV * HEAD_DIM, C_KV * HEAD_DIM,
    F_GROUPS * F_GROUP_DIM,
    N_BRANCH * D_MODEL,
)
D_IN = sum(IN_WIDTHS)

kernel_name = "hybrid_diffusion_parallel_mixer_block"


def rmsnorm(x, g, eps=EPS):
    xf = x.astype(jnp.float32)
    y = xf * lax.rsqrt(jnp.mean(xf * xf, axis=-1, keepdims=True) + eps)
    return (y * g.astype(jnp.float32)).astype(x.dtype)


def adaln(x, g, shift, scale):
    return rmsnorm(x, g) * (1 + scale) + shift


def flat(o):
    return o.reshape(o.shape[0], o.shape[1], -1)


def axial_rope_tables(rows, dtype):
    pos_row = jnp.repeat(jnp.arange(rows, dtype=jnp.int32), GRID_W).astype(jnp.float32)
    pos_col = jnp.tile(jnp.arange(GRID_W, dtype=jnp.int32), rows).astype(jnp.float32)
    quarter = HEAD_DIM // 4
    inv_freq = ROPE_THETA ** (-jnp.arange(quarter, dtype=jnp.float32) / quarter)
    ang_r = pos_row[:, None] * inv_freq
    ang_c = pos_col[:, None] * inv_freq
    ang = jnp.concatenate([ang_r, ang_r, ang_c, ang_c], axis=-1)
    return jnp.cos(ang).astype(dtype), jnp.sin(ang).astype(dtype)


def apply_rope(x, cos, sin):
    shape = (cos.shape[0],) + (1,) * (x.ndim - 3) + (HEAD_DIM,)
    cos = cos.reshape(shape)
    sin = sin.reshape(shape)
    xa = x.reshape(x.shape[:-1] + (2, 2, HEAD_DIM // 4))
    rot = jnp.stack([-xa[..., 1, :], xa[..., 0, :]], axis=-2).reshape(x.shape)
    return x * cos + rot * sin


def sweep_query_blocks(fn, q):
    B, S = q.shape[:2]
    nb = S // Q_BLOCK
    qb = jnp.moveaxis(q.reshape((B, nb, Q_BLOCK) + q.shape[2:]), 1, 0)
    ob = lax.map(lambda a: fn(a[0], a[1]), (qb, jnp.arange(nb, dtype=jnp.int32)))
    return jnp.moveaxis(ob, 0, 1).reshape((B, S) + ob.shape[3:])


def split_projection(p):
    B, N = p.shape[:2]
    points = [int(v) for v in np.cumsum(IN_WIDTHS)[:-1]]
    aq, ak, av, bq, bk, bv, cq, ck, cv, fz, gt = jnp.split(p, points, axis=-1)
    return dict(
        aq=aq.reshape(B, N, A_HEADS, 2, HEAD_DIM),
        ak=ak.reshape(B, N, A_HEADS, 2, HEAD_DIM),
        av=av.reshape(B, N, A_HEADS, 2 * HEAD_DIM),
        bq=bq.reshape(B, N, B_KV, B_HEADS // B_KV, HEAD_DIM),
        bk=bk.reshape(B, N, B_KV, HEAD_DIM),
        bv=bv.reshape(B, N, B_KV, HEAD_DIM),
        cq=cq.reshape(B, N, C_KV, C_HEADS // C_KV, HEAD_DIM),
        ck=ck.reshape(B, N, C_KV, HEAD_DIM),
        cv=cv.reshape(B, N, C_KV, HEAD_DIM),
        fz=fz,
        gates=gt.reshape(B, N, N_BRANCH, D_MODEL),
    )


def diff_attention(q, k, v, lam):
    s = jnp.einsum("bqhmd,bkhmd->bhmqk", q, k).astype(jnp.float32) * HEAD_DIM ** -0.5
    p = jax.nn.softmax(s, axis=-1)
    a = (p[:, :, 0] - lam * p[:, :, 1]).astype(v.dtype)
    return jnp.einsum("bhqk,bkhe->bqhe", a, v)


def diff_head_norm(o, g, lam_init):
    return flat(rmsnorm(o, g, SUBLN_EPS) * (1.0 - lam_init))


def gqa_attention(q, k, v):
    s = jnp.einsum("bqhgd,bkhd->bhgqk", q, k).astype(jnp.float32) * HEAD_DIM ** -0.5
    p = jax.nn.softmax(s, axis=-1).astype(v.dtype)
    return jnp.einsum("bhgqk,bkhd->bqhgd", p, v)


def softmax_with_sink(s, sink):
    sk = jnp.broadcast_to(sink.astype(jnp.float32)[None, :, :, None, None], s.shape[:-1] + (1,))
    p = jax.nn.softmax(jnp.concatenate([s, sk], axis=-1), axis=-1)
    return p[..., :-1]


def sink_attention(q, k, v, sink):
    s = jnp.einsum("bqhgd,bkhd->bhgqk", q, k).astype(jnp.float32) * HEAD_DIM ** -0.5
    p = softmax_with_sink(s, sink).astype(v.dtype)
    return jnp.einsum("bhgqk,bkhd->bqhgd", p, v)


def window_block(q, b, k_pad, v_pad, k_ctx, v_ctx, sink, n_tokens):
    span = Q_BLOCK + 2 * WINDOW
    kb = lax.dynamic_slice_in_dim(k_pad, b * Q_BLOCK, span, axis=1)
    vb = lax.dynamic_slice_in_dim(v_pad, b * Q_BLOCK, span, axis=1)
    scale = HEAD_DIM ** -0.5
    s_band = jnp.einsum("bqhgd,bkhd->bhgqk", q, kb).astype(jnp.float32) * scale
    r = jnp.arange(Q_BLOCK, dtype=jnp.int32)[:, None]
    u = jnp.arange(span, dtype=jnp.int32)[None, :]
    j = b * Q_BLOCK - WINDOW + u
    valid = (u >= r) & (u - r <= 2 * WINDOW) & (j >= 0) & (j < n_tokens)
    s_band = jnp.where(valid, s_band, NEG_INF)
    s_ctx = jnp.einsum("bqhgd,bkhd->bhgqk", q, k_ctx).astype(jnp.float32) * scale
    p = softmax_with_sink(jnp.concatenate([s_band, s_ctx], axis=-1), sink).astype(v_pad.dtype)
    return (jnp.einsum("bhgqk,bkhd->bqhgd", p[..., :span], vb)
            + jnp.einsum("bhgqk,bkhd->bqhgd", p[..., span:], v_ctx))


def fourier_mix(z):
    B, N = z.shape[:2]
    zg = z.reshape(B, N, F_GROUPS, F_GROUP_DIM).astype(jnp.float32)
    y = jnp.fft.fft2(zg, axes=(1, 3), norm="ortho").real
    return y.reshape(B, N, F_GROUPS * F_GROUP_DIM).astype(z.dtype)


def merge_branches(branches, gates, w_br, w_o):
    o = jnp.stack(branches, axis=2)
    y = jnp.einsum("bnrc,rcd->bnrd", o, w_br)
    merged = jnp.sum(jax.nn.sigmoid(gates) * y, axis=2)
    return jnp.einsum("bnd,de->bne", merged, w_o)


def swiglu(h, wg, wu, wd):
    return (jax.nn.silu(h @ wg) * (h @ wu)) @ wd


def moe_swiglu(h, w_r, wg, wu, wd):
    logits = jnp.einsum("bnd,de->bne", h, w_r).astype(jnp.float32)
    top_v, top_i = lax.top_k(logits, TOP_K)
    top_w = jax.nn.softmax(top_v, axis=-1)
    gates = jnp.sum(jax.nn.one_hot(top_i, N_EXPERTS, dtype=jnp.float32) * top_w[..., None], axis=-2)
    gates = gates.astype(h.dtype)
    y = jnp.zeros_like(h)
    for e in range(N_EXPERTS):
        y = y + gates[..., e:e + 1] * swiglu(h, wg[e], wu[e], wd[e])
    return y


def setup_inputs(seed: int = 0) -> dict:
    key = jax.random.key(seed)
    ks = jax.random.split(key, 24)
    n_dense = (DEPTH + 1) // 2
    n_moe = DEPTH // 2
    f32 = jnp.float32

    def nrm(k, shape, scale):
        return jax.random.normal(k, shape, f32) * scale

    return {
        "x": nrm(ks[0], (BATCH, SEQ, D_MODEL), 1.0),
        "c": nrm(ks[1], (BATCH, D_MODEL), 1.0),
        "ctx": nrm(ks[2], (BATCH, CTX_LEN, D_MODEL), 1.0),
        "c_ctx": nrm(ks[3], (D_MODEL,), 1.0),
        "w_mod": nrm(ks[4], (DEPTH, D_MODEL, 6 * D_MODEL), 0.5 * D_MODEL ** -0.5),
        "b_mod": nrm(ks[5], (DEPTH, 6 * D_MODEL), 0.02),
        "norm1_g": 1.0 + nrm(ks[6], (DEPTH, D_MODEL), 0.05),
        "norm2_g": 1.0 + nrm(ks[7], (DEPTH, D_MODEL), 0.05),
        "w_in": nrm(ks[8], (DEPTH, D_MODEL, D_IN), D_MODEL ** -0.5),
        "diff_lambda": nrm(ks[9], (DEPTH, 4, HEAD_DIM), 0.1),
        "diff_subln_g": 1.0 + nrm(ks[10], (DEPTH, 2 * HEAD_DIM), 0.05),
        "qk_norm_g": 1.0 + nrm(ks[11], (DEPTH, 2, HEAD_DIM), 0.05),
        "sink_logits": nrm(ks[12], (DEPTH, C_HEADS), 0.5),
        "w_branch": nrm(ks[13], (DEPTH, N_BRANCH, BRANCH_W, D_MODEL), BRANCH_W ** -0.5),
        "w_out": nrm(ks[14], (DEPTH, D_MODEL, D_MODEL), D_MODEL ** -0.5),
        "ffn_w_gate": nrm(ks[15], (n_dense, D_MODEL, D_FF), D_MODEL ** -0.5),
        "ffn_w_up": nrm(ks[16], (n_dense, D_MODEL, D_FF), D_MODEL ** -0.5),
        "ffn_w_down": nrm(ks[17], (n_dense, D_FF, D_MODEL), D_FF ** -0.5),
        "router_w": nrm(ks[18], (n_moe, D_MODEL, N_EXPERTS), D_MODEL ** -0.5),
        "moe_w_gate": nrm(ks[19], (n_moe, N_EXPERTS, D_MODEL, D_FF_EXPERT), D_MODEL ** -0.5),
        "moe_w_up": nrm(ks[20], (n_moe, N_EXPERTS, D_MODEL, D_FF_EXPERT), D_MODEL ** -0.5),
        "moe_w_down": nrm(ks[21], (n_moe, N_EXPERTS, D_FF_EXPERT, D_MODEL), D_FF_EXPERT ** -0.5),
        "final_g": 1.0 + nrm(ks[22], (D_MODEL,), 0.05),
    }


def reference(x, c, ctx, c_ctx, w_mod, b_mod, norm1_g, norm2_g, w_in, diff_lambda, diff_subln_g,
              qk_norm_g, sink_logits, w_branch, w_out, ffn_w_gate, ffn_w_up, ffn_w_down,
              router_w, moe_w_gate, moe_w_up, moe_w_down, final_g):
    B, S, _ = x.shape
    rows = S // GRID_W
    cos, sin = axial_rope_tables(rows, x.dtype)
    silu_c = jax.nn.silu(c)
    silu_cc = jax.nn.silu(c_ctx)
    h_ctx = ctx
    band_pad = ((0, 0), (WINDOW, WINDOW), (0, 0), (0, 0))
    for l in range(DEPTH):
        last = l == DEPTH - 1
        mx = (silu_c @ w_mod[l] + b_mod[l])[:, None, :]
        mc = silu_cc @ w_mod[l] + b_mod[l]
        sh1, sc1, g1, sh2, sc2, g2 = jnp.split(mx, 6, axis=-1)
        csh1, csc1, cg1, csh2, csc2, cg2 = jnp.split(mc, 6, axis=-1)

        px = split_projection(adaln(x, norm1_g[l], sh1, sc1) @ w_in[l])
        pc = split_projection(adaln(h_ctx, norm1_g[l], csh1, csc1) @ w_in[l])

        lam_init = 0.8 - 0.6 * math.exp(-0.3 * l)
        lp = diff_lambda[l].astype(jnp.float32)
        lam = jnp.exp(jnp.sum(lp[0] * lp[1])) - jnp.exp(jnp.sum(lp[2] * lp[3])) + lam_init
        ka = jnp.concatenate([apply_rope(px["ak"], cos, sin), pc["ak"]], axis=1)
        va = jnp.concatenate([px["av"], pc["av"]], axis=1)
        oa = sweep_query_blocks(lambda q, b: diff_attention(q, ka, va, lam),
                                apply_rope(px["aq"], cos, sin))

        gq, gk = qk_norm_g[l, 0], qk_norm_g[l, 1]
        kb_ctx = rmsnorm(pc["bk"], gk)
        kb = jnp.concatenate([apply_rope(rmsnorm(px["bk"], gk), cos, sin), kb_ctx], axis=1)
        vb = jnp.concatenate([px["bv"], pc["bv"]], axis=1)
        ob = sweep_query_blocks(lambda q, b: gqa_attention(q, kb, vb),
                                apply_rope(rmsnorm(px["bq"], gq), cos, sin))

        sink = sink_logits[l].reshape(C_KV, C_HEADS // C_KV)
        kc = jnp.pad(apply_rope(px["ck"], cos, sin), band_pad)
        vc = jnp.pad(px["cv"], band_pad)
        oc = sweep_query_blocks(
            lambda q, b: window_block(q, b, kc, vc, pc["ck"], pc["cv"], sink, S),
            apply_rope(px["cq"], cos, sin))

        od = fourier_mix(px["fz"])

        branches = [diff_head_norm(oa, diff_subln_g[l], lam_init), flat(ob), flat(oc), od]
        x = x + g1 * merge_branches(branches, px["gates"], w_branch[l], w_out[l])

        if not last:
            oa_c = diff_attention(pc["aq"], pc["ak"], pc["av"], lam)
            ob_c = gqa_attention(rmsnorm(pc["bq"], gq), kb_ctx, pc["bv"])
            oc_c = sink_attention(pc["cq"], pc["ck"], pc["cv"], sink)
            od_c = fourier_mix(pc["fz"])
            branches_c = [diff_head_norm(oa_c, diff_subln_g[l], lam_init), flat(ob_c), flat(oc_c), od_c]
            h_ctx = h_ctx + cg1 * merge_branches(branches_c, pc["gates"], w_branch[l], w_out[l])

        i = l // 2
        if l % 2 == 0:
            ffn = lambda h: swiglu(h, ffn_w_gate[i], ffn_w_up[i], ffn_w_down[i])
        else:
            ffn = lambda h: moe_swiglu(h, router_w[i], moe_w_gate[i], moe_w_up[i], moe_w_down[i])
        x = x + g2 * ffn(adaln(x, norm2_g[l], sh2, sc2))
        if not last:
            h_ctx = h_ctx + cg2 * ffn(adaln(h_ctx, norm2_g[l], csh2, csc2))
    return rmsnorm(x, final_g)
```

```python
import functools
import math

import numpy as np
import jax
import jax.numpy as jnp
from jax import lax
from jax.experimental import pallas as pl
from jax.experimental.pallas import tpu as pltpu

F32 = jnp.float32
BF16 = jnp.bfloat16

GRID_W = 64
HEAD_DIM = 64
ROPE_THETA = 10000.0
WINDOW = 128
A_HEADS = 4
B_HEADS = 8
B_KV = 2
C_HEADS = 8
C_KV = 2
F_GROUPS = 8
F_GROUP_DIM = 64
BRANCH_W = 512
N_BRANCH = 4
N_EXPERTS = 8
EPS = 1e-6
SUBLN_EPS = 1e-5
NEG_INF = -1e30

LANES = 128
ROW_TILE = 256
MOD_ROWS = 16
VMEM_LIMIT = 56 * 1024 * 1024

_OFF = dict(aq=0, ak=512, av=1024, bq=1536, bk=2048, bv=2176, cq=2304, ck=2816, cv=2944,
            fz=3072, gt=3584, end=7680)
_P = dict(gt=0, aq=4096, ak=4608, av=5120, bq=5632, cq=6144, fz=6656,
          bk=7168, bv=7424, ck=7680, cv=7936, end=8192)


def _params(sem):
    return pltpu.CompilerParams(dimension_semantics=sem, vmem_limit_bytes=VMEM_LIMIT)


def _full(shape):
    n = len(shape)
    return pl.BlockSpec(shape, lambda *_: (0,) * n)


def _silu(x):
    return x / (1.0 + jnp.exp(-x))


def _sigmoid(x):
    return 1.0 / (1.0 + jnp.exp(-x))


def _adaln(x, g, shift, scale):
    y = x * lax.rsqrt(jnp.mean(x * x, axis=-1, keepdims=True) + EPS) * g
    return y * (1.0 + scale) + shift


def _mod_kernel(c_ref, w_ref, b_ref, o_ref):
    s = _silu(c_ref[...]).astype(BF16)
    o_ref[...] = jnp.dot(s, w_ref[...].astype(BF16), preferred_element_type=F32) + b_ref[...]


def _modulation(cvec, w_mod, b_mod):
    depth, d, n = w_mod.shape
    tn = 1536 if n % 1536 == 0 else n
    return pl.pallas_call(
        _mod_kernel,
        grid=(depth, n // tn),
        in_specs=[_full((MOD_ROWS, d)),
                  pl.BlockSpec((None, d, tn), lambda l, j: (l, 0, j)),
                  pl.BlockSpec((None, 1, tn), lambda l, j: (l, 0, j))],
        out_specs=pl.BlockSpec((None, MOD_ROWS, tn), lambda l, j: (l, 0, j)),
        out_shape=jax.ShapeDtypeStruct((depth, MOD_ROWS, n), F32),
        compiler_params=_params(("arbitrary", "arbitrary")),
        name="modulation",
    )(cvec, w_mod, b_mod.reshape(depth, 1, n))


def _rope(p, cos, sin_lo, sin_hi):
    outs = []
    for c in range(p.shape[1] // LANES):
        xc = p[:, c * LANES:(c + 1) * LANES]
        outs.append(xc * cos + pltpu.roll(xc, LANES - 16, 1) * sin_lo + pltpu.roll(xc, 16, 1) * sin_hi)
    return outs[0] if len(outs) == 1 else jnp.concatenate(outs, axis=1)


def _group_rmsnorm(y, gmat, g):
    ss = jnp.dot((y * y).astype(BF16), gmat, preferred_element_type=F32)
    return y * lax.rsqrt(ss * (1.0 / HEAD_DIM) + EPS) * g


def _inproj_kernel(x_ref, mod_ref, g_ref, w_ref, cos_ref, slo_ref, shi_ref, qg_ref, kg_ref, gmat_ref,
                   c64_ref, s64_ref,
                   gt_ref, aq_ref, akt_ref, av_ref, bq_ref, bkt_ref, bv_ref, cq_ref, ckt_ref, cv_ref,
                   fzc_ref, fzs_ref, *, d):
    h = _adaln(x_ref[...], g_ref[...], mod_ref[:, 0:d], mod_ref[:, d:2 * d]).astype(BF16)
    cos, slo, shi = cos_ref[...], slo_ref[...], shi_ref[...]
    scale = HEAD_DIM ** -0.5

    def proj(name, width):
        return jnp.dot(h, w_ref[:, _P[name]:_P[name] + width], preferred_element_type=F32)

    for c in range(N_BRANCH * d // 512):
        p = jnp.dot(h, w_ref[:, c * 512:(c + 1) * 512], preferred_element_type=F32)
        gt_ref[:, c * 512:(c + 1) * 512] = _sigmoid(p).astype(BF16)

    aq_ref[...] = (_rope(proj("aq", 512), cos, slo, shi) * scale).astype(BF16)
    akt_ref[...] = _rope(proj("ak", 512), cos, slo, shi).T.astype(BF16)
    av_ref[...] = proj("av", 512).astype(BF16)

    bq = _group_rmsnorm(proj("bq", 512), gmat_ref[...], qg_ref[...])
    bq_ref[...] = (_rope(bq, cos, slo, shi) * scale).astype(BF16)
    bk = _group_rmsnorm(proj("bk", 256), gmat_ref[0:256, 0:256], kg_ref[...])
    bkt_ref[...] = _rope(bk, cos, slo, shi).T.astype(BF16)
    bv_ref[...] = proj("bv", 256).astype(BF16)

    cq_ref[...] = (_rope(proj("cq", 512), cos, slo, shi) * scale).astype(BF16)
    ckt_ref[...] = _rope(proj("ck", 256), cos, slo, shi).T.astype(BF16)
    cv_ref[...] = proj("cv", 256).astype(BF16)

    fz = proj("fz", 512).astype(BF16)
    fzc_ref[...] = jnp.dot(fz, c64_ref[...], preferred_element_type=F32).astype(BF16)
    fzs_ref[...] = jnp.dot(fz, s64_ref[...], preferred_element_type=F32).astype(BF16)


def _inproj(xs, mod_l, g, w_p, tabs, qg, kg, consts, n_lat_tiles):
    b, nt, d = xs.shape
    t = ROW_TILE
    n_mod = mod_l.shape[-1]
    row = lambda i, r: (i, r, 0)
    tr = lambda i, r: (i, 0, r)
    tab = pl.BlockSpec((t, LANES), lambda i, r: (r, 0))

    def out(width):
        return pl.BlockSpec((None, t, width), row), jax.ShapeDtypeStruct((b, nt, width), BF16)

    def out_t(width):
        return pl.BlockSpec((None, width, t), tr), jax.ShapeDtypeStruct((b, width, nt), BF16)

    outs = [out(N_BRANCH * d), out(512), out_t(512), out(512), out(512), out_t(256), out(256),
            out(512), out_t(256), out(256), out(512), out(512)]
    return pl.pallas_call(
        functools.partial(_inproj_kernel, d=d),
        grid=(b, nt // t),
        in_specs=[pl.BlockSpec((None, t, d), row),
                  pl.BlockSpec((None, 1, n_mod), lambda i, r: (jnp.where(r < n_lat_tiles, i, b), 0, 0)),
                  _full((1, d)),
                  pl.BlockSpec(w_p.shape, lambda i, r: (0, 0), pipeline_mode=pl.Buffered(1)),
                  tab, tab, tab,
                  _full((1, 512)), _full((1, 256)), _full((512, 512)), _full((512, 512)), _full((512, 512))],
        out_specs=[o[0] for o in outs],
        out_shape=[o[1] for o in outs],
        compiler_params=_params(("arbitrary", "arbitrary")),
        name="inproj",
    )(xs, mod_l, g, w_p, tabs[0], tabs[1], tabs[2], qg, kg, consts["gmat"], consts["c64"], consts["s64"])


def _softmax_parts(s):
    m = jnp.max(s, axis=-1, keepdims=True)
    e = jnp.exp(s - m)
    return e, jnp.sum(e, axis=-1, keepdims=True)


def _half_masks(shape):
    lane = lax.broadcasted_iota(jnp.int32, shape, 1)
    return lane < HEAD_DIM, lane >= HEAD_DIM


def _attn_a_kernel(q_ref, kt_ref, v_ref, lp_ref, g_ref, o_ref, *, n_lat_tiles, s_lat, lam_init):
    r = pl.program_id(1)
    lp = lp_ref[...]
    lam = (jnp.exp(jnp.sum(lp[0:1] * lp[1:2], axis=-1, keepdims=True))
           - jnp.exp(jnp.sum(lp[2:3] * lp[3:4], axis=-1, keepdims=True)) + lam_init)
    g = g_ref[...] * (1.0 - lam_init)
    nt = kt_ref.shape[1]

    def body(k0):
        lo, hi = _half_masks((q_ref.shape[0], LANES))
        for hd in range(A_HEADS):
            cs = slice(hd * LANES, (hd + 1) * LANES)
            q = q_ref[:, cs]
            kt = kt_ref[cs, k0:nt]
            e1, l1 = _softmax_parts(jnp.dot(jnp.where(lo, q, 0), kt, preferred_element_type=F32))
            e2, l2 = _softmax_parts(jnp.dot(jnp.where(hi, q, 0), kt, preferred_element_type=F32))
            a = (e1 * (1.0 / l1) - e2 * (lam / l2)).astype(BF16)
            o = jnp.dot(a, v_ref[k0:nt, cs], preferred_element_type=F32)
            o = o * lax.rsqrt(jnp.mean(o * o, axis=-1, keepdims=True) + SUBLN_EPS) * g
            o_ref[:, cs] = o.astype(BF16)

    pl.when(r < n_lat_tiles)(lambda: body(0))
    pl.when(r >= n_lat_tiles)(lambda: body(s_lat))


def _attn_b_kernel(q_ref, kt_ref, v_ref, o_ref, *, n_lat_tiles, s_lat):
    r = pl.program_id(1)
    nt = kt_ref.shape[1]

    def body(k0):
        lo, hi = _half_masks((q_ref.shape[0], LANES))
        for pair in range(B_HEADS // 2):
            kv = pair // (B_HEADS // B_KV // 2)
            ks = slice(kv * LANES, (kv + 1) * LANES)
            q = q_ref[:, pair * LANES:(pair + 1) * LANES]
            kt = kt_ref[ks, k0:nt]
            v = v_ref[k0:nt, ks]
            halves = []
            for msk in (lo, hi):
                e, l = _softmax_parts(jnp.dot(jnp.where(msk, q, 0), kt, preferred_element_type=F32))
                halves.append(jnp.dot(e.astype(BF16), v, preferred_element_type=F32) * (1.0 / l))
            o_ref[:, pair * LANES:(pair + 1) * LANES] = jnp.where(lo, halves[0], halves[1]).astype(BF16)

    pl.when(r < n_lat_tiles)(lambda: body(0))
    pl.when(r >= n_lat_tiles)(lambda: body(s_lat))


def _attn_c_kernel(sink_ref, q_ref, ktl_ref, ktm_ref, ktr_ref, ktc_ref, vl_ref, vm_ref, vr_ref, vc_ref, o_ref,
                   *, n_lat_tiles, s_lat):
    r = pl.program_id(1)
    t = q_ref.shape[0]
    n_band = t + 2 * WINDOW
    n_keys = n_band + ktc_ref.shape[1]
    u = lax.broadcasted_iota(jnp.int32, (t, n_keys), 1)
    row = lax.broadcasted_iota(jnp.int32, (t, n_keys), 0)
    j = r * t - WINDOW + u
    in_band = (u >= row) & (u <= row + 2 * WINDOW) & (j >= 0) & (j < s_lat) & (r < n_lat_tiles)
    valid = in_band | (u >= n_band)
    lo, hi = _half_masks((t, LANES))
    for pair in range(C_HEADS // 2):
        kv = pair // (C_HEADS // C_KV // 2)
        ks = slice(kv * LANES, (kv + 1) * LANES)
        q = q_ref[:, pair * LANES:(pair + 1) * LANES]
        kt = jnp.concatenate([ktl_ref[ks, :], ktm_ref[ks, :], ktr_ref[ks, :], ktc_ref[ks, :]], axis=1)
        v = jnp.concatenate([vl_ref[:, ks], vm_ref[:, ks], vr_ref[:, ks], vc_ref[:, ks]], axis=0)
        halves = []
        for half, msk in enumerate((lo, hi)):
            sink = sink_ref[2 * pair + half]
            s = jnp.dot(jnp.where(msk, q, 0), kt, preferred_element_type=F32)
            s = jnp.where(valid, s, NEG_INF)
            m = jnp.maximum(jnp.max(s, axis=-1, keepdims=True), sink)
            e = jnp.exp(s - m)
            l = jnp.sum(e, axis=-1, keepdims=True) + jnp.exp(sink - m)
            halves.append(jnp.dot(e.astype(BF16), v, preferred_element_type=F32) * (1.0 / l))
        o_ref[:, pair * LANES:(pair + 1) * LANES] = jnp.where(lo, halves[0], halves[1]).astype(BF16)


def _attention(p, lp, subln_g, sink, n_lat_tiles, s_lat, lam_init):
    (_, aq, akt, av, bq, bkt, bv, cq, ckt, cv, _, _) = p
    b, nt, _ = aq.shape
    t = ROW_TILE
    grid = (b, nt // t)
    row = lambda i, r: (i, r, 0)
    whole = lambda i, r: (i, 0, 0)
    sem = _params(("arbitrary", "arbitrary"))
    o_spec = pl.BlockSpec((None, t, 512), row)
    o_shape = jax.ShapeDtypeStruct((b, nt, 512), BF16)

    oa = pl.pallas_call(
        functools.partial(_attn_a_kernel, n_lat_tiles=n_lat_tiles, s_lat=s_lat, lam_init=lam_init),
        grid=grid,
        in_specs=[pl.BlockSpec((None, t, 512), row), pl.BlockSpec((None, 512, nt), whole),
                  pl.BlockSpec((None, nt, 512), whole), _full((4, HEAD_DIM)), _full((1, 2 * HEAD_DIM))],
        out_specs=o_spec, out_shape=o_shape, compiler_params=sem, name="attn_diff",
    )(aq, akt, av, lp, subln_g)

    ob = pl.pallas_call(
        functools.partial(_attn_b_kernel, n_lat_tiles=n_lat_tiles, s_lat=s_lat),
        grid=grid,
        in_specs=[pl.BlockSpec((None, t, 512), row), pl.BlockSpec((None, 256, nt), whole),
                  pl.BlockSpec((None, nt, 256), whole)],
        out_specs=o_spec, out_shape=o_shape, compiler_params=sem, name="attn_gqa",
    )(bq, bkt, bv)

    wpt = t // WINDOW
    last = s_lat // WINDOW - 1
    left = lambda r: jnp.clip(r * wpt - 1, 0, last)
    mid = lambda r: jnp.minimum(r, n_lat_tiles - 1)
    right = lambda r: jnp.clip((r + 1) * wpt, 0, last)
    n_ctx = nt - s_lat
    ctx_blk = s_lat // n_ctx
    oc = pl.pallas_call(
        functools.partial(_attn_c_kernel, n_lat_tiles=n_lat_tiles, s_lat=s_lat),
        grid=grid,
        in_specs=[pl.BlockSpec(memory_space=pltpu.SMEM),
                  pl.BlockSpec((None, t, 512), row),
                  pl.BlockSpec((None, 256, WINDOW), lambda i, r: (i, 0, left(r))),
                  pl.BlockSpec((None, 256, t), lambda i, r: (i, 0, mid(r))),
                  pl.BlockSpec((None, 256, WINDOW), lambda i, r: (i, 0, right(r))),
                  pl.BlockSpec((None, 256, n_ctx), lambda i, r: (i, 0, ctx_blk)),
                  pl.BlockSpec((None, WINDOW, 256), lambda i, r: (i, left(r), 0)),
                  pl.BlockSpec((None, t, 256), lambda i, r: (i, mid(r), 0)),
                  pl.BlockSpec((None, WINDOW, 256), lambda i, r: (i, right(r), 0)),
                  pl.BlockSpec((None, n_ctx, 256), lambda i, r: (i, ctx_blk, 0))],
        out_specs=o_spec, out_shape=o_shape, compiler_params=sem, name="attn_window",
    )(sink, cq, ckt, ckt, ckt, ckt, cv, cv, cv, cv)
    return oa, ob, oc


def _fourier_kernel(cn_ref, sn_ref, cc_ref, sc_ref, zc_ref, zs_ref, o_ref, *, n_lat_tiles, s_lat):
    r = pl.program_id(1)
    nt = zc_ref.shape[0]

    @pl.when(r < n_lat_tiles)
    def _():
        y = (jnp.dot(cn_ref[...], zc_ref[0:s_lat, :], preferred_element_type=F32)
             - jnp.dot(sn_ref[...], zs_ref[0:s_lat, :], preferred_element_type=F32))
        o_ref[...] = y.astype(BF16)

    @pl.when(r >= n_lat_tiles)
    def _():
        y = (jnp.dot(cc_ref[...], zc_ref[s_lat:nt, :], preferred_element_type=F32)
             - jnp.dot(sc_ref[...], zs_ref[s_lat:nt, :], preferred_element_type=F32))
        o_ref[...] = y.astype(BF16)


def _fourier(fzc, fzs, consts, n_lat_tiles, s_lat):
    b, nt, w = fzc.shape
    t = ROW_TILE
    n_ctx = nt - s_lat
    lat_rows = lambda i, r: (jnp.minimum(r, n_lat_tiles - 1), 0)
    ctx_rows = lambda i, r: (jnp.maximum(r - n_lat_tiles, 0), 0)
    whole = lambda i, r: (i, 0, 0)
    return pl.pallas_call(
        functools.partial(_fourier_kernel, n_lat_tiles=n_lat_tiles, s_lat=s_lat),
        grid=(b, nt // t),
        in_specs=[pl.BlockSpec((t, s_lat), lat_rows), pl.BlockSpec((t, s_lat), lat_rows),
                  pl.BlockSpec((t, n_ctx), ctx_rows), pl.BlockSpec((t, n_ctx), ctx_rows),
                  pl.BlockSpec((None, nt, w), whole), pl.BlockSpec((None, nt, w), whole)],
        out_specs=pl.BlockSpec((None, t, w), lambda i, r: (i, r, 0)),
        out_shape=jax.ShapeDtypeStruct((b, nt, w), BF16),
        compiler_params=_params(("arbitrary", "arbitrary")),
        name="fourier",
    )(consts["cn"], consts["sn"], consts["cc"], consts["sc"], fzc, fzs)


def _merge_kernel(x_ref, mod_ref, gt_ref, oa_ref, ob_ref, oc_ref, od_ref, wbr_ref, wo_ref, o_ref, *, d):
    merged = None
    for i, br in enumerate((oa_ref, ob_ref, oc_ref, od_ref)):
        y = jnp.dot(br[...], wbr_ref[i], preferred_element_type=F32)
        y = gt_ref[:, i * d:(i + 1) * d].astype(F32) * y
        merged = y if merged is None else merged + y
    out = jnp.dot(merged.astype(BF16), wo_ref[...], preferred_element_type=F32)
    o_ref[...] = x_ref[...] + mod_ref[:, 2 * d:3 * d] * out


def _merge(xs, mod_l, gates, branches, w_br, w_o, n_lat_tiles):
    b, nt, d = xs.shape
    t = ROW_TILE
    row = lambda i, r: (i, r, 0)
    br_spec = pl.BlockSpec((None, t, BRANCH_W), row)
    return pl.pallas_call(
        functools.partial(_merge_kernel, d=d),
        grid=(b, nt // t),
        in_specs=[pl.BlockSpec((None, t, d), row),
                  pl.BlockSpec((None, 1, mod_l.shape[-1]), lambda i, r: (jnp.where(r < n_lat_tiles, i, b), 0, 0)),
                  pl.BlockSpec((None, t, N_BRANCH * d), row),
                  br_spec, br_spec, br_spec, br_spec,
                  _full(w_br.shape), _full(w_o.shape)],
        out_specs=pl.BlockSpec((None, t, d), row),
        out_shape=jax.ShapeDtypeStruct(xs.shape, F32),
        compiler_params=_params(("arbitrary", "arbitrary")),
        name="merge",
    )(xs, mod_l, gates, *branches, w_br, w_o)


def _ffn_kernel(x_ref, mod_ref, g_ref, wg_ref, wu_ref, wd_ref, o_ref, *, d):
    x = x_ref[...]
    h = _adaln(x, g_ref[...], mod_ref[:, 3 * d:4 * d], mod_ref[:, 4 * d:5 * d]).astype(BF16)
    a = jnp.dot(h, wg_ref[...], preferred_element_type=F32)
    u = jnp.dot(h, wu_ref[...], preferred_element_type=F32)
    act = (_silu(a) * u).astype(BF16)
    y = jnp.dot(act, wd_ref[...], preferred_element_type=F32)
    o_ref[...] = x + mod_ref[:, 5 * d:6 * d] * y


def _ffn(xs, mod_l, g, wg, wu, wd, n_lat_tiles):
    b, nt, d = xs.shape
    t = ROW_TILE
    row = lambda i, r: (i, r, 0)
    const = lambda shape: pl.BlockSpec(shape, lambda i, r: (0, 0), pipeline_mode=pl.Buffered(1))
    return pl.pallas_call(
        functools.partial(_ffn_kernel, d=d),
        grid=(b, nt // t),
        in_specs=[pl.BlockSpec((None, t, d), row),
                  pl.BlockSpec((None, 1, mod_l.shape[-1]), lambda i, r: (jnp.where(r < n_lat_tiles, i, b), 0, 0)),
                  _full((1, d)), const(wg.shape), const(wu.shape), const(wd.shape)],
        out_specs=pl.BlockSpec((None, t, d), row),
        out_shape=jax.ShapeDtypeStruct(xs.shape, F32),
        compiler_params=_params(("arbitrary", "arbitrary")),
        name="ffn_dense",
    )(xs, mod_l, g, wg, wu, wd)


def _router_kernel(x_ref, mod_ref, g_ref, wr_ref, h_ref, gate_ref, *, d):
    h = _adaln(x_ref[...], g_ref[...], mod_ref[:, 3 * d:4 * d], mod_ref[:, 4 * d:5 * d])
    h_ref[...] = h.astype(BF16)
    logits = jnp.dot(h, wr_ref[...], preferred_element_type=F32, precision=lax.Precision.HIGHEST)
    lane = lax.broadcasted_iota(jnp.int32, logits.shape, 1)
    logits = jnp.where(lane < N_EXPERTS, logits, -jnp.inf)
    m1 = jnp.max(logits, axis=-1, keepdims=True)
    i1 = jnp.min(jnp.where(logits == m1, lane, LANES), axis=-1, keepdims=True)
    rest = jnp.where(lane == i1, -jnp.inf, logits)
    m2 = jnp.max(rest, axis=-1, keepdims=True)
    i2 = jnp.min(jnp.where(rest == m2, lane, LANES), axis=-1, keepdims=True)
    e2 = jnp.exp(m2 - m1)
    w1 = 1.0 / (1.0 + e2)
    gate_ref[...] = jnp.where(lane == i1, w1, 0.0) + jnp.where(lane == i2, e2 * w1, 0.0)


def _router(xs, mod_l, g, w_r, n_lat_tiles):
    b, nt, d = xs.shape
    t = ROW_TILE
    row = lambda i, r: (i, r, 0)
    return pl.pallas_call(
        functools.partial(_router_kernel, d=d),
        grid=(b, nt // t),
        in_specs=[pl.BlockSpec((None, t, d), row),
                  pl.BlockSpec((None, 1, mod_l.shape[-1]), lambda i, r: (jnp.where(r < n_lat_tiles, i, b), 0, 0)),
                  _full((1, d)), _full(w_r.shape)],
        out_specs=[pl.BlockSpec((None, t, d), row), pl.BlockSpec((None, t, LANES), row)],
        out_shape=[jax.ShapeDtypeStruct(xs.shape, BF16), jax.ShapeDtypeStruct((b, nt, LANES), F32)],
        compiler_params=_params(("arbitrary", "arbitrary")),
        name="moe_router",
    )(xs, mod_l, g, w_r)


def _moe_kernel(h_ref, gate_ref, wg_ref, wu_ref, wd_ref, o_ref):
    e, j = pl.program_id(1), pl.program_id(2)

    @pl.when((e == 0) & (j == 0))
    def _():
        o_ref[...] = jnp.zeros_like(o_ref)

    h = h_ref[...]
    a = jnp.dot(h, wg_ref[...].astype(BF16), preferred_element_type=F32)
    u = jnp.dot(h, wu_ref[...].astype(BF16), preferred_element_type=F32)
    act = (_silu(a) * u * gate_ref[...]).astype(BF16)
    o_ref[...] += jnp.dot(act, wd_ref[...].astype(BF16), preferred_element_type=F32)


def _moe_tile(m):
    for tm in (768, 1024, 512, 256):
        if m % tm == 0:
            return tm
    return m


def _moe(h2, gate_cols, wg, wu, wd):
    m, d = h2.shape
    n_e, _, dff = wg.shape
    tm = _moe_tile(m)
    tf = 512 if dff % 512 == 0 else dff
    return pl.pallas_call(
        _moe_kernel,
        grid=(m // tm, n_e, dff // tf),
        in_specs=[pl.BlockSpec((tm, d), lambda i, e, j: (i, 0)),
                  pl.BlockSpec((None, tm, 1), lambda i, e, j: (e, i, 0)),
                  pl.BlockSpec((None, d, tf), lambda i, e, j: (e, 0, j)),
                  pl.BlockSpec((None, d, tf), lambda i, e, j: (e, 0, j)),
                  pl.BlockSpec((None, tf, d), lambda i, e, j: (e, j, 0))],
        out_specs=pl.BlockSpec((tm, d), lambda i, e, j: (i, 0)),
        out_shape=jax.ShapeDtypeStruct((m, d), F32),
        compiler_params=_params(("arbitrary", "arbitrary", "arbitrary")),
        name="moe_experts",
    )(h2, gate_cols, wg, wu, wd)


def _residual_kernel(x_ref, y_ref, mod_ref, o_ref, *, d):
    o_ref[...] = x_ref[...] + mod_ref[:, 5 * d:6 * d] * y_ref[...]


def _residual_final_kernel(x_ref, y_ref, mod_ref, g_ref, o_ref, *, d):
    x = x_ref[...] + mod_ref[:, 5 * d:6 * d] * y_ref[...]
    o_ref[...] = x * lax.rsqrt(jnp.mean(x * x, axis=-1, keepdims=True) + EPS) * g_ref[...]


def _residual(xs, y, mod_l, n_lat_tiles, final_g=None):
    b, nt, d = xs.shape
    t = ROW_TILE
    row = lambda i, r: (i, r, 0)
    mod_spec = pl.BlockSpec((None, 1, mod_l.shape[-1]), lambda i, r: (jnp.where(r < n_lat_tiles, i, b), 0, 0))
    tile = pl.BlockSpec((None, t, d), row)
    if final_g is None:
        return pl.pallas_call(
            functools.partial(_residual_kernel, d=d), grid=(b, nt // t),
            in_specs=[tile, tile, mod_spec], out_specs=tile,
            out_shape=jax.ShapeDtypeStruct(xs.shape, F32),
            compiler_params=_params(("arbitrary", "arbitrary")), name="moe_residual",
        )(xs, y, mod_l)
    return pl.pallas_call(
        functools.partial(_residual_final_kernel, d=d), grid=(b, n_lat_tiles),
        in_specs=[tile, tile, mod_spec, _full((1, d))], out_specs=tile,
        out_shape=jax.ShapeDtypeStruct((b, n_lat_tiles * t, d), F32),
        compiler_params=_params(("arbitrary", "arbitrary")), name="moe_residual_final",
    )(xs, y, mod_l, final_g)


def _rope_tables(s_lat, n_ctx):
    rows = s_lat // GRID_W
    pos_row = jnp.repeat(jnp.arange(rows, dtype=jnp.int32), GRID_W).astype(F32)
    pos_col = jnp.tile(jnp.arange(GRID_W, dtype=jnp.int32), rows).astype(F32)
    quarter = HEAD_DIM // 4
    inv_freq = ROPE_THETA ** (-jnp.arange(quarter, dtype=F32) / quarter)
    ang_r = pos_row[:, None] * inv_freq
    ang_c = pos_col[:, None] * inv_freq
    ang = jnp.concatenate([ang_r, ang_r, ang_c, ang_c], axis=-1)
    cos, sin = jnp.cos(ang), jnp.sin(ang)
    first = (np.arange(HEAD_DIM) % (2 * quarter)) < quarter
    sin_lo = jnp.where(first, -sin, 0.0)
    sin_hi = jnp.where(first, 0.0, sin)

    def widen(tab, ctx_fill):
        tab = jnp.concatenate([tab, jnp.full((n_ctx, HEAD_DIM), ctx_fill, F32)], axis=0)
        return jnp.tile(tab, (1, LANES // HEAD_DIM))

    return widen(cos, 1.0), widen(sin_lo, 0.0), widen(sin_hi, 0.0)


def _dft_tables(n):
    k = np.arange(n, dtype=np.int64)
    ang = 2.0 * np.pi * ((k[:, None] * k[None, :]) % n).astype(np.float64) / n
    return np.cos(ang), np.sin(ang)


def _constants(s_lat, n_ctx):
    c64, s64 = _dft_tables(F_GROUP_DIM)
    eye = np.eye(F_GROUPS)
    cn, sn = _dft_tables(s_lat)
    cc, sc = _dft_tables(n_ctx)
    bf = lambda a: jnp.asarray(a, dtype=F32).astype(BF16)
    return dict(
        gmat=bf(np.kron(np.eye(512 // HEAD_DIM), np.ones((HEAD_DIM, HEAD_DIM)))),
        c64=bf(np.kron(eye, c64) / math.sqrt(F_GROUP_DIM)),
        s64=bf(np.kron(eye, s64) / math.sqrt(F_GROUP_DIM)),
        cn=bf(cn / math.sqrt(s_lat)), sn=bf(sn / math.sqrt(s_lat)),
        cc=bf(cc / math.sqrt(n_ctx)), sc=bf(sc / math.sqrt(n_ctx)),
    )


def _dup_heads(w, off, n_heads):
    cols = []
    for hd in range(n_heads):
        blk = w[:, off + hd * HEAD_DIM: off + (hd + 1) * HEAD_DIM]
        cols += [blk, blk]
    return cols


def _relayout_w_in(w):
    cols = [w[:, _OFF["gt"]:_OFF["end"]], w[:, _OFF["aq"]:_OFF["bq"]], w[:, _OFF["bq"]:_OFF["bk"]],
            w[:, _OFF["cq"]:_OFF["ck"]], w[:, _OFF["fz"]:_OFF["gt"]]]
    cols += _dup_heads(w, _OFF["bk"], B_KV) + _dup_heads(w, _OFF["bv"], B_KV)
    cols += _dup_heads(w, _OFF["ck"], C_KV) + _dup_heads(w, _OFF["cv"], C_KV)
    return jnp.concatenate(cols, axis=1).astype(BF16)


def kernel(x, c, ctx, c_ctx, w_mod, b_mod, norm1_g, norm2_g, w_in, diff_lambda, diff_subln_g, qk_norm_g,
           sink_logits, w_branch, w_out, ffn_w_gate, ffn_w_up, ffn_w_down, router_w, moe_w_gate, moe_w_up,
           moe_w_down, final_g):
    b, s_lat, d = x.shape
    n_ctx = ctx.shape[1]
    depth = w_mod.shape[0]
    nt = s_lat + n_ctx
    t = ROW_TILE
    assert s_lat % t == 0 and n_ctx % t == 0 and s_lat % n_ctx == 0 and b < MOD_ROWS
    assert w_in.shape[-1] == _OFF["end"] and d * N_BRANCH == _OFF["end"] - _OFF["gt"]
    n_lat_tiles = s_lat // t

    xs = jnp.concatenate([x, ctx], axis=1)
    cvec = jnp.concatenate([c, c_ctx[None], jnp.zeros((MOD_ROWS - b - 1, d), F32)], axis=0)
    mod = _modulation(cvec, w_mod, b_mod).reshape(depth, MOD_ROWS, 1, 6 * d)
    tabs = _rope_tables(s_lat, n_ctx)
    consts = _constants(s_lat, n_ctx)

    out = None
    for l in range(depth):
        last = l == depth - 1
        mod_l = mod[l]
        p = _inproj(xs, mod_l, norm1_g[l][None], _relayout_w_in(w_in[l]), tabs,
                    jnp.tile(qk_norm_g[l, 0], 512 // HEAD_DIM)[None], jnp.tile(qk_norm_g[l, 1], 256 // HEAD_DIM)[None],
                    consts, n_lat_tiles)
        lam_init = 0.8 - 0.6 * math.exp(-0.3 * l)
        oa, ob, oc = _attention(p, diff_lambda[l], diff_subln_g[l][None], sink_logits[l],
                                n_lat_tiles, s_lat, lam_init)
        od = _fourier(p[10], p[11], consts, n_lat_tiles, s_lat)
        xs = _merge(xs, mod_l, p[0], (oa, ob, oc, od), w_branch[l].astype(BF16), w_out[l].astype(BF16), n_lat_tiles)

        i = l // 2
        if l % 2 == 0:
            xs = _ffn(xs, mod_l, norm2_g[l][None], ffn_w_gate[i].astype(BF16), ffn_w_up[i].astype(BF16),
                      ffn_w_down[i].astype(BF16), n_lat_tiles)
        else:
            w_r = jnp.pad(router_w[i], ((0, 0), (0, LANES - N_EXPERTS)))
            h2, gates = _router(xs, mod_l, norm2_g[l][None], w_r, n_lat_tiles)
            gate_cols = jnp.transpose(gates[..., :N_EXPERTS].reshape(b * nt, N_EXPERTS))[..., None]
            y = _moe(h2.reshape(b * nt, d), gate_cols, moe_w_gate[i], moe_w_up[i], moe_w_down[i]).reshape(b, nt, d)
            if last:
                out = _residual(xs, y, mod_l, n_lat_tiles, final_g[None])
            else:
                xs = _residual(xs, y, mod_l, n_lat_tiles)
    if out is None:
        raise NotImplementedError("final normalisation is fused into the last (mixture-of-experts) layer")
    return out
```

```python
import functools
import math

import numpy as np
import jax
import jax.numpy as jnp
from jax import lax
from jax.experimental import pallas as pl
from jax.experimental.pallas import tpu as pltpu

F32 = jnp.float32
BF16 = jnp.bfloat16

GRID_W = 64
HEAD_DIM = 64
ROPE_THETA = 10000.0
WINDOW = 128
A_HEADS = 4
B_HEADS = 8
B_KV = 2
C_HEADS = 8
C_KV = 2
F_GROUPS = 8
F_GROUP_DIM = 64
BRANCH_W = 512
N_BRANCH = 4
N_EXPERTS = 8
EPS = 1e-6
SUBLN_EPS = 1e-5
NEG_INF = -1e30

LANES = 128
ROW_TILE = 256
MOE_TILE = 1024
MOE_FF_TILE = 512
MOD_ROWS = 16
VMEM_LIMIT = 56 * 1024 * 1024

_OFF = dict(aq=0, ak=512, av=1024, bq=1536, bk=2048, bv=2176, cq=2304, ck=2816, cv=2944,
            fz=3072, gt=3584, end=7680)
_P = dict(gt=0, aq=4096, ak=4608, av=5120, bq=5632, cq=6144, fz=6656,
          bk=7168, bv=7424, ck=7680, cv=7936, end=8192)


def _params(sem):
    return pltpu.CompilerParams(dimension_semantics=sem, vmem_limit_bytes=VMEM_LIMIT)


def _full(shape):
    n = len(shape)
    return pl.BlockSpec(shape, lambda *_: (0,) * n)


def _silu(x):
    return x / (1.0 + jnp.exp(-x))


def _sigmoid(x):
    return 1.0 / (1.0 + jnp.exp(-x))


def _adaln(x, g, shift, scale):
    y = x * lax.rsqrt(jnp.mean(x * x, axis=-1, keepdims=True) + EPS) * g
    return y * (1.0 + scale) + shift


def _mod_kernel(c_ref, w_ref, b_ref, o_ref):
    s = _silu(c_ref[...]).astype(BF16)
    o_ref[...] = jnp.dot(s, w_ref[...].astype(BF16), preferred_element_type=F32) + b_ref[...]


def _modulation(cvec, w_mod, b_mod):
    depth, d, n = w_mod.shape
    tn = 1536 if n % 1536 == 0 else n
    return pl.pallas_call(
        _mod_kernel,
        grid=(depth, n // tn),
        in_specs=[_full((MOD_ROWS, d)),
                  pl.BlockSpec((None, d, tn), lambda l, j: (l, 0, j)),
                  pl.BlockSpec((None, 1, tn), lambda l, j: (l, 0, j))],
        out_specs=pl.BlockSpec((None, MOD_ROWS, tn), lambda l, j: (l, 0, j)),
        out_shape=jax.ShapeDtypeStruct((depth, MOD_ROWS, n), F32),
        compiler_params=_params(("arbitrary", "arbitrary")),
        name="modulation",
    )(cvec, w_mod, b_mod.reshape(depth, 1, n))


def _rope(p, cos, sin_lo, sin_hi):
    outs = []
    for c in range(p.shape[1] // LANES):
        xc = p[:, c * LANES:(c + 1) * LANES]
        outs.append(xc * cos + pltpu.roll(xc, LANES - 16, 1) * sin_lo + pltpu.roll(xc, 16, 1) * sin_hi)
    return outs[0] if len(outs) == 1 else jnp.concatenate(outs, axis=1)


def _group_rmsnorm(y, gmat, g):
    ss = jnp.dot((y * y).astype(BF16), gmat, preferred_element_type=F32)
    return y * lax.rsqrt(ss * (1.0 / HEAD_DIM) + EPS) * g


def _inproj_kernel(x_ref, mod_ref, g_ref, w_ref, cos_ref, slo_ref, shi_ref, qg_ref, kg_ref, gmat_ref,
                   c64_ref, s64_ref,
                   gt_ref, aq_ref, akt_ref, av_ref, bq_ref, bkt_ref, bv_ref, cq_ref, ckt_ref, cv_ref,
                   fzc_ref, fzs_ref, *, d):
    h = _adaln(x_ref[...], g_ref[...], mod_ref[:, 0:d], mod_ref[:, d:2 * d]).astype(BF16)
    cos, slo, shi = cos_ref[...], slo_ref[...], shi_ref[...]
    scale = HEAD_DIM ** -0.5

    def proj(name, width):
        return jnp.dot(h, w_ref[:, _P[name]:_P[name] + width], preferred_element_type=F32)

    for c in range(N_BRANCH * d // 512):
        p = jnp.dot(h, w_ref[:, c * 512:(c + 1) * 512], preferred_element_type=F32)
        gt_ref[:, c * 512:(c + 1) * 512] = _sigmoid(p).astype(BF16)

    aq_ref[...] = (_rope(proj("aq", 512), cos, slo, shi) * scale).astype(BF16)
    akt_ref[...] = _rope(proj("ak", 512), cos, slo, shi).T.astype(BF16)
    av_ref[...] = proj("av", 512).astype(BF16)

    bq = _group_rmsnorm(proj("bq", 512), gmat_ref[...], qg_ref[...])
    bq_ref[...] = (_rope(bq, cos, slo, shi) * scale).astype(BF16)
    bk = _group_rmsnorm(proj("bk", 256), gmat_ref[0:256, 0:256], kg_ref[...])
    bkt_ref[...] = _rope(bk, cos, slo, shi).T.astype(BF16)
    bv_ref[...] = proj("bv", 256).astype(BF16)

    cq_ref[...] = (_rope(proj("cq", 512), cos, slo, shi) * scale).astype(BF16)
    ckt_ref[...] = _rope(proj("ck", 256), cos, slo, shi).T.astype(BF16)
    cv_ref[...] = proj("cv", 256).astype(BF16)

    fz = proj("fz", 512).astype(BF16)
    fzc_ref[...] = jnp.dot(fz, c64_ref[...], preferred_element_type=F32).astype(BF16)
    fzs_ref[...] = jnp.dot(fz, s64_ref[...], preferred_element_type=F32).astype(BF16)


def _inproj(xs, mod_l, g, w_p, tabs, qg, kg, consts, n_lat_tiles):
    b, nt, d = xs.shape
    t = ROW_TILE
    n_mod = mod_l.shape[-1]
    row = lambda i, r: (i, r, 0)
    tr = lambda i, r: (i, 0, r)
    tab = pl.BlockSpec((t, LANES), lambda i, r: (r, 0))

    def out(width):
        return pl.BlockSpec((None, t, width), row), jax.ShapeDtypeStruct((b, nt, width), BF16)

    def out_t(width):
        return pl.BlockSpec((None, width, t), tr), jax.ShapeDtypeStruct((b, width, nt), BF16)

    outs = [out(N_BRANCH * d), out(512), out_t(512), out(512), out(512), out_t(256), out(256),
            out(512), out_t(256), out(256), out(512), out(512)]
    return pl.pallas_call(
        functools.partial(_inproj_kernel, d=d),
        grid=(b, nt // t),
        in_specs=[pl.BlockSpec((None, t, d), row),
                  pl.BlockSpec((None, 1, n_mod), lambda i, r: (jnp.where(r < n_lat_tiles, i, b), 0, 0)),
                  _full((1, d)),
                  pl.BlockSpec(w_p.shape, lambda i, r: (0, 0), pipeline_mode=pl.Buffered(1)),
                  tab, tab, tab,
                  _full((1, 512)), _full((1, 256)), _full((512, 512)), _full((512, 512)), _full((512, 512))],
        out_specs=[o[0] for o in outs],
        out_shape=[o[1] for o in outs],
        compiler_params=_params(("arbitrary", "arbitrary")),
        name="inproj",
    )(xs, mod_l, g, w_p, tabs[0], tabs[1], tabs[2], qg, kg, consts["gmat"], consts["c64"], consts["s64"])


def _softmax_parts(s):
    m = jnp.max(s, axis=-1, keepdims=True)
    e = jnp.exp(s - m)
    return e, jnp.sum(e, axis=-1, keepdims=True)


def _half_masks(shape):
    lane = lax.broadcasted_iota(jnp.int32, shape, 1)
    return lane < HEAD_DIM, lane >= HEAD_DIM


def _attn_a_kernel(q_ref, kt_ref, v_ref, lp_ref, g_ref, o_ref, *, n_lat_tiles, s_lat, lam_init):
    r = pl.program_id(1)
    lp = lp_ref[...]
    lam = (jnp.exp(jnp.sum(lp[0:1] * lp[1:2], axis=-1, keepdims=True))
           - jnp.exp(jnp.sum(lp[2:3] * lp[3:4], axis=-1, keepdims=True)) + lam_init)
    g = g_ref[...] * (1.0 - lam_init)
    nt = kt_ref.shape[1]

    def body(k0):
        lo, hi = _half_masks((q_ref.shape[0], LANES))
        for hd in range(A_HEADS):
            cs = slice(hd * LANES, (hd + 1) * LANES)
            q = q_ref[:, cs]
            kt = kt_ref[cs, k0:nt]
            e1, l1 = _softmax_parts(jnp.dot(jnp.where(lo, q, 0), kt, preferred_element_type=F32))
            e2, l2 = _softmax_parts(jnp.dot(jnp.where(hi, q, 0), kt, preferred_element_type=F32))
            a = (e1 * (1.0 / l1) - e2 * (lam / l2)).astype(BF16)
            o = jnp.dot(a, v_ref[k0:nt, cs], preferred_element_type=F32)
            o = o * lax.rsqrt(jnp.mean(o * o, axis=-1, keepdims=True) + SUBLN_EPS) * g
            o_ref[:, cs] = o.astype(BF16)

    pl.when(r < n_lat_tiles)(lambda: body(0))
    pl.when(r >= n_lat_tiles)(lambda: body(s_lat))


def _attn_b_kernel(q_ref, kt_ref, v_ref, o_ref, *, n_lat_tiles, s_lat):
    r = pl.program_id(1)
    nt = kt_ref.shape[1]

    def body(k0):
        lo, hi = _half_masks((q_ref.shape[0], LANES))
        for pair in range(B_HEADS // 2):
            kv = pair // (B_HEADS // B_KV // 2)
            ks = slice(kv * LANES, (kv + 1) * LANES)
            q = q_ref[:, pair * LANES:(pair + 1) * LANES]
            kt = kt_ref[ks, k0:nt]
            v = v_ref[k0:nt, ks]
            halves = []
            for msk in (lo, hi):
                e, l = _softmax_parts(jnp.dot(jnp.where(msk, q, 0), kt, preferred_element_type=F32))
                halves.append(jnp.dot(e.astype(BF16), v, preferred_element_type=F32) * (1.0 / l))
            o_ref[:, pair * LANES:(pair + 1) * LANES] = jnp.where(lo, halves[0], halves[1]).astype(BF16)

    pl.when(r < n_lat_tiles)(lambda: body(0))
    pl.when(r >= n_lat_tiles)(lambda: body(s_lat))


def _attn_c_kernel(sink_ref, q_ref, ktl_ref, ktm_ref, ktr_ref, ktc_ref, vl_ref, vm_ref, vr_ref, vc_ref, o_ref,
                   *, n_lat_tiles, s_lat):
    r = pl.program_id(1)
    t = q_ref.shape[0]
    n_band = t + 2 * WINDOW
    n_keys = n_band + ktc_ref.shape[1]
    u = lax.broadcasted_iota(jnp.int32, (t, n_keys), 1)
    row = lax.broadcasted_iota(jnp.int32, (t, n_keys), 0)
    j = r * t - WINDOW + u
    in_band = (u >= row) & (u <= row + 2 * WINDOW) & (j >= 0) & (j < s_lat) & (r < n_lat_tiles)
    valid = in_band | (u >= n_band)
    lo, hi = _half_masks((t, LANES))
    for pair in range(C_HEADS // 2):
        kv = pair // (C_HEADS // C_KV // 2)
        ks = slice(kv * LANES, (kv + 1) * LANES)
        q = q_ref[:, pair * LANES:(pair + 1) * LANES]
        kt = jnp.concatenate([ktl_ref[ks, :], ktm_ref[ks, :], ktr_ref[ks, :], ktc_ref[ks, :]], axis=1)
        v = jnp.concatenate([vl_ref[:, ks], vm_ref[:, ks], vr_ref[:, ks], vc_ref[:, ks]], axis=0)
        halves = []
        for half, msk in enumerate((lo, hi)):
            sink = sink_ref[2 * pair + half]
            s = jnp.dot(jnp.where(msk, q, 0), kt, preferred_element_type=F32)
            s = jnp.where(valid, s, NEG_INF)
            m = jnp.maximum(jnp.max(s, axis=-1, keepdims=True), sink)
            e = jnp.exp(s - m)
            l = jnp.sum(e, axis=-1, keepdims=True) + jnp.exp(sink - m)
            halves.append(jnp.dot(e.astype(BF16), v, preferred_element_type=F32) * (1.0 / l))
        o_ref[:, pair * LANES:(pair + 1) * LANES] = jnp.where(lo, halves[0], halves[1]).astype(BF16)


def _attention(p, lp, subln_g, sink, n_lat_tiles, s_lat, lam_init):
    (_, aq, akt, av, bq, bkt, bv, cq, ckt, cv, _, _) = p
    b, nt, _ = aq.shape
    t = ROW_TILE
    grid = (b, nt // t)
    row = lambda i, r: (i, r, 0)
    whole = lambda i, r: (i, 0, 0)
    sem = _params(("arbitrary", "arbitrary"))
    o_spec = pl.BlockSpec((None, t, 512), row)
    o_shape = jax.ShapeDtypeStruct((b, nt, 512), BF16)

    oa = pl.pallas_call(
        functools.partial(_attn_a_kernel, n_lat_tiles=n_lat_tiles, s_lat=s_lat, lam_init=lam_init),
        grid=grid,
        in_specs=[pl.BlockSpec((None, t, 512), row), pl.BlockSpec((None, 512, nt), whole),
                  pl.BlockSpec((None, nt, 512), whole), _full((4, HEAD_DIM)), _full((1, 2 * HEAD_DIM))],
        out_specs=o_spec, out_shape=o_shape, compiler_params=sem, name="attn_diff",
    )(aq, akt, av, lp, subln_g)

    ob = pl.pallas_call(
        functools.partial(_attn_b_kernel, n_lat_tiles=n_lat_tiles, s_lat=s_lat),
        grid=grid,
        in_specs=[pl.BlockSpec((None, t, 512), row), pl.BlockSpec((None, 256, nt), whole),
                  pl.BlockSpec((None, nt, 256), whole)],
        out_specs=o_spec, out_shape=o_shape, compiler_params=sem, name="attn_gqa",
    )(bq, bkt, bv)

    wpt = t // WINDOW
    last = s_lat // WINDOW - 1
    left = lambda r: jnp.clip(r * wpt - 1, 0, last)
    mid = lambda r: jnp.minimum(r, n_lat_tiles - 1)
    right = lambda r: jnp.clip((r + 1) * wpt, 0, last)
    n_ctx = nt - s_lat
    ctx_blk = s_lat // n_ctx
    oc = pl.pallas_call(
        functools.partial(_attn_c_kernel, n_lat_tiles=n_lat_tiles, s_lat=s_lat),
        grid=grid,
        in_specs=[pl.BlockSpec(memory_space=pltpu.SMEM),
                  pl.BlockSpec((None, t, 512), row),
                  pl.BlockSpec((None, 256, WINDOW), lambda i, r: (i, 0, left(r))),
                  pl.BlockSpec((None, 256, t), lambda i, r: (i, 0, mid(r))),
                  pl.BlockSpec((None, 256, WINDOW), lambda i, r: (i, 0, right(r))),
                  pl.BlockSpec((None, 256, n_ctx), lambda i, r: (i, 0, ctx_blk)),
                  pl.BlockSpec((None, WINDOW, 256), lambda i, r: (i, left(r), 0)),
                  pl.BlockSpec((None, t, 256), lambda i, r: (i, mid(r), 0)),
                  pl.BlockSpec((None, WINDOW, 256), lambda i, r: (i, right(r), 0)),
                  pl.BlockSpec((None, n_ctx, 256), lambda i, r: (i, ctx_blk, 0))],
        out_specs=o_spec, out_shape=o_shape, compiler_params=sem, name="attn_window",
    )(sink, cq, ckt, ckt, ckt, ckt, cv, cv, cv, cv)
    return oa, ob, oc


def _fourier_kernel(cn_ref, sn_ref, cc_ref, sc_ref, zc_ref, zs_ref, o_ref, *, n_lat_tiles, s_lat):
    r = pl.program_id(1)
    nt = zc_ref.shape[0]

    @pl.when(r < n_lat_tiles)
    def _():
        y = (jnp.dot(cn_ref[...], zc_ref[0:s_lat, :], preferred_element_type=F32)
             - jnp.dot(sn_ref[...], zs_ref[0:s_lat, :], preferred_element_type=F32))
        o_ref[...] = y.astype(BF16)

    @pl.when(r >= n_lat_tiles)
    def _():
        y = (jnp.dot(cc_ref[...], zc_ref[s_lat:nt, :], preferred_element_type=F32)
             - jnp.dot(sc_ref[...], zs_ref[s_lat:nt, :], preferred_element_type=F32))
        o_ref[...] = y.astype(BF16)


def _fourier(fzc, fzs, consts, n_lat_tiles, s_lat):
    b, nt, w = fzc.shape
    t = ROW_TILE
    n_ctx = nt - s_lat
    lat_rows = lambda i, r: (jnp.minimum(r, n_lat_tiles - 1), 0)
    ctx_rows = lambda i, r: (jnp.maximum(r - n_lat_tiles, 0), 0)
    whole = lambda i, r: (i, 0, 0)
    return pl.pallas_call(
        functools.partial(_fourier_kernel, n_lat_tiles=n_lat_tiles, s_lat=s_lat),
        grid=(b, nt // t),
        in_specs=[pl.BlockSpec((t, s_lat), lat_rows), pl.BlockSpec((t, s_lat), lat_rows),
                  pl.BlockSpec((t, n_ctx), ctx_rows), pl.BlockSpec((t, n_ctx), ctx_rows),
                  pl.BlockSpec((None, nt, w), whole), pl.BlockSpec((None, nt, w), whole)],
        out_specs=pl.BlockSpec((None, t, w), lambda i, r: (i, r, 0)),
        out_shape=jax.ShapeDtypeStruct((b, nt, w), BF16),
        compiler_params=_params(("arbitrary", "arbitrary")),
        name="fourier",
    )(consts["cn"], consts["sn"], consts["cc"], consts["sc"], fzc, fzs)


def _merge_kernel(x_ref, mod_ref, gt_ref, oa_ref, ob_ref, oc_ref, od_ref, wbr_ref, wo_ref, o_ref, *, d):
    merged = None
    for i, br in enumerate((oa_ref, ob_ref, oc_ref, od_ref)):
        y = jnp.dot(br[...], wbr_ref[i], preferred_element_type=F32)
        y = gt_ref[:, i * d:(i + 1) * d].astype(F32) * y
        merged = y if merged is None else merged + y
    out = jnp.dot(merged.astype(BF16), wo_ref[...], preferred_element_type=F32)
    o_ref[...] = x_ref[...] + mod_ref[:, 2 * d:3 * d] * out


def _merge(xs, mod_l, gates, branches, w_br, w_o, n_lat_tiles):
    b, nt, d = xs.shape
    t = ROW_TILE
    row = lambda i, r: (i, r, 0)
    br_spec = pl.BlockSpec((None, t, BRANCH_W), row)
    return pl.pallas_call(
        functools.partial(_merge_kernel, d=d),
        grid=(b, nt // t),
        in_specs=[pl.BlockSpec((None, t, d), row),
                  pl.BlockSpec((None, 1, mod_l.shape[-1]), lambda i, r: (jnp.where(r < n_lat_tiles, i, b), 0, 0)),
                  pl.BlockSpec((None, t, N_BRANCH * d), row),
                  br_spec, br_spec, br_spec, br_spec,
                  _full(w_br.shape), _full(w_o.shape)],
        out_specs=pl.BlockSpec((None, t, d), row),
        out_shape=jax.ShapeDtypeStruct(xs.shape, F32),
        compiler_params=_params(("arbitrary", "arbitrary")),
        name="merge",
    )(xs, mod_l, gates, *branches, w_br, w_o)


def _ffn_kernel(x_ref, mod_ref, g_ref, wg_ref, wu_ref, wd_ref, o_ref, *, d):
    x = x_ref[...]
    h = _adaln(x, g_ref[...], mod_ref[:, 3 * d:4 * d], mod_ref[:, 4 * d:5 * d]).astype(BF16)
    a = jnp.dot(h, wg_ref[...], preferred_element_type=F32)
    u = jnp.dot(h, wu_ref[...], preferred_element_type=F32)
    act = (_silu(a) * u).astype(BF16)
    y = jnp.dot(act, wd_ref[...], preferred_element_type=F32)
    o_ref[...] = x + mod_ref[:, 5 * d:6 * d] * y


def _ffn(xs, mod_l, g, wg, wu, wd, n_lat_tiles):
    b, nt, d = xs.shape
    t = ROW_TILE
    row = lambda i, r: (i, r, 0)
    const = lambda shape: pl.BlockSpec(shape, lambda i, r: (0, 0), pipeline_mode=pl.Buffered(1))
    return pl.pallas_call(
        functools.partial(_ffn_kernel, d=d),
        grid=(b, nt // t),
        in_specs=[pl.BlockSpec((None, t, d), row),
                  pl.BlockSpec((None, 1, mod_l.shape[-1]), lambda i, r: (jnp.where(r < n_lat_tiles, i, b), 0, 0)),
                  _full((1, d)), const(wg.shape), const(wu.shape), const(wd.shape)],
        out_specs=pl.BlockSpec((None, t, d), row),
        out_shape=jax.ShapeDtypeStruct(xs.shape, F32),
        compiler_params=_params(("arbitrary", "arbitrary")),
        name="ffn_dense",
    )(xs, mod_l, g, wg, wu, wd)


def _router_kernel(x_ref, mod_ref, g_ref, wr_ref, h_ref, sel_ref, *, d):
    h = _adaln(x_ref[...], g_ref[...], mod_ref[:, 3 * d:4 * d], mod_ref[:, 4 * d:5 * d])
    h_ref[...] = h
    logits = jnp.dot(h, wr_ref[...], preferred_element_type=F32, precision=lax.Precision.HIGHEST)
    lane = lax.broadcasted_iota(jnp.int32, logits.shape, 1)
    logits = jnp.where(lane < N_EXPERTS, logits, -jnp.inf)
    m1 = jnp.max(logits, axis=-1, keepdims=True)
    i1 = jnp.min(jnp.where(logits == m1, lane, LANES), axis=-1, keepdims=True)
    rest = jnp.where(lane == i1, -jnp.inf, logits)
    m2 = jnp.max(rest, axis=-1, keepdims=True)
    i2 = jnp.min(jnp.where(rest == m2, lane, LANES), axis=-1, keepdims=True)
    e2 = jnp.exp(m2 - m1)
    w1 = 1.0 / (1.0 + e2)
    sel_ref[...] = (jnp.where(lane == 0, w1, 0.0) + jnp.where(lane == 1, e2 * w1, 0.0)
                    + jnp.where(lane == 2, i1.astype(F32), 0.0) + jnp.where(lane == 3, i2.astype(F32), 0.0))


def _router(xs, mod_l, g, w_r, n_lat_tiles, n_row_tiles):
    b, _, d = xs.shape
    t = ROW_TILE
    row = lambda i, r: (i, r, 0)
    return pl.pallas_call(
        functools.partial(_router_kernel, d=d),
        grid=(b, n_row_tiles),
        in_specs=[pl.BlockSpec((None, t, d), row),
                  pl.BlockSpec((None, 1, mod_l.shape[-1]), lambda i, r: (jnp.where(r < n_lat_tiles, i, b), 0, 0)),
                  _full((1, d)), _full(w_r.shape)],
        out_specs=[pl.BlockSpec((None, t, d), row), pl.BlockSpec((None, t, LANES), row)],
        out_shape=[jax.ShapeDtypeStruct((b, n_row_tiles * t, d), F32),
                   jax.ShapeDtypeStruct((b, n_row_tiles * t, LANES), F32)],
        compiler_params=_params(("arbitrary", "arbitrary")),
        name="moe_router",
    )(xs, mod_l, g, w_r)


def _dispatch(sel, tm):
    m = sel.shape[0]
    idx = sel[:, 2:4].astype(jnp.int32)
    e_flat = jnp.concatenate([idx[:, 0], idx[:, 1]])
    order = jnp.argsort(e_flat, stable=True).astype(jnp.int32)
    rank = jnp.argsort(order).astype(jnp.int32)
    counts = jnp.sum((e_flat[:, None] == jnp.arange(N_EXPERTS)[None, :]).astype(jnp.int32), axis=0)
    padded = (counts + tm - 1) // tm * tm
    pad_end = jnp.cumsum(padded)
    pad_start = pad_end - padded
    start = jnp.cumsum(counts) - counts
    n_tiles = 2 * m // tm + N_EXPERTS
    n_used = pad_end[-1] // tm
    tile = jnp.minimum(jnp.arange(n_tiles, dtype=jnp.int32), n_used - 1)
    tile_e = jnp.sum((tile[:, None] * tm >= pad_end[None, :]).astype(jnp.int32), axis=1)
    slot = jnp.arange(n_tiles * tm, dtype=jnp.int32)
    slot_e = tile_e[slot // tm]
    pos = slot - pad_start[slot_e]
    valid = (pos < counts[slot_e]) & (slot < pad_end[-1])
    src = jnp.where(valid, order[jnp.clip(start[slot_e] + pos, 0, 2 * m - 1)] % m, 0)
    slot_of = pad_start[e_flat] + rank - start[e_flat]
    return tile_e, n_used.reshape(1).astype(jnp.int32), src.astype(jnp.int32), slot_of.astype(jnp.int32)


def _experts_kernel(te_ref, nu_ref, src_ref, h_hbm, wg_ref, wu_ref, wd_ref, o_ref, xg_ref, xb_ref, sem):
    t, j = pl.program_id(0), pl.program_id(1)
    tm = xg_ref.shape[0]
    used = t < nu_ref[0]

    @pl.when(j == 0)
    def _():
        o_ref[...] = jnp.zeros_like(o_ref)

    @pl.when(used & (j == 0))
    def _():
        def issue(i, carry):
            pltpu.make_async_copy(h_hbm.at[pl.ds(src_ref[t * tm + i], 1)], xg_ref.at[pl.ds(i, 1)], sem).start()
            return carry

        lax.fori_loop(0, tm, issue, 0, unroll=8)
        pltpu.make_async_copy(h_hbm.at[pl.ds(0, tm)], xg_ref, sem).wait()
        xb_ref[...] = xg_ref[...].astype(BF16)

    @pl.when(used)
    def _():
        h = xb_ref[...]
        a = jnp.dot(h, wg_ref[...].astype(BF16), preferred_element_type=F32)
        u = jnp.dot(h, wu_ref[...].astype(BF16), preferred_element_type=F32)
        act = (_silu(a) * u).astype(BF16)
        o_ref[...] += jnp.dot(act, wd_ref[...].astype(BF16), preferred_element_type=F32)


def _experts(h2, tables, wg, wu, wd, layer, tm):
    m, d = h2.shape
    dff = wg.shape[-1]
    tile_e, n_used, src, _ = tables
    n_tiles = tile_e.shape[0]
    tf = MOE_FF_TILE if dff % MOE_FF_TILE == 0 else dff
    nj = dff // tf
    col = lambda t, j, te, nu, src: (layer, te[t], 0, jnp.where(t < nu[0], j, nj - 1))
    rowb = lambda t, j, te, nu, src: (layer, te[t], jnp.where(t < nu[0], j, nj - 1), 0)
    return pl.pallas_call(
        _experts_kernel,
        grid_spec=pltpu.PrefetchScalarGridSpec(
            num_scalar_prefetch=3,
            grid=(n_tiles, nj),
            in_specs=[pl.BlockSpec(memory_space=pl.ANY),
                      pl.BlockSpec((None, None, d, tf), col),
                      pl.BlockSpec((None, None, d, tf), col),
                      pl.BlockSpec((None, None, tf, d), rowb)],
            out_specs=pl.BlockSpec((tm, d), lambda t, j, te, nu, src: (t, 0)),
            scratch_shapes=[pltpu.VMEM((tm, d), F32), pltpu.VMEM((tm, d), BF16), pltpu.SemaphoreType.DMA]),
        out_shape=jax.ShapeDtypeStruct((n_tiles * tm, d), F32),
        compiler_params=_params(("arbitrary", "arbitrary")),
        name="moe_experts",
    )(tile_e, n_used, src, h2, wg, wu, wd)


def _combine_kernel(slot_ref, x_ref, sel_ref, mod_ref, g_ref, y_hbm, o_ref, r1_ref, r2_ref, sems,
                    *, d, m, rows_per_sample, final):
    t = x_ref.shape[0]
    base = pl.program_id(0) * rows_per_sample + pl.program_id(1) * t

    def issue(k, carry):
        pltpu.make_async_copy(y_hbm.at[pl.ds(slot_ref[base + k], 1)], r1_ref.at[pl.ds(k, 1)], sems.at[0]).start()
        pltpu.make_async_copy(y_hbm.at[pl.ds(slot_ref[m + base + k], 1)], r2_ref.at[pl.ds(k, 1)], sems.at[1]).start()
        return carry

    lax.fori_loop(0, t, issue, 0, unroll=8)
    pltpu.make_async_copy(y_hbm.at[pl.ds(0, t)], r1_ref, sems.at[0]).wait()
    pltpu.make_async_copy(y_hbm.at[pl.ds(0, t)], r2_ref, sems.at[1]).wait()
    y = sel_ref[:, 0:1] * r1_ref[...] + sel_ref[:, 1:2] * r2_ref[...]
    x = x_ref[...] + mod_ref[:, 5 * d:6 * d] * y
    if final:
        x = x * lax.rsqrt(jnp.mean(x * x, axis=-1, keepdims=True) + EPS) * g_ref[...]
    o_ref[...] = x


def _combine(xs, sel, mod_l, y_sorted, slot_of, final_g, n_lat_tiles, n_row_tiles, final):
    b, _, d = xs.shape
    t = ROW_TILE
    rows = n_row_tiles * t
    row = lambda i, r, s: (i, r, 0)
    return pl.pallas_call(
        functools.partial(_combine_kernel, d=d, m=b * rows, rows_per_sample=rows, final=final),
        grid_spec=pltpu.PrefetchScalarGridSpec(
            num_scalar_prefetch=1,
            grid=(b, n_row_tiles),
            in_specs=[pl.BlockSpec((None, t, d), row),
                      pl.BlockSpec((None, t, LANES), row),
                      pl.BlockSpec((None, 1, mod_l.shape[-1]),
                                   lambda i, r, s: (jnp.where(r < n_lat_tiles, i, b), 0, 0)),
                      pl.BlockSpec((1, d), lambda i, r, s: (0, 0)),
                      pl.BlockSpec(memory_space=pl.ANY)],
            out_specs=pl.BlockSpec((None, t, d), row),
            scratch_shapes=[pltpu.VMEM((t, d), F32), pltpu.VMEM((t, d), F32), pltpu.SemaphoreType.DMA((2,))]),
        out_shape=jax.ShapeDtypeStruct((b, rows, d), F32),
        compiler_params=_params(("arbitrary", "arbitrary")),
        name="moe_combine",
    )(slot_of, xs, sel, mod_l, final_g, y_sorted)


def _rope_tables(s_lat, n_ctx):
    rows = s_lat // GRID_W
    pos_row = jnp.repeat(jnp.arange(rows, dtype=jnp.int32), GRID_W).astype(F32)
    pos_col = jnp.tile(jnp.arange(GRID_W, dtype=jnp.int32), rows).astype(F32)
    quarter = HEAD_DIM // 4
    inv_freq = ROPE_THETA ** (-jnp.arange(quarter, dtype=F32) / quarter)
    ang_r = pos_row[:, None] * inv_freq
    ang_c = pos_col[:, None] * inv_freq
    ang = jnp.concatenate([ang_r, ang_r, ang_c, ang_c], axis=-1)
    cos, sin = jnp.cos(ang), jnp.sin(ang)
    first = (np.arange(HEAD_DIM) % (2 * quarter)) < quarter
    sin_lo = jnp.where(first, -sin, 0.0)
    sin_hi = jnp.where(first, 0.0, sin)

    def widen(tab, ctx_fill):
        tab = jnp.concatenate([tab, jnp.full((n_ctx, HEAD_DIM), ctx_fill, F32)], axis=0)
        return jnp.tile(tab, (1, LANES // HEAD_DIM))

    return widen(cos, 1.0), widen(sin_lo, 0.0), widen(sin_hi, 0.0)


def _dft_tables(n):
    k = np.arange(n, dtype=np.int64)
    ang = 2.0 * np.pi * ((k[:, None] * k[None, :]) % n).astype(np.float64) / n
    return np.cos(ang), np.sin(ang)


def _constants(s_lat, n_ctx):
    c64, s64 = _dft_tables(F_GROUP_DIM)
    eye = np.eye(F_GROUPS)
    cn, sn = _dft_tables(s_lat)
    cc, sc = _dft_tables(n_ctx)
    bf = lambda a: jnp.asarray(a, dtype=F32).astype(BF16)
    return dict(
        gmat=bf(np.kron(np.eye(512 // HEAD_DIM), np.ones((HEAD_DIM, HEAD_DIM)))),
        c64=bf(np.kron(eye, c64) / math.sqrt(F_GROUP_DIM)),
        s64=bf(np.kron(eye, s64) / math.sqrt(F_GROUP_DIM)),
        cn=bf(cn / math.sqrt(s_lat)), sn=bf(sn / math.sqrt(s_lat)),
        cc=bf(cc / math.sqrt(n_ctx)), sc=bf(sc / math.sqrt(n_ctx)),
    )


def _dup_heads(w, off, n_heads):
    cols = []
    for hd in range(n_heads):
        blk = w[:, off + hd * HEAD_DIM: off + (hd + 1) * HEAD_DIM]
        cols += [blk, blk]
    return cols


def _relayout_w_in(w):
    cols = [w[:, _OFF["gt"]:_OFF["end"]], w[:, _OFF["aq"]:_OFF["bq"]], w[:, _OFF["bq"]:_OFF["bk"]],
            w[:, _OFF["cq"]:_OFF["ck"]], w[:, _OFF["fz"]:_OFF["gt"]]]
    cols += _dup_heads(w, _OFF["bk"], B_KV) + _dup_heads(w, _OFF["bv"], B_KV)
    cols += _dup_heads(w, _OFF["ck"], C_KV) + _dup_heads(w, _OFF["cv"], C_KV)
    return jnp.concatenate(cols, axis=1).astype(BF16)


def kernel(x, c, ctx, c_ctx, w_mod, b_mod, norm1_g, norm2_g, w_in, diff_lambda, diff_subln_g, qk_norm_g,
           sink_logits, w_branch, w_out, ffn_w_gate, ffn_w_up, ffn_w_down, router_w, moe_w_gate, moe_w_up,
           moe_w_down, final_g):
    b, s_lat, d = x.shape
    n_ctx = ctx.shape[1]
    depth = w_mod.shape[0]
    nt = s_lat + n_ctx
    t = ROW_TILE
    assert s_lat % t == 0 and n_ctx % t == 0 and s_lat % n_ctx == 0 and b < MOD_ROWS
    assert w_in.shape[-1] == _OFF["end"] and d * N_BRANCH == _OFF["end"] - _OFF["gt"]
    n_lat_tiles = s_lat // t

    xs = jnp.concatenate([x, ctx], axis=1)
    cvec = jnp.concatenate([c, c_ctx[None], jnp.zeros((MOD_ROWS - b - 1, d), F32)], axis=0)
    mod = _modulation(cvec, w_mod, b_mod).reshape(depth, MOD_ROWS, 1, 6 * d)
    tabs = _rope_tables(s_lat, n_ctx)
    consts = _constants(s_lat, n_ctx)

    out = None
    for l in range(depth):
        last = l == depth - 1
        mod_l = mod[l]
        p = _inproj(xs, mod_l, norm1_g[l][None], _relayout_w_in(w_in[l]), tabs,
                    jnp.tile(qk_norm_g[l, 0], 512 // HEAD_DIM)[None], jnp.tile(qk_norm_g[l, 1], 256 // HEAD_DIM)[None],
                    consts, n_lat_tiles)
        lam_init = 0.8 - 0.6 * math.exp(-0.3 * l)
        oa, ob, oc = _attention(p, diff_lambda[l], diff_subln_g[l][None], sink_logits[l],
                                n_lat_tiles, s_lat, lam_init)
        od = _fourier(p[10], p[11], consts, n_lat_tiles, s_lat)
        xs = _merge(xs, mod_l, p[0], (oa, ob, oc, od), w_branch[l].astype(BF16), w_out[l].astype(BF16), n_lat_tiles)

        i = l // 2
        if l % 2 == 0:
            xs = _ffn(xs, mod_l, norm2_g[l][None], ffn_w_gate[i].astype(BF16), ffn_w_up[i].astype(BF16),
                      ffn_w_down[i].astype(BF16), n_lat_tiles)
        else:
            w_r = jnp.pad(router_w[i], ((0, 0), (0, LANES - N_EXPERTS)))
            n_row_tiles = n_lat_tiles if last else nt // t
            m = b * n_row_tiles * t
            h2, sel = _router(xs, mod_l, norm2_g[l][None], w_r, n_lat_tiles, n_row_tiles)
            tm = MOE_TILE if 2 * m >= N_EXPERTS * MOE_TILE else ROW_TILE
            tables = _dispatch(sel.reshape(m, LANES), tm)
            y_sorted = _experts(h2.reshape(m, d), tables, moe_w_gate, moe_w_up, moe_w_down, i, tm)
            res = _combine(xs, sel, mod_l, y_sorted, tables[3], final_g[None], n_lat_tiles, n_row_tiles, last)
            if last:
                out = res
            else:
                xs = res
    if out is None:
        raise NotImplementedError("final normalisation is fused into the last (mixture-of-experts) layer")
    return out
```

```python
import functools
import math

import numpy as np
import jax
import jax.numpy as jnp
from jax import lax
from jax.experimental import pallas as pl
from jax.experimental.pallas import tpu as pltpu

F32 = jnp.float32
BF16 = jnp.bfloat16

GRID_W = 64
HEAD_DIM = 64
ROPE_THETA = 10000.0
WINDOW = 128
A_HEADS = 4
B_HEADS = 8
B_KV = 2
C_HEADS = 8
C_KV = 2
F_GROUPS = 8
F_GROUP_DIM = 64
BRANCH_W = 512
N_BRANCH = 4
N_EXPERTS = 8
EPS = 1e-6
SUBLN_EPS = 1e-5
NEG_INF = -1e30

LANES = 128
ROW_TILE = 256
MOE_TILE = 896
MOE_FF_TILE = 512
MOD_ROWS = 16
VMEM_LIMIT = 56 * 1024 * 1024

_OFF = dict(aq=0, ak=512, av=1024, bq=1536, bk=2048, bv=2176, cq=2304, ck=2816, cv=2944,
            fz=3072, gt=3584, end=7680)
_P = dict(gt=0, aq=4096, ak=4608, av=5120, bq=5632, cq=6144, fz=6656,
          bk=7168, bv=7424, ck=7680, cv=7936, end=8192)


def _params(sem):
    return pltpu.CompilerParams(dimension_semantics=sem, vmem_limit_bytes=VMEM_LIMIT)


def _full(shape):
    n = len(shape)
    return pl.BlockSpec(shape, lambda *_: (0,) * n)


def _silu(x):
    return x / (1.0 + jnp.exp(-x))


def _sigmoid(x):
    return 1.0 / (1.0 + jnp.exp(-x))


def _adaln(x, g, shift, scale):
    y = x * lax.rsqrt(jnp.mean(x * x, axis=-1, keepdims=True) + EPS) * g
    return y * (1.0 + scale) + shift


def _mod_kernel(c_ref, w_ref, b_ref, o_ref):
    s = _silu(c_ref[...]).astype(BF16)
    o_ref[...] = jnp.dot(s, w_ref[...].astype(BF16), preferred_element_type=F32) + b_ref[...]


def _modulation(cvec, w_mod, b_mod):
    depth, d, n = w_mod.shape
    tn = 1536 if n % 1536 == 0 else n
    return pl.pallas_call(
        _mod_kernel,
        grid=(depth, n // tn),
        in_specs=[_full((MOD_ROWS, d)),
                  pl.BlockSpec((None, d, tn), lambda l, j: (l, 0, j)),
                  pl.BlockSpec((None, 1, tn), lambda l, j: (l, 0, j))],
        out_specs=pl.BlockSpec((None, MOD_ROWS, tn), lambda l, j: (l, 0, j)),
        out_shape=jax.ShapeDtypeStruct((depth, MOD_ROWS, n), F32),
        compiler_params=_params(("arbitrary", "arbitrary")),
        name="modulation",
    )(cvec, w_mod, b_mod.reshape(depth, 1, n))


def _rope(p, cos, sin_lo, sin_hi):
    outs = []
    for c in range(p.shape[1] // LANES):
        xc = p[:, c * LANES:(c + 1) * LANES]
        outs.append(xc * cos + pltpu.roll(xc, LANES - 16, 1) * sin_lo + pltpu.roll(xc, 16, 1) * sin_hi)
    return outs[0] if len(outs) == 1 else jnp.concatenate(outs, axis=1)


def _group_rmsnorm(y, gmat, g):
    ss = jnp.dot((y * y).astype(BF16), gmat, preferred_element_type=F32)
    return y * lax.rsqrt(ss * (1.0 / HEAD_DIM) + EPS) * g


def _inproj_kernel(x_ref, mod_ref, g_ref, w_ref, cos_ref, slo_ref, shi_ref, qg_ref, kg_ref, gmat_ref,
                   c64_ref, s64_ref,
                   gt_ref, aq_ref, akt_ref, av_ref, bq_ref, bkt_ref, bv_ref, cq_ref, ckt_ref, cv_ref,
                   fzc_ref, fzs_ref, *, d):
    h = _adaln(x_ref[...], g_ref[...], mod_ref[:, 0:d], mod_ref[:, d:2 * d]).astype(BF16)
    cos, slo, shi = cos_ref[...], slo_ref[...], shi_ref[...]
    scale = HEAD_DIM ** -0.5

    def proj(name, width):
        return jnp.dot(h, w_ref[:, _P[name]:_P[name] + width], preferred_element_type=F32)

    for c in range(N_BRANCH * d // 512):
        p = jnp.dot(h, w_ref[:, c * 512:(c + 1) * 512], preferred_element_type=F32)
        gt_ref[:, c * 512:(c + 1) * 512] = _sigmoid(p).astype(BF16)

    aq_ref[...] = (_rope(proj("aq", 512), cos, slo, shi) * scale).astype(BF16)
    akt_ref[...] = _rope(proj("ak", 512), cos, slo, shi).T.astype(BF16)
    av_ref[...] = proj("av", 512).astype(BF16)

    bq = _group_rmsnorm(proj("bq", 512), gmat_ref[...], qg_ref[...])
    bq_ref[...] = (_rope(bq, cos, slo, shi) * scale).astype(BF16)
    bk = _group_rmsnorm(proj("bk", 256), gmat_ref[0:256, 0:256], kg_ref[...])
    bkt_ref[...] = _rope(bk, cos, slo, shi).T.astype(BF16)
    bv_ref[...] = proj("bv", 256).astype(BF16)

    cq_ref[...] = (_rope(proj("cq", 512), cos, slo, shi) * scale).astype(BF16)
    ckt_ref[...] = _rope(proj("ck", 256), cos, slo, shi).T.astype(BF16)
    cv_ref[...] = proj("cv", 256).astype(BF16)

    fz = proj("fz", 512).astype(BF16)
    fzc_ref[...] = jnp.dot(fz, c64_ref[...], preferred_element_type=F32).astype(BF16)
    fzs_ref[...] = jnp.dot(fz, s64_ref[...], preferred_element_type=F32).astype(BF16)


def _inproj(xs, mod_l, g, w_p, tabs, qg, kg, consts, n_lat_tiles):
    b, nt, d = xs.shape
    t = ROW_TILE
    n_mod = mod_l.shape[-1]
    row = lambda i, r: (i, r, 0)
    tr = lambda i, r: (i, 0, r)
    tab = pl.BlockSpec((t, LANES), lambda i, r: (r, 0))

    def out(width):
        return pl.BlockSpec((None, t, width), row), jax.ShapeDtypeStruct((b, nt, width), BF16)

    def out_t(width):
        return pl.BlockSpec((None, width, t), tr), jax.ShapeDtypeStruct((b, width, nt), BF16)

    outs = [out(N_BRANCH * d), out(512), out_t(512), out(512), out(512), out_t(256), out(256),
            out(512), out_t(256), out(256), out(512), out(512)]
    return pl.pallas_call(
        functools.partial(_inproj_kernel, d=d),
        grid=(b, nt // t),
        in_specs=[pl.BlockSpec((None, t, d), row),
                  pl.BlockSpec((None, 1, n_mod), lambda i, r: (jnp.where(r < n_lat_tiles, i, b), 0, 0)),
                  _full((1, d)),
                  pl.BlockSpec(w_p.shape, lambda i, r: (0, 0), pipeline_mode=pl.Buffered(1)),
                  tab, tab, tab,
                  _full((1, 512)), _full((1, 256)), _full((512, 512)), _full((512, 512)), _full((512, 512))],
        out_specs=[o[0] for o in outs],
        out_shape=[o[1] for o in outs],
        compiler_params=_params(("arbitrary", "arbitrary")),
        name="inproj",
    )(xs, mod_l, g, w_p, tabs[0], tabs[1], tabs[2], qg, kg, consts["gmat"], consts["c64"], consts["s64"])


def _softmax_parts(s):
    m = jnp.max(s, axis=-1, keepdims=True)
    e = jnp.exp(s - m)
    return e, jnp.sum(e, axis=-1, keepdims=True)


def _half_masks(shape):
    lane = lax.broadcasted_iota(jnp.int32, shape, 1)
    return lane < HEAD_DIM, lane >= HEAD_DIM


def _attn_a_kernel(q_ref, kt_ref, v_ref, lp_ref, g_ref, o_ref, *, n_lat_tiles, s_lat, lam_init):
    r = pl.program_id(1)
    lp = lp_ref[...]
    lam = (jnp.exp(jnp.sum(lp[0:1] * lp[1:2], axis=-1, keepdims=True))
           - jnp.exp(jnp.sum(lp[2:3] * lp[3:4], axis=-1, keepdims=True)) + lam_init)
    g = g_ref[...] * (1.0 - lam_init)
    nt = kt_ref.shape[1]

    def body(k0):
        lo, hi = _half_masks((q_ref.shape[0], LANES))
        for hd in range(A_HEADS):
            cs = slice(hd * LANES, (hd + 1) * LANES)
            q = q_ref[:, cs]
            kt = kt_ref[cs, k0:nt]
            e1, l1 = _softmax_parts(jnp.dot(jnp.where(lo, q, 0), kt, preferred_element_type=F32))
            e2, l2 = _softmax_parts(jnp.dot(jnp.where(hi, q, 0), kt, preferred_element_type=F32))
            a = (e1 * (1.0 / l1) - e2 * (lam / l2)).astype(BF16)
            o = jnp.dot(a, v_ref[k0:nt, cs], preferred_element_type=F32)
            o = o * lax.rsqrt(jnp.mean(o * o, axis=-1, keepdims=True) + SUBLN_EPS) * g
            o_ref[:, cs] = o.astype(BF16)

    pl.when(r < n_lat_tiles)(lambda: body(0))
    pl.when(r >= n_lat_tiles)(lambda: body(s_lat))


def _attn_b_kernel(q_ref, kt_ref, v_ref, o_ref, *, n_lat_tiles, s_lat):
    r = pl.program_id(1)
    nt = kt_ref.shape[1]

    def body(k0):
        lo, hi = _half_masks((q_ref.shape[0], LANES))
        for pair in range(B_HEADS // 2):
            kv = pair // (B_HEADS // B_KV // 2)
            ks = slice(kv * LANES, (kv + 1) * LANES)
            q = q_ref[:, pair * LANES:(pair + 1) * LANES]
            kt = kt_ref[ks, k0:nt]
            v = v_ref[k0:nt, ks]
            halves = []
            for msk in (lo, hi):
                e, l = _softmax_parts(jnp.dot(jnp.where(msk, q, 0), kt, preferred_element_type=F32))
                halves.append(jnp.dot(e.astype(BF16), v, preferred_element_type=F32) * (1.0 / l))
            o_ref[:, pair * LANES:(pair + 1) * LANES] = jnp.where(lo, halves[0], halves[1]).astype(BF16)

    pl.when(r < n_lat_tiles)(lambda: body(0))
    pl.when(r >= n_lat_tiles)(lambda: body(s_lat))


def _attn_c_kernel(sink_ref, q_ref, ktl_ref, ktm_ref, ktr_ref, ktc_ref, vl_ref, vm_ref, vr_ref, vc_ref, o_ref,
                   *, n_lat_tiles, s_lat):
    r = pl.program_id(1)
    t = q_ref.shape[0]
    n_band = t + 2 * WINDOW
    n_keys = n_band + ktc_ref.shape[1]
    u = lax.broadcasted_iota(jnp.int32, (t, n_keys), 1)
    row = lax.broadcasted_iota(jnp.int32, (t, n_keys), 0)
    j = r * t - WINDOW + u
    in_band = (u >= row) & (u <= row + 2 * WINDOW) & (j >= 0) & (j < s_lat) & (r < n_lat_tiles)
    valid = in_band | (u >= n_band)
    lo, hi = _half_masks((t, LANES))
    for pair in range(C_HEADS // 2):
        kv = pair // (C_HEADS // C_KV // 2)
        ks = slice(kv * LANES, (kv + 1) * LANES)
        q = q_ref[:, pair * LANES:(pair + 1) * LANES]
        kt = jnp.concatenate([ktl_ref[ks, :], ktm_ref[ks, :], ktr_ref[ks, :], ktc_ref[ks, :]], axis=1)
        v = jnp.concatenate([vl_ref[:, ks], vm_ref[:, ks], vr_ref[:, ks], vc_ref[:, ks]], axis=0)
        halves = []
        for half, msk in enumerate((lo, hi)):
            sink = sink_ref[2 * pair + half]
            s = jnp.dot(jnp.where(msk, q, 0), kt, preferred_element_type=F32)
            s = jnp.where(valid, s, NEG_INF)
            m = jnp.maximum(jnp.max(s, axis=-1, keepdims=True), sink)
            e = jnp.exp(s - m)
            l = jnp.sum(e, axis=-1, keepdims=True) + jnp.exp(sink - m)
            halves.append(jnp.dot(e.astype(BF16), v, preferred_element_type=F32) * (1.0 / l))
        o_ref[:, pair * LANES:(pair + 1) * LANES] = jnp.where(lo, halves[0], halves[1]).astype(BF16)


def _attention(p, lp, subln_g, sink, n_lat_tiles, n_q_tiles, s_lat, lam_init):
    (_, aq, akt, av, bq, bkt, bv, cq, ckt, cv, _, _) = p
    b, nt, _ = aq.shape
    t = ROW_TILE
    grid = (b, n_q_tiles)
    row = lambda i, r: (i, r, 0)
    whole = lambda i, r: (i, 0, 0)
    sem = _params(("arbitrary", "arbitrary"))
    o_spec = pl.BlockSpec((None, t, 512), row)
    o_shape = jax.ShapeDtypeStruct((b, n_q_tiles * t, 512), BF16)

    oa = pl.pallas_call(
        functools.partial(_attn_a_kernel, n_lat_tiles=n_lat_tiles, s_lat=s_lat, lam_init=lam_init),
        grid=grid,
        in_specs=[pl.BlockSpec((None, t, 512), row), pl.BlockSpec((None, 512, nt), whole),
                  pl.BlockSpec((None, nt, 512), whole), _full((4, HEAD_DIM)), _full((1, 2 * HEAD_DIM))],
        out_specs=o_spec, out_shape=o_shape, compiler_params=sem, name="attn_diff",
    )(aq, akt, av, lp, subln_g)

    ob = pl.pallas_call(
        functools.partial(_attn_b_kernel, n_lat_tiles=n_lat_tiles, s_lat=s_lat),
        grid=grid,
        in_specs=[pl.BlockSpec((None, t, 512), row), pl.BlockSpec((None, 256, nt), whole),
                  pl.BlockSpec((None, nt, 256), whole)],
        out_specs=o_spec, out_shape=o_shape, compiler_params=sem, name="attn_gqa",
    )(bq, bkt, bv)

    wpt = t // WINDOW
    last = s_lat // WINDOW - 1
    left = lambda r: jnp.clip(r * wpt - 1, 0, last)
    mid = lambda r: jnp.minimum(r, n_lat_tiles - 1)
    right = lambda r: jnp.clip((r + 1) * wpt, 0, last)
    n_ctx = nt - s_lat
    ctx_blk = s_lat // n_ctx
    oc = pl.pallas_call(
        functools.partial(_attn_c_kernel, n_lat_tiles=n_lat_tiles, s_lat=s_lat),
        grid=grid,
        in_specs=[pl.BlockSpec(memory_space=pltpu.SMEM),
                  pl.BlockSpec((None, t, 512), row),
                  pl.BlockSpec((None, 256, WINDOW), lambda i, r: (i, 0, left(r))),
                  pl.BlockSpec((None, 256, t), lambda i, r: (i, 0, mid(r))),
                  pl.BlockSpec((None, 256, WINDOW), lambda i, r: (i, 0, right(r))),
                  pl.BlockSpec((None, 256, n_ctx), lambda i, r: (i, 0, ctx_blk)),
                  pl.BlockSpec((None, WINDOW, 256), lambda i, r: (i, left(r), 0)),
                  pl.BlockSpec((None, t, 256), lambda i, r: (i, mid(r), 0)),
                  pl.BlockSpec((None, WINDOW, 256), lambda i, r: (i, right(r), 0)),
                  pl.BlockSpec((None, n_ctx, 256), lambda i, r: (i, ctx_blk, 0))],
        out_specs=o_spec, out_shape=o_shape, compiler_params=sem, name="attn_window",
    )(sink, cq, ckt, ckt, ckt, ckt, cv, cv, cv, cv)
    return oa, ob, oc


def _fourier_kernel(cn_ref, sn_ref, cc_ref, sc_ref, zc_ref, zs_ref, o_ref, *, n_lat_tiles, s_lat):
    r = pl.program_id(1)
    nt = zc_ref.shape[0]

    @pl.when(r < n_lat_tiles)
    def _():
        y = (jnp.dot(cn_ref[...], zc_ref[0:s_lat, :], preferred_element_type=F32)
             - jnp.dot(sn_ref[...], zs_ref[0:s_lat, :], preferred_element_type=F32))
        o_ref[...] = y.astype(BF16)

    @pl.when(r >= n_lat_tiles)
    def _():
        y = (jnp.dot(cc_ref[...], zc_ref[s_lat:nt, :], preferred_element_type=F32)
             - jnp.dot(sc_ref[...], zs_ref[s_lat:nt, :], preferred_element_type=F32))
        o_ref[...] = y.astype(BF16)


def _fourier(fzc, fzs, consts, n_lat_tiles, n_q_tiles, s_lat):
    b, nt, w = fzc.shape
    t = ROW_TILE
    n_ctx = nt - s_lat
    lat_rows = lambda i, r: (jnp.minimum(r, n_lat_tiles - 1), 0)
    ctx_rows = lambda i, r: (jnp.maximum(r - n_lat_tiles, 0), 0)
    whole = lambda i, r: (i, 0, 0)
    return pl.pallas_call(
        functools.partial(_fourier_kernel, n_lat_tiles=n_lat_tiles, s_lat=s_lat),
        grid=(b, n_q_tiles),
        in_specs=[pl.BlockSpec((t, s_lat), lat_rows), pl.BlockSpec((t, s_lat), lat_rows),
                  pl.BlockSpec((t, n_ctx), ctx_rows), pl.BlockSpec((t, n_ctx), ctx_rows),
                  pl.BlockSpec((None, nt, w), whole), pl.BlockSpec((None, nt, w), whole)],
        out_specs=pl.BlockSpec((None, t, w), lambda i, r: (i, r, 0)),
        out_shape=jax.ShapeDtypeStruct((b, n_q_tiles * t, w), BF16),
        compiler_params=_params(("arbitrary", "arbitrary")),
        name="fourier",
    )(consts["cn"], consts["sn"], consts["cc"], consts["sc"], fzc, fzs)


def _merge_kernel(x_ref, mod_ref, gt_ref, oa_ref, ob_ref, oc_ref, od_ref, wbr_ref, wo_ref, o_ref, *, d):
    merged = None
    for i, br in enumerate((oa_ref, ob_ref, oc_ref, od_ref)):
        y = jnp.dot(br[...], wbr_ref[i], preferred_element_type=F32)
        y = gt_ref[:, i * d:(i + 1) * d].astype(F32) * y
        merged = y if merged is None else merged + y
    out = jnp.dot(merged.astype(BF16), wo_ref[...], preferred_element_type=F32)
    o_ref[...] = x_ref[...] + mod_ref[:, 2 * d:3 * d] * out


def _merge(xs, mod_l, gates, branches, w_br, w_o, n_lat_tiles, n_q_tiles):
    b, _, d = xs.shape
    t = ROW_TILE
    row = lambda i, r: (i, r, 0)
    br_spec = pl.BlockSpec((None, t, BRANCH_W), row)
    return pl.pallas_call(
        functools.partial(_merge_kernel, d=d),
        grid=(b, n_q_tiles),
        in_specs=[pl.BlockSpec((None, t, d), row),
                  pl.BlockSpec((None, 1, mod_l.shape[-1]), lambda i, r: (jnp.where(r < n_lat_tiles, i, b), 0, 0)),
                  pl.BlockSpec((None, t, N_BRANCH * d), row),
                  br_spec, br_spec, br_spec, br_spec,
                  _full(w_br.shape), _full(w_o.shape)],
        out_specs=pl.BlockSpec((None, t, d), row),
        out_shape=jax.ShapeDtypeStruct((b, n_q_tiles * t, d), F32),
        compiler_params=_params(("arbitrary", "arbitrary")),
        name="merge",
    )(xs, mod_l, gates, *branches, w_br, w_o)


def _ffn_kernel(x_ref, mod_ref, g_ref, wg_ref, wu_ref, wd_ref, o_ref, *, d):
    x = x_ref[...]
    h = _adaln(x, g_ref[...], mod_ref[:, 3 * d:4 * d], mod_ref[:, 4 * d:5 * d]).astype(BF16)
    a = jnp.dot(h, wg_ref[...], preferred_element_type=F32)
    u = jnp.dot(h, wu_ref[...], preferred_element_type=F32)
    act = (_silu(a) * u).astype(BF16)
    y = jnp.dot(act, wd_ref[...], preferred_element_type=F32)
    o_ref[...] = x + mod_ref[:, 5 * d:6 * d] * y


def _ffn(xs, mod_l, g, wg, wu, wd, n_lat_tiles):
    b, nt, d = xs.shape
    t = ROW_TILE
    row = lambda i, r: (i, r, 0)
    const = lambda shape: pl.BlockSpec(shape, lambda i, r: (0, 0), pipeline_mode=pl.Buffered(1))
    return pl.pallas_call(
        functools.partial(_ffn_kernel, d=d),
        grid=(b, nt // t),
        in_specs=[pl.BlockSpec((None, t, d), row),
                  pl.BlockSpec((None, 1, mod_l.shape[-1]), lambda i, r: (jnp.where(r < n_lat_tiles, i, b), 0, 0)),
                  _full((1, d)), const(wg.shape), const(wu.shape), const(wd.shape)],
        out_specs=pl.BlockSpec((None, t, d), row),
        out_shape=jax.ShapeDtypeStruct(xs.shape, F32),
        compiler_params=_params(("arbitrary", "arbitrary")),
        name="ffn_dense",
    )(xs, mod_l, g, wg, wu, wd)


def _router_kernel(x_ref, mod_ref, g_ref, wr_ref, h_ref, sel_ref, *, d):
    h = _adaln(x_ref[...], g_ref[...], mod_ref[:, 3 * d:4 * d], mod_ref[:, 4 * d:5 * d])
    h_ref[...] = h
    logits = jnp.dot(h, wr_ref[...], preferred_element_type=F32, precision=lax.Precision.HIGHEST)
    lane = lax.broadcasted_iota(jnp.int32, logits.shape, 1)
    logits = jnp.where(lane < N_EXPERTS, logits, -jnp.inf)
    m1 = jnp.max(logits, axis=-1, keepdims=True)
    i1 = jnp.min(jnp.where(logits == m1, lane, LANES), axis=-1, keepdims=True)
    rest = jnp.where(lane == i1, -jnp.inf, logits)
    m2 = jnp.max(rest, axis=-1, keepdims=True)
    i2 = jnp.min(jnp.where(rest == m2, lane, LANES), axis=-1, keepdims=True)
    e2 = jnp.exp(m2 - m1)
    w1 = 1.0 / (1.0 + e2)
    sel_ref[...] = (jnp.where(lane == 0, w1, 0.0) + jnp.where(lane == 1, e2 * w1, 0.0)
                    + jnp.where(lane == 2, i1.astype(F32), 0.0) + jnp.where(lane == 3, i2.astype(F32), 0.0))


def _router(xs, mod_l, g, w_r, n_lat_tiles, n_row_tiles):
    b, _, d = xs.shape
    t = ROW_TILE
    row = lambda i, r: (i, r, 0)
    return pl.pallas_call(
        functools.partial(_router_kernel, d=d),
        grid=(b, n_row_tiles),
        in_specs=[pl.BlockSpec((None, t, d), row),
                  pl.BlockSpec((None, 1, mod_l.shape[-1]), lambda i, r: (jnp.where(r < n_lat_tiles, i, b), 0, 0)),
                  _full((1, d)), _full(w_r.shape)],
        out_specs=[pl.BlockSpec((None, t, d), row), pl.BlockSpec((None, t, LANES), row)],
        out_shape=[jax.ShapeDtypeStruct((b, n_row_tiles * t, d), F32),
                   jax.ShapeDtypeStruct((b, n_row_tiles * t, LANES), F32)],
        compiler_params=_params(("arbitrary", "arbitrary")),
        name="moe_router",
    )(xs, mod_l, g, w_r)


def _dispatch(sel, tm):
    m = sel.shape[0]
    i32 = jnp.int32
    experts = jnp.arange(N_EXPERTS, dtype=i32)

    def lookup(table, idx):
        return jnp.sum(jnp.where(idx[:, None] == experts[None, :], table[None, :], 0), axis=1)

    idx = sel[:, 2:4].astype(i32)
    e_flat = jnp.concatenate([idx[:, 0], idx[:, 1]])
    assignment = jnp.arange(2 * m, dtype=i32)
    _, order = lax.sort_key_val(e_flat, assignment)
    _, rank = lax.sort_key_val(order, assignment)
    counts = jnp.sum((e_flat[:, None] == experts[None, :]).astype(i32), axis=0)
    padded = (counts + tm - 1) // tm * tm
    pad_end = jnp.cumsum(padded)
    pad_start = pad_end - padded
    start = jnp.cumsum(counts) - counts
    n_tiles = 2 * m // tm + N_EXPERTS
    n_used = pad_end[-1] // tm
    tile = jnp.minimum(jnp.arange(n_tiles, dtype=i32), n_used - 1)
    tile_e = jnp.sum((tile[:, None] * tm >= pad_end[None, :]).astype(i32), axis=1)
    slot = jnp.arange(n_tiles * tm, dtype=i32)
    slot_e = jnp.sum((jnp.minimum(slot, pad_end[-1] - 1)[:, None] >= pad_end[None, :]).astype(i32), axis=1)
    pos = slot - lookup(pad_start, slot_e)
    valid = (pos < lookup(counts, slot_e)) & (slot < pad_end[-1])
    src = jnp.where(valid, order[jnp.clip(lookup(start, slot_e) + pos, 0, 2 * m - 1)] % m, 0)
    slot_of = lookup(pad_start - start, e_flat) + rank
    return tile_e, n_used.reshape(1).astype(i32), src.astype(i32), slot_of.astype(i32)


def _experts_kernel(te_ref, nu_ref, src_ref, h_hbm, wg_ref, wu_ref, wd_ref, o_ref, xg_ref, xb_ref, sems,
                    *, n_tiles, n_steps):
    t, j = pl.program_id(0), pl.program_id(1)
    tm = xb_ref.shape[0]
    rows_per_step = tm // n_steps
    n_used = nu_ref[0]
    used = t < n_used
    buf = t % 2

    def start_row(tile, dst_buf, r):
        pltpu.make_async_copy(h_hbm.at[pl.ds(src_ref[tile * tm + r], 1)], xg_ref.at[dst_buf, pl.ds(r, 1)],
                              sems.at[dst_buf]).start()

    def wait_rows(b_):
        pltpu.make_async_copy(h_hbm.at[pl.ds(0, tm)], xg_ref.at[b_], sems.at[b_]).wait()

    @pl.when((t == 0) & (j == 0))
    def _():
        def issue(r, carry):
            start_row(0, 0, r)
            return carry

        lax.fori_loop(0, tm, issue, 0, unroll=8)

    @pl.when((j == 0) & ((t == 0) | (t - 1 < n_used)))
    def _():
        wait_rows(buf)
        xb_ref[...] = xg_ref[buf].astype(BF16)

    @pl.when(j == 0)
    def _():
        o_ref[...] = jnp.zeros_like(o_ref)

    @pl.when(used)
    def _():
        nxt = jnp.minimum(t + 1, n_tiles - 1)
        for r in range(rows_per_step):
            start_row(nxt, 1 - buf, j * rows_per_step + r)
        h = xb_ref[...]
        a = jnp.dot(h, wg_ref[...].astype(BF16), preferred_element_type=F32)
        u = jnp.dot(h, wu_ref[...].astype(BF16), preferred_element_type=F32)
        act = (_silu(a) * u).astype(BF16)
        o_ref[...] += jnp.dot(act, wd_ref[...].astype(BF16), preferred_element_type=F32)

    @pl.when(used & (t == n_tiles - 1) & (j == n_steps - 1))
    def _():
        wait_rows(1 - buf)


def _experts(h2, tables, wg, wu, wd, layer, tm):
    m, d = h2.shape
    dff = wg.shape[-1]
    tile_e, n_used, src, _ = tables
    n_tiles = tile_e.shape[0]
    tf = MOE_FF_TILE if dff % MOE_FF_TILE == 0 else dff
    nj = dff // tf
    assert tm % nj == 0
    col = lambda t, j, te, nu, src: (layer, te[t], 0, jnp.where(t < nu[0], j, nj - 1))
    rowb = lambda t, j, te, nu, src: (layer, te[t], jnp.where(t < nu[0], j, nj - 1), 0)
    return pl.pallas_call(
        functools.partial(_experts_kernel, n_tiles=n_tiles, n_steps=nj),
        grid_spec=pltpu.PrefetchScalarGridSpec(
            num_scalar_prefetch=3,
            grid=(n_tiles, nj),
            in_specs=[pl.BlockSpec(memory_space=pl.ANY),
                      pl.BlockSpec((None, None, d, tf), col),
                      pl.BlockSpec((None, None, d, tf), col),
                      pl.BlockSpec((None, None, tf, d), rowb)],
            out_specs=pl.BlockSpec((tm, d), lambda t, j, te, nu, src: (t, 0)),
            scratch_shapes=[pltpu.VMEM((2, tm, d), F32), pltpu.VMEM((tm, d), BF16),
                            pltpu.SemaphoreType.DMA((2,))]),
        out_shape=jax.ShapeDtypeStruct((n_tiles * tm, d), F32),
        compiler_params=_params(("arbitrary", "arbitrary")),
        name="moe_experts",
    )(tile_e, n_used, src, h2, wg, wu, wd)


def _combine_kernel(slot_ref, x_ref, sel_ref, mod_ref, g_ref, y_hbm, o_ref, r1_ref, r2_ref, sems,
                    *, d, m, rows_per_sample, final):
    t = x_ref.shape[0]
    base = pl.program_id(0) * rows_per_sample + pl.program_id(1) * t

    def issue(k, carry):
        pltpu.make_async_copy(y_hbm.at[pl.ds(slot_ref[base + k], 1)], r1_ref.at[pl.ds(k, 1)], sems.at[0]).start()
        pltpu.make_async_copy(y_hbm.at[pl.ds(slot_ref[m + base + k], 1)], r2_ref.at[pl.ds(k, 1)], sems.at[1]).start()
        return carry

    lax.fori_loop(0, t, issue, 0, unroll=8)
    pltpu.make_async_copy(y_hbm.at[pl.ds(0, t)], r1_ref, sems.at[0]).wait()
    pltpu.make_async_copy(y_hbm.at[pl.ds(0, t)], r2_ref, sems.at[1]).wait()
    y = sel_ref[:, 0:1] * r1_ref[...] + sel_ref[:, 1:2] * r2_ref[...]
    x = x_ref[...] + mod_ref[:, 5 * d:6 * d] * y
    if final:
        x = x * lax.rsqrt(jnp.mean(x * x, axis=-1, keepdims=True) + EPS) * g_ref[...]
    o_ref[...] = x


def _combine(xs, sel, mod_l, y_sorted, slot_of, final_g, n_lat_tiles, n_row_tiles, final):
    b, _, d = xs.shape
    t = ROW_TILE
    rows = n_row_tiles * t
    row = lambda i, r, s: (i, r, 0)
    return pl.pallas_call(
        functools.partial(_combine_kernel, d=d, m=b * rows, rows_per_sample=rows, final=final),
        grid_spec=pltpu.PrefetchScalarGridSpec(
            num_scalar_prefetch=1,
            grid=(b, n_row_tiles),
            in_specs=[pl.BlockSpec((None, t, d), row),
                      pl.BlockSpec((None, t, LANES), row),
                      pl.BlockSpec((None, 1, mod_l.shape[-1]),
                                   lambda i, r, s: (jnp.where(r < n_lat_tiles, i, b), 0, 0)),
                      pl.BlockSpec((1, d), lambda i, r, s: (0, 0)),
                      pl.BlockSpec(memory_space=pl.ANY)],
            out_specs=pl.BlockSpec((None, t, d), row),
            scratch_shapes=[pltpu.VMEM((t, d), F32), pltpu.VMEM((t, d), F32), pltpu.SemaphoreType.DMA((2,))]),
        out_shape=jax.ShapeDtypeStruct((b, rows, d), F32),
        compiler_params=_params(("arbitrary", "arbitrary")),
        name="moe_combine",
    )(slot_of, xs, sel, mod_l, final_g, y_sorted)


def _rope_tables(s_lat, n_ctx):
    rows = s_lat // GRID_W
    pos_row = jnp.repeat(jnp.arange(rows, dtype=jnp.int32), GRID_W).astype(F32)
    pos_col = jnp.tile(jnp.arange(GRID_W, dtype=jnp.int32), rows).astype(F32)
    quarter = HEAD_DIM // 4
    inv_freq = ROPE_THETA ** (-jnp.arange(quarter, dtype=F32) / quarter)
    ang_r = pos_row[:, None] * inv_freq
    ang_c = pos_col[:, None] * inv_freq
    ang = jnp.concatenate([ang_r, ang_r, ang_c, ang_c], axis=-1)
    cos, sin = jnp.cos(ang), jnp.sin(ang)
    first = (np.arange(HEAD_DIM) % (2 * quarter)) < quarter
    sin_lo = jnp.where(first, -sin, 0.0)
    sin_hi = jnp.where(first, 0.0, sin)

    def widen(tab, ctx_fill):
        tab = jnp.concatenate([tab, jnp.full((n_ctx, HEAD_DIM), ctx_fill, F32)], axis=0)
        return jnp.tile(tab, (1, LANES // HEAD_DIM))

    return widen(cos, 1.0), widen(sin_lo, 0.0), widen(sin_hi, 0.0)


def _dft_tables(n):
    k = np.arange(n, dtype=np.int64)
    ang = 2.0 * np.pi * ((k[:, None] * k[None, :]) % n).astype(np.float64) / n
    return np.cos(ang), np.sin(ang)


def _constants(s_lat, n_ctx):
    c64, s64 = _dft_tables(F_GROUP_DIM)
    eye = np.eye(F_GROUPS)
    cn, sn = _dft_tables(s_lat)
    cc, sc = _dft_tables(n_ctx)
    bf = lambda a: jnp.asarray(a, dtype=F32).astype(BF16)
    return dict(
        gmat=bf(np.kron(np.eye(512 // HEAD_DIM), np.ones((HEAD_DIM, HEAD_DIM)))),
        c64=bf(np.kron(eye, c64) / math.sqrt(F_GROUP_DIM)),
        s64=bf(np.kron(eye, s64) / math.sqrt(F_GROUP_DIM)),
        cn=bf(cn / math.sqrt(s_lat)), sn=bf(sn / math.sqrt(s_lat)),
        cc=bf(cc / math.sqrt(n_ctx)), sc=bf(sc / math.sqrt(n_ctx)),
    )


def _dup_heads(w, off, n_heads):
    cols = []
    for hd in range(n_heads):
        blk = w[:, off + hd * HEAD_DIM: off + (hd + 1) * HEAD_DIM]
        cols += [blk, blk]
    return cols


def _relayout_w_in(w):
    cols = [w[:, _OFF["gt"]:_OFF["end"]], w[:, _OFF["aq"]:_OFF["bq"]], w[:, _OFF["bq"]:_OFF["bk"]],
            w[:, _OFF["cq"]:_OFF["ck"]], w[:, _OFF["fz"]:_OFF["gt"]]]
    cols += _dup_heads(w, _OFF["bk"], B_KV) + _dup_heads(w, _OFF["bv"], B_KV)
    cols += _dup_heads(w, _OFF["ck"], C_KV) + _dup_heads(w, _OFF["cv"], C_KV)
    return jnp.concatenate(cols, axis=1).astype(BF16)


def kernel(x, c, ctx, c_ctx, w_mod, b_mod, norm1_g, norm2_g, w_in, diff_lambda, diff_subln_g, qk_norm_g,
           sink_logits, w_branch, w_out, ffn_w_gate, ffn_w_up, ffn_w_down, router_w, moe_w_gate, moe_w_up,
           moe_w_down, final_g):
    b, s_lat, d = x.shape
    n_ctx = ctx.shape[1]
    depth = w_mod.shape[0]
    nt = s_lat + n_ctx
    t = ROW_TILE
    assert s_lat % t == 0 and n_ctx % t == 0 and s_lat % n_ctx == 0 and b < MOD_ROWS
    assert w_in.shape[-1] == _OFF["end"] and d * N_BRANCH == _OFF["end"] - _OFF["gt"]
    n_lat_tiles = s_lat // t

    xs = jnp.concatenate([x, ctx], axis=1)
    cvec = jnp.concatenate([c, c_ctx[None], jnp.zeros((MOD_ROWS - b - 1, d), F32)], axis=0)
    mod = _modulation(cvec, w_mod, b_mod).reshape(depth, MOD_ROWS, 1, 6 * d)
    tabs = _rope_tables(s_lat, n_ctx)
    consts = _constants(s_lat, n_ctx)

    out = None
    for l in range(depth):
        last = l == depth - 1
        mod_l = mod[l]
        p = _inproj(xs, mod_l, norm1_g[l][None], _relayout_w_in(w_in[l]), tabs,
                    jnp.tile(qk_norm_g[l, 0], 512 // HEAD_DIM)[None], jnp.tile(qk_norm_g[l, 1], 256 // HEAD_DIM)[None],
                    consts, n_lat_tiles)
        lam_init = 0.8 - 0.6 * math.exp(-0.3 * l)
        n_q_tiles = n_lat_tiles if last else nt // t
        oa, ob, oc = _attention(p, diff_lambda[l], diff_subln_g[l][None], sink_logits[l],
                                n_lat_tiles, n_q_tiles, s_lat, lam_init)
        od = _fourier(p[10], p[11], consts, n_lat_tiles, n_q_tiles, s_lat)
        xs = _merge(xs, mod_l, p[0], (oa, ob, oc, od), w_branch[l].astype(BF16), w_out[l].astype(BF16),
                    n_lat_tiles, n_q_tiles)

        i = l // 2
        if l % 2 == 0:
            assert not last
            xs = _ffn(xs, mod_l, norm2_g[l][None], ffn_w_gate[i].astype(BF16), ffn_w_up[i].astype(BF16),
                      ffn_w_down[i].astype(BF16), n_lat_tiles)
        else:
            w_r = jnp.pad(router_w[i], ((0, 0), (0, LANES - N_EXPERTS)))
            n_row_tiles = n_q_tiles
            m = b * n_row_tiles * t
            h2, sel = _router(xs, mod_l, norm2_g[l][None], w_r, n_lat_tiles, n_row_tiles)
            tm = MOE_TILE if 2 * m >= N_EXPERTS * MOE_TILE else ROW_TILE
            tables = _dispatch(sel.reshape(m, LANES), tm)
            y_sorted = _experts(h2.reshape(m, d), tables, moe_w_gate, moe_w_up, moe_w_down, i, tm)
            res = _combine(xs, sel, mod_l, y_sorted, tables[3], final_g[None], n_lat_tiles, n_row_tiles, last)
            if last:
                out = res
            else:
                xs = res
    if out is None:
        raise NotImplementedError("final normalisation is fused into the last (mixture-of-experts) layer")
    return out
```

```python
import functools
import math

import numpy as np
import jax
import jax.numpy as jnp
from jax import lax
from jax.experimental import pallas as pl
from jax.experimental.pallas import tpu as pltpu

F32 = jnp.float32
BF16 = jnp.bfloat16

GRID_W = 64
HEAD_DIM = 64
ROPE_THETA = 10000.0
WINDOW = 128
A_HEADS = 4
B_HEADS = 8
B_KV = 2
C_HEADS = 8
C_KV = 2
F_GROUPS = 8
F_GROUP_DIM = 64
BRANCH_W = 512
N_BRANCH = 4
N_EXPERTS = 8
EPS = 1e-6
SUBLN_EPS = 1e-5
NEG_INF = -1e30

LANES = 128
SUBLANES = 8
ROW_TILE = 256
MOE_TILE = 896
MOE_FF_TILE = 512
MOD_ROWS = 16
VMEM_LIMIT = 56 * 1024 * 1024

_OFF = dict(aq=0, ak=512, av=1024, bq=1536, bk=2048, bv=2176, cq=2304, ck=2816, cv=2944,
            fz=3072, gt=3584, end=7680)
_P = dict(gt=0, aq=4096, ak=4608, av=5120, bq=5632, cq=6144, fz=6656,
          bk=7168, bv=7424, ck=7680, cv=7936, end=8192)


def _params(sem):
    return pltpu.CompilerParams(dimension_semantics=sem, vmem_limit_bytes=VMEM_LIMIT)


def _full(shape):
    n = len(shape)
    return pl.BlockSpec(shape, lambda *_: (0,) * n)


def _silu(x):
    return x / (1.0 + jnp.exp(-x))


def _sigmoid(x):
    return 1.0 / (1.0 + jnp.exp(-x))


def _adaln(x, g, shift, scale):
    y = x * lax.rsqrt(jnp.mean(x * x, axis=-1, keepdims=True) + EPS) * g
    return y * (1.0 + scale) + shift


def _mod_kernel(c_ref, w_ref, b_ref, o_ref):
    s = _silu(c_ref[...]).astype(BF16)
    o_ref[...] = jnp.dot(s, w_ref[...].astype(BF16), preferred_element_type=F32) + b_ref[...]


def _modulation(cvec, w_mod, b_mod):
    depth, d, n = w_mod.shape
    tn = 1536 if n % 1536 == 0 else n
    return pl.pallas_call(
        _mod_kernel,
        grid=(depth, n // tn),
        in_specs=[_full((MOD_ROWS, d)),
                  pl.BlockSpec((None, d, tn), lambda l, j: (l, 0, j)),
                  pl.BlockSpec((None, 1, tn), lambda l, j: (l, 0, j))],
        out_specs=pl.BlockSpec((None, MOD_ROWS, tn), lambda l, j: (l, 0, j)),
        out_shape=jax.ShapeDtypeStruct((depth, MOD_ROWS, n), F32),
        compiler_params=_params(("arbitrary", "arbitrary")),
        name="modulation",
    )(cvec, w_mod, b_mod.reshape(depth, 1, n))


def _rope(p, cos, sin_lo, sin_hi):
    outs = []
    for c in range(p.shape[1] // LANES):
        xc = p[:, c * LANES:(c + 1) * LANES]
        outs.append(xc * cos + pltpu.roll(xc, LANES - 16, 1) * sin_lo + pltpu.roll(xc, 16, 1) * sin_hi)
    return outs[0] if len(outs) == 1 else jnp.concatenate(outs, axis=1)


def _group_rmsnorm(y, gmat, g):
    ss = jnp.dot((y * y).astype(BF16), gmat, preferred_element_type=F32)
    return y * lax.rsqrt(ss * (1.0 / HEAD_DIM) + EPS) * g


def _inproj_kernel(x_ref, mod_ref, g_ref, w_ref, cos_ref, slo_ref, shi_ref, qg_ref, kg_ref, gmat_ref,
                   c64_ref, s64_ref,
                   gt_ref, aq_ref, akt_ref, av_ref, bq_ref, bkt_ref, bv_ref, cq_ref, ckt_ref, cv_ref,
                   fzc_ref, fzs_ref, *, d):
    h = _adaln(x_ref[...], g_ref[...], mod_ref[:, 0:d], mod_ref[:, d:2 * d]).astype(BF16)
    cos, slo, shi = cos_ref[...], slo_ref[...], shi_ref[...]
    scale = HEAD_DIM ** -0.5

    def proj(name, width):
        return jnp.dot(h, w_ref[:, _P[name]:_P[name] + width], preferred_element_type=F32)

    for c in range(N_BRANCH * d // 512):
        p = jnp.dot(h, w_ref[:, c * 512:(c + 1) * 512], preferred_element_type=F32)
        gt_ref[:, c * 512:(c + 1) * 512] = _sigmoid(p).astype(BF16)

    aq_ref[...] = (_rope(proj("aq", 512), cos, slo, shi) * scale).astype(BF16)
    akt_ref[...] = _rope(proj("ak", 512), cos, slo, shi).T.astype(BF16)
    av_ref[...] = proj("av", 512).astype(BF16)

    bq = _group_rmsnorm(proj("bq", 512), gmat_ref[...], qg_ref[...])
    bq_ref[...] = (_rope(bq, cos, slo, shi) * scale).astype(BF16)
    bk = _group_rmsnorm(proj("bk", 256), gmat_ref[0:256, 0:256], kg_ref[...])
    bkt_ref[...] = _rope(bk, cos, slo, shi).T.astype(BF16)
    bv_ref[...] = proj("bv", 256).astype(BF16)

    cq_ref[...] = (_rope(proj("cq", 512), cos, slo, shi) * scale).astype(BF16)
    ckt_ref[...] = _rope(proj("ck", 256), cos, slo, shi).T.astype(BF16)
    cv_ref[...] = proj("cv", 256).astype(BF16)

    fz = proj("fz", 512).astype(BF16)
    fzc_ref[...] = jnp.dot(fz, c64_ref[...], preferred_element_type=F32).astype(BF16)
    fzs_ref[...] = jnp.dot(fz, s64_ref[...], preferred_element_type=F32).astype(BF16)


def _inproj(xs, mod_l, g, w_p, tabs, qg, kg, consts, n_lat_tiles):
    b, nt, d = xs.shape
    t = ROW_TILE
    n_mod = mod_l.shape[-1]
    row = lambda i, r: (i, r, 0)
    tr = lambda i, r: (i, 0, r)
    tab = pl.BlockSpec((t, LANES), lambda i, r: (r, 0))

    def out(width):
        return pl.BlockSpec((None, t, width), row), jax.ShapeDtypeStruct((b, nt, width), BF16)

    def out_t(width):
        return pl.BlockSpec((None, width, t), tr), jax.ShapeDtypeStruct((b, width, nt), BF16)

    outs = [out(N_BRANCH * d), out(512), out_t(512), out(512), out(512), out_t(256), out(256),
            out(512), out_t(256), out(256), out(512), out(512)]
    return pl.pallas_call(
        functools.partial(_inproj_kernel, d=d),
        grid=(b, nt // t),
        in_specs=[pl.BlockSpec((None, t, d), row),
                  pl.BlockSpec((None, 1, n_mod), lambda i, r: (jnp.where(r < n_lat_tiles, i, b), 0, 0)),
                  _full((1, d)),
                  pl.BlockSpec(w_p.shape, lambda i, r: (0, 0), pipeline_mode=pl.Buffered(1)),
                  tab, tab, tab,
                  _full((1, 512)), _full((1, 256)), _full((512, 512)), _full((512, 512)), _full((512, 512))],
        out_specs=[o[0] for o in outs],
        out_shape=[o[1] for o in outs],
        compiler_params=_params(("arbitrary", "arbitrary")),
        name="inproj",
    )(xs, mod_l, g, w_p, tabs[0], tabs[1], tabs[2], qg, kg, consts["gmat"], consts["c64"], consts["s64"])


def _softmax_parts(s):
    m = jnp.max(s, axis=-1, keepdims=True)
    e = jnp.exp(s - m)
    return e, jnp.sum(e, axis=-1, keepdims=True)


def _half_masks(shape):
    lane = lax.broadcasted_iota(jnp.int32, shape, 1)
    return lane < HEAD_DIM, lane >= HEAD_DIM


def _attn_a_kernel(q_ref, kt_ref, v_ref, lp_ref, g_ref, o_ref, *, n_lat_tiles, s_lat, lam_init):
    r = pl.program_id(1)
    lp = lp_ref[...]
    lam = (jnp.exp(jnp.sum(lp[0:1] * lp[1:2], axis=-1, keepdims=True))
           - jnp.exp(jnp.sum(lp[2:3] * lp[3:4], axis=-1, keepdims=True)) + lam_init)
    g = g_ref[...] * (1.0 - lam_init)
    nt = kt_ref.shape[1]

    def body(k0):
        lo, hi = _half_masks((q_ref.shape[0], LANES))
        for hd in range(A_HEADS):
            cs = slice(hd * LANES, (hd + 1) * LANES)
            q = q_ref[:, cs]
            kt = kt_ref[cs, k0:nt]
            e1, l1 = _softmax_parts(jnp.dot(jnp.where(lo, q, 0), kt, preferred_element_type=F32))
            e2, l2 = _softmax_parts(jnp.dot(jnp.where(hi, q, 0), kt, preferred_element_type=F32))
            a = (e1 * (1.0 / l1) - e2 * (lam / l2)).astype(BF16)
            o = jnp.dot(a, v_ref[k0:nt, cs], preferred_element_type=F32)
            o = o * lax.rsqrt(jnp.mean(o * o, axis=-1, keepdims=True) + SUBLN_EPS) * g
            o_ref[:, cs] = o.astype(BF16)

    pl.when(r < n_lat_tiles)(lambda: body(0))
    pl.when(r >= n_lat_tiles)(lambda: body(s_lat))


def _attn_b_kernel(q_ref, kt_ref, v_ref, o_ref, *, n_lat_tiles, s_lat):
    r = pl.program_id(1)
    nt = kt_ref.shape[1]

    def body(k0):
        lo, hi = _half_masks((q_ref.shape[0], LANES))
        for pair in range(B_HEADS // 2):
            kv = pair // (B_HEADS // B_KV // 2)
            ks = slice(kv * LANES, (kv + 1) * LANES)
            q = q_ref[:, pair * LANES:(pair + 1) * LANES]
            kt = kt_ref[ks, k0:nt]
            v = v_ref[k0:nt, ks]
            halves = []
            for msk in (lo, hi):
                e, l = _softmax_parts(jnp.dot(jnp.where(msk, q, 0), kt, preferred_element_type=F32))
                halves.append(jnp.dot(e.astype(BF16), v, preferred_element_type=F32) * (1.0 / l))
            o_ref[:, pair * LANES:(pair + 1) * LANES] = jnp.where(lo, halves[0], halves[1]).astype(BF16)

    pl.when(r < n_lat_tiles)(lambda: body(0))
    pl.when(r >= n_lat_tiles)(lambda: body(s_lat))


def _attn_c_kernel(sink_ref, q_ref, ktl_ref, ktm_ref, ktr_ref, ktc_ref, vl_ref, vm_ref, vr_ref, vc_ref, o_ref,
                   *, n_lat_tiles, s_lat):
    r = pl.program_id(1)
    t = q_ref.shape[0]
    n_band = t + 2 * WINDOW
    n_keys = n_band + ktc_ref.shape[1]
    u = lax.broadcasted_iota(jnp.int32, (t, n_keys), 1)
    row = lax.broadcasted_iota(jnp.int32, (t, n_keys), 0)
    j = r * t - WINDOW + u
    in_band = (u >= row) & (u <= row + 2 * WINDOW) & (j >= 0) & (j < s_lat) & (r < n_lat_tiles)
    valid = in_band | (u >= n_band)
    lo, hi = _half_masks((t, LANES))
    for pair in range(C_HEADS // 2):
        kv = pair // (C_HEADS // C_KV // 2)
        ks = slice(kv * LANES, (kv + 1) * LANES)
        q = q_ref[:, pair * LANES:(pair + 1) * LANES]
        kt = jnp.concatenate([ktl_ref[ks, :], ktm_ref[ks, :], ktr_ref[ks, :], ktc_ref[ks, :]], axis=1)
        v = jnp.concatenate([vl_ref[:, ks], vm_ref[:, ks], vr_ref[:, ks], vc_ref[:, ks]], axis=0)
        halves = []
        for half, msk in enumerate((lo, hi)):
            sink = sink_ref[2 * pair + half]
            s = jnp.dot(jnp.where(msk, q, 0), kt, preferred_element_type=F32)
            s = jnp.where(valid, s, NEG_INF)
            m = jnp.maximum(jnp.max(s, axis=-1, keepdims=True), sink)
            e = jnp.exp(s - m)
            l = jnp.sum(e, axis=-1, keepdims=True) + jnp.exp(sink - m)
            halves.append(jnp.dot(e.astype(BF16), v, preferred_element_type=F32) * (1.0 / l))
        o_ref[:, pair * LANES:(pair + 1) * LANES] = jnp.where(lo, halves[0], halves[1]).astype(BF16)


def _attention(p, lp, subln_g, sink, n_lat_tiles, n_q_tiles, s_lat, lam_init):
    (_, aq, akt, av, bq, bkt, bv, cq, ckt, cv, _, _) = p
    b, nt, _ = aq.shape
    t = ROW_TILE
    grid = (b, n_q_tiles)
    row = lambda i, r: (i, r, 0)
    whole = lambda i, r: (i, 0, 0)
    sem = _params(("arbitrary", "arbitrary"))
    o_spec = pl.BlockSpec((None, t, 512), row)
    o_shape = jax.ShapeDtypeStruct((b, n_q_tiles * t, 512), BF16)

    oa = pl.pallas_call(
        functools.partial(_attn_a_kernel, n_lat_tiles=n_lat_tiles, s_lat=s_lat, lam_init=lam_init),
        grid=grid,
        in_specs=[pl.BlockSpec((None, t, 512), row), pl.BlockSpec((None, 512, nt), whole),
                  pl.BlockSpec((None, nt, 512), whole), _full((4, HEAD_DIM)), _full((1, 2 * HEAD_DIM))],
        out_specs=o_spec, out_shape=o_shape, compiler_params=sem, name="attn_diff",
    )(aq, akt, av, lp, subln_g)

    ob = pl.pallas_call(
        functools.partial(_attn_b_kernel, n_lat_tiles=n_lat_tiles, s_lat=s_lat),
        grid=grid,
        in_specs=[pl.BlockSpec((None, t, 512), row), pl.BlockSpec((None, 256, nt), whole),
                  pl.BlockSpec((None, nt, 256), whole)],
        out_specs=o_spec, out_shape=o_shape, compiler_params=sem, name="attn_gqa",
    )(bq, bkt, bv)

    wpt = t // WINDOW
    last = s_lat // WINDOW - 1
    left = lambda r: jnp.clip(r * wpt - 1, 0, last)
    mid = lambda r: jnp.minimum(r, n_lat_tiles - 1)
    right = lambda r: jnp.clip((r + 1) * wpt, 0, last)
    n_ctx = nt - s_lat
    ctx_blk = s_lat // n_ctx
    oc = pl.pallas_call(
        functools.partial(_attn_c_kernel, n_lat_tiles=n_lat_tiles, s_lat=s_lat),
        grid=grid,
        in_specs=[pl.BlockSpec(memory_space=pltpu.SMEM),
                  pl.BlockSpec((None, t, 512), row),
                  pl.BlockSpec((None, 256, WINDOW), lambda i, r: (i, 0, left(r))),
                  pl.BlockSpec((None, 256, t), lambda i, r: (i, 0, mid(r))),
                  pl.BlockSpec((None, 256, WINDOW), lambda i, r: (i, 0, right(r))),
                  pl.BlockSpec((None, 256, n_ctx), lambda i, r: (i, 0, ctx_blk)),
                  pl.BlockSpec((None, WINDOW, 256), lambda i, r: (i, left(r), 0)),
                  pl.BlockSpec((None, t, 256), lambda i, r: (i, mid(r), 0)),
                  pl.BlockSpec((None, WINDOW, 256), lambda i, r: (i, right(r), 0)),
                  pl.BlockSpec((None, n_ctx, 256), lambda i, r: (i, ctx_blk, 0))],
        out_specs=o_spec, out_shape=o_shape, compiler_params=sem, name="attn_window",
    )(sink, cq, ckt, ckt, ckt, ckt, cv, cv, cv, cv)
    return oa, ob, oc


def _fourier_kernel(cn_ref, sn_ref, cc_ref, sc_ref, zc_ref, zs_ref, o_ref, *, n_lat_tiles, s_lat):
    r = pl.program_id(1)
    nt = zc_ref.shape[0]

    @pl.when(r < n_lat_tiles)
    def _():
        y = (jnp.dot(cn_ref[...], zc_ref[0:s_lat, :], preferred_element_type=F32)
             - jnp.dot(sn_ref[...], zs_ref[0:s_lat, :], preferred_element_type=F32))
        o_ref[...] = y.astype(BF16)

    @pl.when(r >= n_lat_tiles)
    def _():
        y = (jnp.dot(cc_ref[...], zc_ref[s_lat:nt, :], preferred_element_type=F32)
             - jnp.dot(sc_ref[...], zs_ref[s_lat:nt, :], preferred_element_type=F32))
        o_ref[...] = y.astype(BF16)


def _fourier(fzc, fzs, consts, n_lat_tiles, n_q_tiles, s_lat):
    b, nt, w = fzc.shape
    t = ROW_TILE
    n_ctx = nt - s_lat
    lat_rows = lambda i, r: (jnp.minimum(r, n_lat_tiles - 1), 0)
    ctx_rows = lambda i, r: (jnp.maximum(r - n_lat_tiles, 0), 0)
    whole = lambda i, r: (i, 0, 0)
    return pl.pallas_call(
        functools.partial(_fourier_kernel, n_lat_tiles=n_lat_tiles, s_lat=s_lat),
        grid=(b, n_q_tiles),
        in_specs=[pl.BlockSpec((t, s_lat), lat_rows), pl.BlockSpec((t, s_lat), lat_rows),
                  pl.BlockSpec((t, n_ctx), ctx_rows), pl.BlockSpec((t, n_ctx), ctx_rows),
                  pl.BlockSpec((None, nt, w), whole), pl.BlockSpec((None, nt, w), whole)],
        out_specs=pl.BlockSpec((None, t, w), lambda i, r: (i, r, 0)),
        out_shape=jax.ShapeDtypeStruct((b, n_q_tiles * t, w), BF16),
        compiler_params=_params(("arbitrary", "arbitrary")),
        name="fourier",
    )(consts["cn"], consts["sn"], consts["cc"], consts["sc"], fzc, fzs)


def _merge_kernel(x_ref, mod_ref, gt_ref, oa_ref, ob_ref, oc_ref, od_ref, wbr_ref, wo_ref, o_ref, *, d):
    merged = None
    for i, br in enumerate((oa_ref, ob_ref, oc_ref, od_ref)):
        y = jnp.dot(br[...], wbr_ref[i], preferred_element_type=F32)
        y = gt_ref[:, i * d:(i + 1) * d].astype(F32) * y
        merged = y if merged is None else merged + y
    out = jnp.dot(merged.astype(BF16), wo_ref[...], preferred_element_type=F32)
    o_ref[...] = x_ref[...] + mod_ref[:, 2 * d:3 * d] * out


def _merge(xs, mod_l, gates, branches, w_br, w_o, n_lat_tiles, n_q_tiles):
    b, _, d = xs.shape
    t = ROW_TILE
    row = lambda i, r: (i, r, 0)
    br_spec = pl.BlockSpec((None, t, BRANCH_W), row)
    return pl.pallas_call(
        functools.partial(_merge_kernel, d=d),
        grid=(b, n_q_tiles),
        in_specs=[pl.BlockSpec((None, t, d), row),
                  pl.BlockSpec((None, 1, mod_l.shape[-1]), lambda i, r: (jnp.where(r < n_lat_tiles, i, b), 0, 0)),
                  pl.BlockSpec((None, t, N_BRANCH * d), row),
                  br_spec, br_spec, br_spec, br_spec,
                  _full(w_br.shape), _full(w_o.shape)],
        out_specs=pl.BlockSpec((None, t, d), row),
        out_shape=jax.ShapeDtypeStruct((b, n_q_tiles * t, d), F32),
        compiler_params=_params(("arbitrary", "arbitrary")),
        name="merge",
    )(xs, mod_l, gates, *branches, w_br, w_o)


def _ffn_kernel(x_ref, mod_ref, g_ref, wg_ref, wu_ref, wd_ref, o_ref, *, d):
    x = x_ref[...]
    h = _adaln(x, g_ref[...], mod_ref[:, 3 * d:4 * d], mod_ref[:, 4 * d:5 * d]).astype(BF16)
    a = jnp.dot(h, wg_ref[...], preferred_element_type=F32)
    u = jnp.dot(h, wu_ref[...], preferred_element_type=F32)
    act = (_silu(a) * u).astype(BF16)
    y = jnp.dot(act, wd_ref[...], preferred_element_type=F32)
    o_ref[...] = x + mod_ref[:, 5 * d:6 * d] * y


def _ffn(xs, mod_l, g, wg, wu, wd, n_lat_tiles):
    b, nt, d = xs.shape
    t = ROW_TILE
    row = lambda i, r: (i, r, 0)
    const = lambda shape: pl.BlockSpec(shape, lambda i, r: (0, 0), pipeline_mode=pl.Buffered(1))
    return pl.pallas_call(
        functools.partial(_ffn_kernel, d=d),
        grid=(b, nt // t),
        in_specs=[pl.BlockSpec((None, t, d), row),
                  pl.BlockSpec((None, 1, mod_l.shape[-1]), lambda i, r: (jnp.where(r < n_lat_tiles, i, b), 0, 0)),
                  _full((1, d)), const(wg.shape), const(wu.shape), const(wd.shape)],
        out_specs=pl.BlockSpec((None, t, d), row),
        out_shape=jax.ShapeDtypeStruct(xs.shape, F32),
        compiler_params=_params(("arbitrary", "arbitrary")),
        name="ffn_dense",
    )(xs, mod_l, g, wg, wu, wd)


def _store_token_major(ref, val):
    rows = val.shape[0]
    for c in range(SUBLANES):
        ref[pl.ds(c, rows, stride=SUBLANES), :] = val[:, c * LANES:(c + 1) * LANES]


def _load_token_major(ref, rows):
    return jnp.concatenate([ref[pl.ds(c, rows, stride=SUBLANES), :] for c in range(SUBLANES)], axis=1)


def _router_kernel(x_ref, mod_ref, g_ref, wr_ref, h_ref, sel_ref, *, d):
    h = _adaln(x_ref[...], g_ref[...], mod_ref[:, 3 * d:4 * d], mod_ref[:, 4 * d:5 * d])
    _store_token_major(h_ref, h)
    logits = jnp.dot(h, wr_ref[...], preferred_element_type=F32, precision=lax.Precision.HIGHEST)
    lane = lax.broadcasted_iota(jnp.int32, logits.shape, 1)
    logits = jnp.where(lane < N_EXPERTS, logits, -jnp.inf)
    m1 = jnp.max(logits, axis=-1, keepdims=True)
    i1 = jnp.min(jnp.where(logits == m1, lane, LANES), axis=-1, keepdims=True)
    rest = jnp.where(lane == i1, -jnp.inf, logits)
    m2 = jnp.max(rest, axis=-1, keepdims=True)
    i2 = jnp.min(jnp.where(rest == m2, lane, LANES), axis=-1, keepdims=True)
    e2 = jnp.exp(m2 - m1)
    w1 = 1.0 / (1.0 + e2)
    sel_ref[...] = (jnp.where(lane == 0, w1, 0.0) + jnp.where(lane == 1, e2 * w1, 0.0)
                    + jnp.where(lane == 2, i1.astype(F32), 0.0) + jnp.where(lane == 3, i2.astype(F32), 0.0))


def _router(xs, mod_l, g, w_r, n_lat_tiles, n_row_tiles):
    b, _, d = xs.shape
    t = ROW_TILE
    row = lambda i, r: (i, r, 0)
    return pl.pallas_call(
        functools.partial(_router_kernel, d=d),
        grid=(b, n_row_tiles),
        in_specs=[pl.BlockSpec((None, t, d), row),
                  pl.BlockSpec((None, 1, mod_l.shape[-1]), lambda i, r: (jnp.where(r < n_lat_tiles, i, b), 0, 0)),
                  _full((1, d)), _full(w_r.shape)],
        out_specs=[pl.BlockSpec((None, t * SUBLANES, LANES), row), pl.BlockSpec((None, t, LANES), row)],
        out_shape=[jax.ShapeDtypeStruct((b, n_row_tiles * t * SUBLANES, LANES), F32),
                   jax.ShapeDtypeStruct((b, n_row_tiles * t, LANES), F32)],
        compiler_params=_params(("arbitrary", "arbitrary")),
        name="moe_router",
    )(xs, mod_l, g, w_r)


def _dispatch(sel, tm):
    m = sel.shape[0]
    i32 = jnp.int32
    experts = jnp.arange(N_EXPERTS, dtype=i32)

    def lookup(table, idx):
        return jnp.sum(jnp.where(idx[:, None] == experts[None, :], table[None, :], 0), axis=1)

    idx = sel[:, 2:4].astype(i32)
    e_flat = jnp.concatenate([idx[:, 0], idx[:, 1]])
    assignment = jnp.arange(2 * m, dtype=i32)
    _, order = lax.sort_key_val(e_flat, assignment)
    _, rank = lax.sort_key_val(order, assignment)
    counts = jnp.sum((e_flat[:, None] == experts[None, :]).astype(i32), axis=0)
    padded = (counts + tm - 1) // tm * tm
    pad_end = jnp.cumsum(padded)
    pad_start = pad_end - padded
    start = jnp.cumsum(counts) - counts
    n_tiles = 2 * m // tm + N_EXPERTS
    n_used = pad_end[-1] // tm
    tile = jnp.minimum(jnp.arange(n_tiles, dtype=i32), n_used - 1)
    tile_e = jnp.sum((tile[:, None] * tm >= pad_end[None, :]).astype(i32), axis=1)
    slot = jnp.arange(n_tiles * tm, dtype=i32)
    slot_e = jnp.sum((jnp.minimum(slot, pad_end[-1] - 1)[:, None] >= pad_end[None, :]).astype(i32), axis=1)
    pos = slot - lookup(pad_start, slot_e)
    valid = (pos < lookup(counts, slot_e)) & (slot < pad_end[-1])
    src = jnp.where(valid, order[jnp.clip(lookup(start, slot_e) + pos, 0, 2 * m - 1)] % m, 0)
    slot_of = lookup(pad_start - start, e_flat) + rank
    return (tile_e, n_used.reshape(1).astype(i32), (src * SUBLANES).astype(i32), (slot_of * SUBLANES).astype(i32))


def _experts_kernel(te_ref, nu_ref, src_ref, h_hbm, wg_ref, wu_ref, wd_ref, o_ref, xg_ref, xb_ref, acc_ref, sems,
                    *, n_tiles, n_steps):
    t, j = pl.program_id(0), pl.program_id(1)
    tm = xb_ref.shape[0]
    rows_per_step = tm // n_steps
    n_used = nu_ref[0]
    used = t < n_used
    buf = t % 2

    def start_row(tile, dst_buf, r):
        src = pl.multiple_of(src_ref[tile * tm + r], SUBLANES)
        dst = pl.multiple_of(r * SUBLANES, SUBLANES)
        pltpu.make_async_copy(h_hbm.at[pl.ds(src, SUBLANES)], xg_ref.at[dst_buf, pl.ds(dst, SUBLANES)],
                              sems.at[dst_buf]).start()

    def wait_rows(b_):
        pltpu.make_async_copy(h_hbm.at[pl.ds(0, tm * SUBLANES)], xg_ref.at[b_], sems.at[b_]).wait()

    @pl.when((t == 0) & (j == 0))
    def _():
        def issue(r, carry):
            start_row(0, 0, r)
            return carry

        lax.fori_loop(0, tm, issue, 0, unroll=8)

    @pl.when((j == 0) & ((t == 0) | (t - 1 < n_used)))
    def _():
        wait_rows(buf)
        xb_ref[...] = _load_token_major(xg_ref.at[buf], tm).astype(BF16)

    @pl.when(j == 0)
    def _():
        acc_ref[...] = jnp.zeros_like(acc_ref)

    @pl.when(used)
    def _():
        nxt = jnp.minimum(t + 1, n_tiles - 1)
        for r in range(rows_per_step):
            start_row(nxt, 1 - buf, j * rows_per_step + r)
        h = xb_ref[...]
        a = jnp.dot(h, wg_ref[...].astype(BF16), preferred_element_type=F32)
        u = jnp.dot(h, wu_ref[...].astype(BF16), preferred_element_type=F32)
        act = (_silu(a) * u).astype(BF16)
        acc_ref[...] += jnp.dot(act, wd_ref[...].astype(BF16), preferred_element_type=F32)

    @pl.when(j == n_steps - 1)
    def _():
        _store_token_major(o_ref, acc_ref[...])

    @pl.when(used & (t == n_tiles - 1) & (j == n_steps - 1))
    def _():
        wait_rows(1 - buf)


def _experts(h2, tables, wg, wu, wd, layer, tm):
    d, dff = wg.shape[-2:]
    assert d == SUBLANES * LANES
    tile_e, n_used, src, _ = tables
    n_tiles = tile_e.shape[0]
    tf = MOE_FF_TILE if dff % MOE_FF_TILE == 0 else dff
    nj = dff // tf
    assert tm % nj == 0
    col = lambda t, j, te, nu, src: (layer, te[t], 0, jnp.where(t < nu[0], j, nj - 1))
    rowb = lambda t, j, te, nu, src: (layer, te[t], jnp.where(t < nu[0], j, nj - 1), 0)
    return pl.pallas_call(
        functools.partial(_experts_kernel, n_tiles=n_tiles, n_steps=nj),
        grid_spec=pltpu.PrefetchScalarGridSpec(
            num_scalar_prefetch=3,
            grid=(n_tiles, nj),
            in_specs=[pl.BlockSpec(memory_space=pl.ANY),
                      pl.BlockSpec((None, None, d, tf), col),
                      pl.BlockSpec((None, None, d, tf), col),
                      pl.BlockSpec((None, None, tf, d), rowb)],
            out_specs=pl.BlockSpec((tm * SUBLANES, LANES), lambda t, j, te, nu, src: (t, 0)),
            scratch_shapes=[pltpu.VMEM((2, tm * SUBLANES, LANES), F32), pltpu.VMEM((tm, d), BF16),
                            pltpu.VMEM((tm, d), F32), pltpu.SemaphoreType.DMA((2,))]),
        out_shape=jax.ShapeDtypeStruct((n_tiles * tm * SUBLANES, LANES), F32),
        compiler_params=_params(("arbitrary", "arbitrary")),
        name="moe_experts",
    )(tile_e, n_used, src, h2, wg, wu, wd)


def _combine_kernel(slot_ref, x_ref, sel_ref, mod_ref, g_ref, y_hbm, o_ref, r1_ref, r2_ref, sems,
                    *, d, m, rows_per_sample, final):
    t = x_ref.shape[0]
    base = pl.program_id(0) * rows_per_sample + pl.program_id(1) * t

    def issue(k, carry):
        dst = pl.ds(pl.multiple_of(k * SUBLANES, SUBLANES), SUBLANES)
        s1 = pl.multiple_of(slot_ref[base + k], SUBLANES)
        s2 = pl.multiple_of(slot_ref[m + base + k], SUBLANES)
        pltpu.make_async_copy(y_hbm.at[pl.ds(s1, SUBLANES)], r1_ref.at[dst], sems.at[0]).start()
        pltpu.make_async_copy(y_hbm.at[pl.ds(s2, SUBLANES)], r2_ref.at[dst], sems.at[1]).start()
        return carry

    lax.fori_loop(0, t, issue, 0, unroll=8)
    pltpu.make_async_copy(y_hbm.at[pl.ds(0, t * SUBLANES)], r1_ref, sems.at[0]).wait()
    pltpu.make_async_copy(y_hbm.at[pl.ds(0, t * SUBLANES)], r2_ref, sems.at[1]).wait()
    y = sel_ref[:, 0:1] * _load_token_major(r1_ref, t) + sel_ref[:, 1:2] * _load_token_major(r2_ref, t)
    x = x_ref[...] + mod_ref[:, 5 * d:6 * d] * y
    if final:
        x = x * lax.rsqrt(jnp.mean(x * x, axis=-1, keepdims=True) + EPS) * g_ref[...]
    o_ref[...] = x


def _combine(xs, sel, mod_l, y_sorted, slot_of, final_g, n_lat_tiles, n_row_tiles, final):
    b, _, d = xs.shape
    t = ROW_TILE
    rows = n_row_tiles * t
    row = lambda i, r, s: (i, r, 0)
    return pl.pallas_call(
        functools.partial(_combine_kernel, d=d, m=b * rows, rows_per_sample=rows, final=final),
        grid_spec=pltpu.PrefetchScalarGridSpec(
            num_scalar_prefetch=1,
            grid=(b, n_row_tiles),
            in_specs=[pl.BlockSpec((None, t, d), row),
                      pl.BlockSpec((None, t, LANES), row),
                      pl.BlockSpec((None, 1, mod_l.shape[-1]),
                                   lambda i, r, s: (jnp.where(r < n_lat_tiles, i, b), 0, 0)),
                      pl.BlockSpec((1, d), lambda i, r, s: (0, 0)),
                      pl.BlockSpec(memory_space=pl.ANY)],
            out_specs=pl.BlockSpec((None, t, d), row),
            scratch_shapes=[pltpu.VMEM((t * SUBLANES, LANES), F32), pltpu.VMEM((t * SUBLANES, LANES), F32),
                            pltpu.SemaphoreType.DMA((2,))]),
        out_shape=jax.ShapeDtypeStruct((b, rows, d), F32),
        compiler_params=_params(("arbitrary", "arbitrary")),
        name="moe_combine",
    )(slot_of, xs, sel, mod_l, final_g, y_sorted)


def _rope_tables(s_lat, n_ctx):
    rows = s_lat // GRID_W
    pos_row = jnp.repeat(jnp.arange(rows, dtype=jnp.int32), GRID_W).astype(F32)
    pos_col = jnp.tile(jnp.arange(GRID_W, dtype=jnp.int32), rows).astype(F32)
    quarter = HEAD_DIM // 4
    inv_freq = ROPE_THETA ** (-jnp.arange(quarter, dtype=F32) / quarter)
    ang_r = pos_row[:, None] * inv_freq
    ang_c = pos_col[:, None] * inv_freq
    ang = jnp.concatenate([ang_r, ang_r, ang_c, ang_c], axis=-1)
    cos, sin = jnp.cos(ang), jnp.sin(ang)
    first = (np.arange(HEAD_DIM) % (2 * quarter)) < quarter
    sin_lo = jnp.where(first, -sin, 0.0)
    sin_hi = jnp.where(first, 0.0, sin)

    def widen(tab, ctx_fill):
        tab = jnp.concatenate([tab, jnp.full((n_ctx, HEAD_DIM), ctx_fill, F32)], axis=0)
        return jnp.tile(tab, (1, LANES // HEAD_DIM))

    return widen(cos, 1.0), widen(sin_lo, 0.0), widen(sin_hi, 0.0)


def _dft_tables(n):
    k = np.arange(n, dtype=np.int64)
    ang = 2.0 * np.pi * ((k[:, None] * k[None, :]) % n).astype(np.float64) / n
    return np.cos(ang), np.sin(ang)


def _constants(s_lat, n_ctx):
    c64, s64 = _dft_tables(F_GROUP_DIM)
    eye = np.eye(F_GROUPS)
    cn, sn = _dft_tables(s_lat)
    cc, sc = _dft_tables(n_ctx)
    bf = lambda a: jnp.asarray(a, dtype=F32).astype(BF16)
    return dict(
        gmat=bf(np.kron(np.eye(512 // HEAD_DIM), np.ones((HEAD_DIM, HEAD_DIM)))),
        c64=bf(np.kron(eye, c64) / math.sqrt(F_GROUP_DIM)),
        s64=bf(np.kron(eye, s64) / math.sqrt(F_GROUP_DIM)),
        cn=bf(cn / math.sqrt(s_lat)), sn=bf(sn / math.sqrt(s_lat)),
        cc=bf(cc / math.sqrt(n_ctx)), sc=bf(sc / math.sqrt(n_ctx)),
    )


def _dup_heads(w, off, n_heads):
    cols = []
    for hd in range(n_heads):
        blk = w[:, off + hd * HEAD_DIM: off + (hd + 1) * HEAD_DIM]
        cols += [blk, blk]
    return cols


def _relayout_w_in(w):
    cols = [w[:, _OFF["gt"]:_OFF["end"]], w[:, _OFF["aq"]:_OFF["bq"]], w[:, _OFF["bq"]:_OFF["bk"]],
            w[:, _OFF["cq"]:_OFF["ck"]], w[:, _OFF["fz"]:_OFF["gt"]]]
    cols += _dup_heads(w, _OFF["bk"], B_KV) + _dup_heads(w, _OFF["bv"], B_KV)
    cols += _dup_heads(w, _OFF["ck"], C_KV) + _dup_heads(w, _OFF["cv"], C_KV)
    return jnp.concatenate(cols, axis=1).astype(BF16)


def kernel(x, c, ctx, c_ctx, w_mod, b_mod, norm1_g, norm2_g, w_in, diff_lambda, diff_subln_g, qk_norm_g,
           sink_logits, w_branch, w_out, ffn_w_gate, ffn_w_up, ffn_w_down, router_w, moe_w_gate, moe_w_up,
           moe_w_down, final_g):
    b, s_lat, d = x.shape
    n_ctx = ctx.shape[1]
    depth = w_mod.shape[0]
    nt = s_lat + n_ctx
    t = ROW_TILE
    assert s_lat % t == 0 and n_ctx % t == 0 and s_lat % n_ctx == 0 and b < MOD_ROWS
    assert w_in.shape[-1] == _OFF["end"] and d * N_BRANCH == _OFF["end"] - _OFF["gt"]
    n_lat_tiles = s_lat // t

    xs = jnp.concatenate([x, ctx], axis=1)
    cvec = jnp.concatenate([c, c_ctx[None], jnp.zeros((MOD_ROWS - b - 1, d), F32)], axis=0)
    mod = _modulation(cvec, w_mod, b_mod).reshape(depth, MOD_ROWS, 1, 6 * d)
    tabs = _rope_tables(s_lat, n_ctx)
    consts = _constants(s_lat, n_ctx)

    out = None
    for l in range(depth):
        last = l == depth - 1
        mod_l = mod[l]
        p = _inproj(xs, mod_l, norm1_g[l][None], _relayout_w_in(w_in[l]), tabs,
                    jnp.tile(qk_norm_g[l, 0], 512 // HEAD_DIM)[None], jnp.tile(qk_norm_g[l, 1], 256 // HEAD_DIM)[None],
                    consts, n_lat_tiles)
        lam_init = 0.8 - 0.6 * math.exp(-0.3 * l)
        n_q_tiles = n_lat_tiles if last else nt // t
        oa, ob, oc = _attention(p, diff_lambda[l], diff_subln_g[l][None], sink_logits[l],
                                n_lat_tiles, n_q_tiles, s_lat, lam_init)
        od = _fourier(p[10], p[11], consts, n_lat_tiles, n_q_tiles, s_lat)
        xs = _merge(xs, mod_l, p[0], (oa, ob, oc, od), w_branch[l].astype(BF16), w_out[l].astype(BF16),
                    n_lat_tiles, n_q_tiles)

        i = l // 2
        if l % 2 == 0:
            assert not last
            xs = _ffn(xs, mod_l, norm2_g[l][None], ffn_w_gate[i].astype(BF16), ffn_w_up[i].astype(BF16),
                      ffn_w_down[i].astype(BF16), n_lat_tiles)
        else:
            w_r = jnp.pad(router_w[i], ((0, 0), (0, LANES - N_EXPERTS)))
            n_row_tiles = n_q_tiles
            m = b * n_row_tiles * t
            h2, sel = _router(xs, mod_l, norm2_g[l][None], w_r, n_lat_tiles, n_row_tiles)
            tm = MOE_TILE if 2 * m >= N_EXPERTS * MOE_TILE else ROW_TILE
            tables = _dispatch(sel.reshape(m, LANES), tm)
            y_sorted = _experts(h2.reshape(m * SUBLANES, LANES), tables, moe_w_gate, moe_w_up, moe_w_down, i, tm)
            res = _combine(xs, sel, mod_l, y_sorted, tables[3], final_g[None], n_lat_tiles, n_row_tiles, last)
            if last:
                out = res
            else:
                xs = res
    if out is None:
        raise NotImplementedError("final normalisation is fused into the last (mixture-of-experts) layer")
    return out
```

```python
import functools
import math

import numpy as np
import jax
import jax.numpy as jnp
from jax import lax
from jax.experimental import pallas as pl
from jax.experimental.pallas import tpu as pltpu

F32 = jnp.float32
BF16 = jnp.bfloat16

GRID_W = 64
HEAD_DIM = 64
ROPE_THETA = 10000.0
WINDOW = 128
A_HEADS = 4
B_HEADS = 8
B_KV = 2
C_HEADS = 8
C_KV = 2
F_GROUPS = 8
F_GROUP_DIM = 64
BRANCH_W = 512
N_BRANCH = 4
N_EXPERTS = 8
EPS = 1e-6
SUBLN_EPS = 1e-5
NEG_INF = -1e30

LANES = 128
SUBLANES = 8
ROW_TILE = 256
MOE_TILES_PER_EXPERT = 4
MOE_FF_TILE = 512
MOD_ROWS = 16
VMEM_LIMIT = 56 * 1024 * 1024

_OFF = dict(aq=0, ak=512, av=1024, bq=1536, bk=2048, bv=2176, cq=2304, ck=2816, cv=2944,
            fz=3072, gt=3584, end=7680)
_P = dict(gt=0, aq=4096, ak=4608, av=5120, bq=5632, cq=6144, fz=6656,
          bk=7168, bv=7424, ck=7680, cv=7936, end=8192)


def _params(sem):
    return pltpu.CompilerParams(dimension_semantics=sem, vmem_limit_bytes=VMEM_LIMIT)


def _full(shape):
    n = len(shape)
    return pl.BlockSpec(shape, lambda *_: (0,) * n)


def _silu(x):
    return x / (1.0 + jnp.exp(-x))


def _sigmoid(x):
    return 1.0 / (1.0 + jnp.exp(-x))


def _adaln(x, g, shift, scale):
    y = x * lax.rsqrt(jnp.mean(x * x, axis=-1, keepdims=True) + EPS) * g
    return y * (1.0 + scale) + shift


def _mod_kernel(c_ref, w_ref, b_ref, o_ref):
    s = _silu(c_ref[...]).astype(BF16)
    o_ref[...] = jnp.dot(s, w_ref[...].astype(BF16), preferred_element_type=F32) + b_ref[...]


def _modulation(cvec, w_mod, b_mod):
    depth, d, n = w_mod.shape
    tn = 1536 if n % 1536 == 0 else n
    return pl.pallas_call(
        _mod_kernel,
        grid=(depth, n // tn),
        in_specs=[_full((MOD_ROWS, d)),
                  pl.BlockSpec((None, d, tn), lambda l, j: (l, 0, j)),
                  pl.BlockSpec((None, 1, tn), lambda l, j: (l, 0, j))],
        out_specs=pl.BlockSpec((None, MOD_ROWS, tn), lambda l, j: (l, 0, j)),
        out_shape=jax.ShapeDtypeStruct((depth, MOD_ROWS, n), F32),
        compiler_params=_params(("arbitrary", "arbitrary")),
        name="modulation",
    )(cvec, w_mod, b_mod.reshape(depth, 1, n))


def _rope(p, cos, sin_lo, sin_hi):
    outs = []
    for c in range(p.shape[1] // LANES):
        xc = p[:, c * LANES:(c + 1) * LANES]
        outs.append(xc * cos + pltpu.roll(xc, LANES - 16, 1) * sin_lo + pltpu.roll(xc, 16, 1) * sin_hi)
    return outs[0] if len(outs) == 1 else jnp.concatenate(outs, axis=1)


def _group_rmsnorm(y, gmat, g):
    ss = jnp.dot((y * y).astype(BF16), gmat, preferred_element_type=F32)
    return y * lax.rsqrt(ss * (1.0 / HEAD_DIM) + EPS) * g


def _inproj_kernel(x_ref, mod_ref, g_ref, w_ref, cos_ref, slo_ref, shi_ref, qg_ref, kg_ref, gmat_ref,
                   c64_ref, s64_ref,
                   gt_ref, aq_ref, akt_ref, av_ref, bq_ref, bkt_ref, bv_ref, cq_ref, ckt_ref, cv_ref,
                   fzc_ref, fzs_ref, *, d):
    h = _adaln(x_ref[...], g_ref[...], mod_ref[:, 0:d], mod_ref[:, d:2 * d]).astype(BF16)
    cos, slo, shi = cos_ref[...], slo_ref[...], shi_ref[...]
    scale = HEAD_DIM ** -0.5

    def proj(name, width):
        return jnp.dot(h, w_ref[:, _P[name]:_P[name] + width], preferred_element_type=F32)

    for c in range(N_BRANCH * d // 512):
        p = jnp.dot(h, w_ref[:, c * 512:(c + 1) * 512], preferred_element_type=F32)
        gt_ref[:, c * 512:(c + 1) * 512] = _sigmoid(p).astype(BF16)

    aq_ref[...] = (_rope(proj("aq", 512), cos, slo, shi) * scale).astype(BF16)
    akt_ref[...] = _rope(proj("ak", 512), cos, slo, shi).T.astype(BF16)
    av_ref[...] = proj("av", 512).astype(BF16)

    bq = _group_rmsnorm(proj("bq", 512), gmat_ref[...], qg_ref[...])
    bq_ref[...] = (_rope(bq, cos, slo, shi) * scale).astype(BF16)
    bk = _group_rmsnorm(proj("bk", 256), gmat_ref[0:256, 0:256], kg_ref[...])
    bkt_ref[...] = _rope(bk, cos, slo, shi).T.astype(BF16)
    bv_ref[...] = proj("bv", 256).astype(BF16)

    cq_ref[...] = (_rope(proj("cq", 512), cos, slo, shi) * scale).astype(BF16)
    ckt_ref[...] = _rope(proj("ck", 256), cos, slo, shi).T.astype(BF16)
    cv_ref[...] = proj("cv", 256).astype(BF16)

    fz = proj("fz", 512).astype(BF16)
    fzc_ref[...] = jnp.dot(fz, c64_ref[...], preferred_element_type=F32).astype(BF16)
    fzs_ref[...] = jnp.dot(fz, s64_ref[...], preferred_element_type=F32).astype(BF16)


def _inproj(xs, mod_l, g, w_p, tabs, qg, kg, consts, n_lat_tiles):
    b, nt, d = xs.shape
    t = ROW_TILE
    n_mod = mod_l.shape[-1]
    row = lambda i, r: (i, r, 0)
    tr = lambda i, r: (i, 0, r)
    tab = pl.BlockSpec((t, LANES), lambda i, r: (r, 0))

    def out(width):
        return pl.BlockSpec((None, t, width), row), jax.ShapeDtypeStruct((b, nt, width), BF16)

    def out_t(width):
        return pl.BlockSpec((None, width, t), tr), jax.ShapeDtypeStruct((b, width, nt), BF16)

    outs = [out(N_BRANCH * d), out(512), out_t(512), out(512), out(512), out_t(256), out(256),
            out(512), out_t(256), out(256), out(512), out(512)]
    return pl.pallas_call(
        functools.partial(_inproj_kernel, d=d),
        grid=(b, nt // t),
        in_specs=[pl.BlockSpec((None, t, d), row),
                  pl.BlockSpec((None, 1, n_mod), lambda i, r: (jnp.where(r < n_lat_tiles, i, b), 0, 0)),
                  _full((1, d)),
                  pl.BlockSpec(w_p.shape, lambda i, r: (0, 0), pipeline_mode=pl.Buffered(1)),
                  tab, tab, tab,
                  _full((1, 512)), _full((1, 256)), _full((512, 512)), _full((512, 512)), _full((512, 512))],
        out_specs=[o[0] for o in outs],
        out_shape=[o[1] for o in outs],
        compiler_params=_params(("arbitrary", "arbitrary")),
        name="inproj",
    )(xs, mod_l, g, w_p, tabs[0], tabs[1], tabs[2], qg, kg, consts["gmat"], consts["c64"], consts["s64"])


def _softmax_parts(s):
    m = jnp.max(s, axis=-1, keepdims=True)
    e = jnp.exp(s - m)
    return e, jnp.sum(e, axis=-1, keepdims=True)


def _half_masks(shape):
    lane = lax.broadcasted_iota(jnp.int32, shape, 1)
    return lane < HEAD_DIM, lane >= HEAD_DIM


def _attn_a_kernel(q_ref, kt_ref, v_ref, lp_ref, g_ref, o_ref, *, n_lat_tiles, s_lat, lam_init):
    r = pl.program_id(1)
    lp = lp_ref[...]
    lam = (jnp.exp(jnp.sum(lp[0:1] * lp[1:2], axis=-1, keepdims=True))
           - jnp.exp(jnp.sum(lp[2:3] * lp[3:4], axis=-1, keepdims=True)) + lam_init)
    g = g_ref[...] * (1.0 - lam_init)
    nt = kt_ref.shape[1]

    def body(k0):
        lo, hi = _half_masks((q_ref.shape[0], LANES))
        for hd in range(A_HEADS):
            cs = slice(hd * LANES, (hd + 1) * LANES)
            q = q_ref[:, cs]
            kt = kt_ref[cs, k0:nt]
            e1, l1 = _softmax_parts(jnp.dot(jnp.where(lo, q, 0), kt, preferred_element_type=F32))
            e2, l2 = _softmax_parts(jnp.dot(jnp.where(hi, q, 0), kt, preferred_element_type=F32))
            a = (e1 * (1.0 / l1) - e2 * (lam / l2)).astype(BF16)
            o = jnp.dot(a, v_ref[k0:nt, cs], preferred_element_type=F32)
            o = o * lax.rsqrt(jnp.mean(o * o, axis=-1, keepdims=True) + SUBLN_EPS) * g
            o_ref[:, cs] = o.astype(BF16)

    pl.when(r < n_lat_tiles)(lambda: body(0))
    pl.when(r >= n_lat_tiles)(lambda: body(s_lat))


def _attn_b_kernel(q_ref, kt_ref, v_ref, o_ref, *, n_lat_tiles, s_lat):
    r = pl.program_id(1)
    nt = kt_ref.shape[1]

    def body(k0):
        lo, hi = _half_masks((q_ref.shape[0], LANES))
        for pair in range(B_HEADS // 2):
            kv = pair // (B_HEADS // B_KV // 2)
            ks = slice(kv * LANES, (kv + 1) * LANES)
            q = q_ref[:, pair * LANES:(pair + 1) * LANES]
            kt = kt_ref[ks, k0:nt]
            v = v_ref[k0:nt, ks]
            halves = []
            for msk in (lo, hi):
                e, l = _softmax_parts(jnp.dot(jnp.where(msk, q, 0), kt, preferred_element_type=F32))
                halves.append(jnp.dot(e.astype(BF16), v, preferred_element_type=F32) * (1.0 / l))
            o_ref[:, pair * LANES:(pair + 1) * LANES] = jnp.where(lo, halves[0], halves[1]).astype(BF16)

    pl.when(r < n_lat_tiles)(lambda: body(0))
    pl.when(r >= n_lat_tiles)(lambda: body(s_lat))


def _attn_c_kernel(sink_ref, q_ref, ktl_ref, ktm_ref, ktr_ref, ktc_ref, vl_ref, vm_ref, vr_ref, vc_ref, o_ref,
                   *, n_lat_tiles, s_lat):
    r = pl.program_id(1)
    t = q_ref.shape[0]
    n_band = t + 2 * WINDOW
    n_keys = n_band + ktc_ref.shape[1]
    u = lax.broadcasted_iota(jnp.int32, (t, n_keys), 1)
    row = lax.broadcasted_iota(jnp.int32, (t, n_keys), 0)
    j = r * t - WINDOW + u
    in_band = (u >= row) & (u <= row + 2 * WINDOW) & (j >= 0) & (j < s_lat) & (r < n_lat_tiles)
    valid = in_band | (u >= n_band)
    lo, hi = _half_masks((t, LANES))
    for pair in range(C_HEADS // 2):
        kv = pair // (C_HEADS // C_KV // 2)
        ks = slice(kv * LANES, (kv + 1) * LANES)
        q = q_ref[:, pair * LANES:(pair + 1) * LANES]
        kt = jnp.concatenate([ktl_ref[ks, :], ktm_ref[ks, :], ktr_ref[ks, :], ktc_ref[ks, :]], axis=1)
        v = jnp.concatenate([vl_ref[:, ks], vm_ref[:, ks], vr_ref[:, ks], vc_ref[:, ks]], axis=0)
        halves = []
        for half, msk in enumerate((lo, hi)):
            sink = sink_ref[2 * pair + half]
            s = jnp.dot(jnp.where(msk, q, 0), kt, preferred_element_type=F32)
            s = jnp.where(valid, s, NEG_INF)
            m = jnp.maximum(jnp.max(s, axis=-1, keepdims=True), sink)
            e = jnp.exp(s - m)
            l = jnp.sum(e, axis=-1, keepdims=True) + jnp.exp(sink - m)
            halves.append(jnp.dot(e.astype(BF16), v, preferred_element_type=F32) * (1.0 / l))
        o_ref[:, pair * LANES:(pair + 1) * LANES] = jnp.where(lo, halves[0], halves[1]).astype(BF16)


def _attention(p, lp, subln_g, sink, n_lat_tiles, n_q_tiles, s_lat, lam_init):
    (_, aq, akt, av, bq, bkt, bv, cq, ckt, cv, _, _) = p
    b, nt, _ = aq.shape
    t = ROW_TILE
    grid = (b, n_q_tiles)
    row = lambda i, r: (i, r, 0)
    whole = lambda i, r: (i, 0, 0)
    sem = _params(("arbitrary", "arbitrary"))
    o_spec = pl.BlockSpec((None, t, 512), row)
    o_shape = jax.ShapeDtypeStruct((b, n_q_tiles * t, 512), BF16)

    oa = pl.pallas_call(
        functools.partial(_attn_a_kernel, n_lat_tiles=n_lat_tiles, s_lat=s_lat, lam_init=lam_init),
        grid=grid,
        in_specs=[pl.BlockSpec((None, t, 512), row), pl.BlockSpec((None, 512, nt), whole),
                  pl.BlockSpec((None, nt, 512), whole), _full((4, HEAD_DIM)), _full((1, 2 * HEAD_DIM))],
        out_specs=o_spec, out_shape=o_shape, compiler_params=sem, name="attn_diff",
    )(aq, akt, av, lp, subln_g)

    ob = pl.pallas_call(
        functools.partial(_attn_b_kernel, n_lat_tiles=n_lat_tiles, s_lat=s_lat),
        grid=grid,
        in_specs=[pl.BlockSpec((None, t, 512), row), pl.BlockSpec((None, 256, nt), whole),
                  pl.BlockSpec((None, nt, 256), whole)],
        out_specs=o_spec, out_shape=o_shape, compiler_params=sem, name="attn_gqa",
    )(bq, bkt, bv)

    wpt = t // WINDOW
    last = s_lat // WINDOW - 1
    left = lambda r: jnp.clip(r * wpt - 1, 0, last)
    mid = lambda r: jnp.minimum(r, n_lat_tiles - 1)
    right = lambda r: jnp.clip((r + 1) * wpt, 0, last)
    n_ctx = nt - s_lat
    ctx_blk = s_lat // n_ctx
    oc = pl.pallas_call(
        functools.partial(_attn_c_kernel, n_lat_tiles=n_lat_tiles, s_lat=s_lat),
        grid=grid,
        in_specs=[pl.BlockSpec(memory_space=pltpu.SMEM),
                  pl.BlockSpec((None, t, 512), row),
                  pl.BlockSpec((None, 256, WINDOW), lambda i, r: (i, 0, left(r))),
                  pl.BlockSpec((None, 256, t), lambda i, r: (i, 0, mid(r))),
                  pl.BlockSpec((None, 256, WINDOW), lambda i, r: (i, 0, right(r))),
                  pl.BlockSpec((None, 256, n_ctx), lambda i, r: (i, 0, ctx_blk)),
                  pl.BlockSpec((None, WINDOW, 256), lambda i, r: (i, left(r), 0)),
                  pl.BlockSpec((None, t, 256), lambda i, r: (i, mid(r), 0)),
                  pl.BlockSpec((None, WINDOW, 256), lambda i, r: (i, right(r), 0)),
                  pl.BlockSpec((None, n_ctx, 256), lambda i, r: (i, ctx_blk, 0))],
        out_specs=o_spec, out_shape=o_shape, compiler_params=sem, name="attn_window",
    )(sink, cq, ckt, ckt, ckt, ckt, cv, cv, cv, cv)
    return oa, ob, oc


def _fourier_kernel(cn_ref, sn_ref, cc_ref, sc_ref, zc_ref, zs_ref, o_ref, *, n_lat_tiles, s_lat):
    r = pl.program_id(1)
    nt = zc_ref.shape[0]

    @pl.when(r < n_lat_tiles)
    def _():
        y = (jnp.dot(cn_ref[...], zc_ref[0:s_lat, :], preferred_element_type=F32)
             - jnp.dot(sn_ref[...], zs_ref[0:s_lat, :], preferred_element_type=F32))
        o_ref[...] = y.astype(BF16)

    @pl.when(r >= n_lat_tiles)
    def _():
        y = (jnp.dot(cc_ref[...], zc_ref[s_lat:nt, :], preferred_element_type=F32)
             - jnp.dot(sc_ref[...], zs_ref[s_lat:nt, :], preferred_element_type=F32))
        o_ref[...] = y.astype(BF16)


def _fourier(fzc, fzs, consts, n_lat_tiles, n_q_tiles, s_lat):
    b, nt, w = fzc.shape
    t = ROW_TILE
    n_ctx = nt - s_lat
    lat_rows = lambda i, r: (jnp.minimum(r, n_lat_tiles - 1), 0)
    ctx_rows = lambda i, r: (jnp.maximum(r - n_lat_tiles, 0), 0)
    whole = lambda i, r: (i, 0, 0)
    return pl.pallas_call(
        functools.partial(_fourier_kernel, n_lat_tiles=n_lat_tiles, s_lat=s_lat),
        grid=(b, n_q_tiles),
        in_specs=[pl.BlockSpec((t, s_lat), lat_rows), pl.BlockSpec((t, s_lat), lat_rows),
                  pl.BlockSpec((t, n_ctx), ctx_rows), pl.BlockSpec((t, n_ctx), ctx_rows),
                  pl.BlockSpec((None, nt, w), whole), pl.BlockSpec((None, nt, w), whole)],
        out_specs=pl.BlockSpec((None, t, w), lambda i, r: (i, r, 0)),
        out_shape=jax.ShapeDtypeStruct((b, n_q_tiles * t, w), BF16),
        compiler_params=_params(("arbitrary", "arbitrary")),
        name="fourier",
    )(consts["cn"], consts["sn"], consts["cc"], consts["sc"], fzc, fzs)


def _merge_kernel(x_ref, mod_ref, gt_ref, oa_ref, ob_ref, oc_ref, od_ref, wbr_ref, wo_ref, o_ref, *, d):
    merged = None
    for i, br in enumerate((oa_ref, ob_ref, oc_ref, od_ref)):
        y = jnp.dot(br[...], wbr_ref[i], preferred_element_type=F32)
        y = gt_ref[:, i * d:(i + 1) * d].astype(F32) * y
        merged = y if merged is None else merged + y
    out = jnp.dot(merged.astype(BF16), wo_ref[...], preferred_element_type=F32)
    o_ref[...] = x_ref[...] + mod_ref[:, 2 * d:3 * d] * out


def _merge(xs, mod_l, gates, branches, w_br, w_o, n_lat_tiles, n_q_tiles):
    b, _, d = xs.shape
    t = ROW_TILE
    row = lambda i, r: (i, r, 0)
    br_spec = pl.BlockSpec((None, t, BRANCH_W), row)
    return pl.pallas_call(
        functools.partial(_merge_kernel, d=d),
        grid=(b, n_q_tiles),
        in_specs=[pl.BlockSpec((None, t, d), row),
                  pl.BlockSpec((None, 1, mod_l.shape[-1]), lambda i, r: (jnp.where(r < n_lat_tiles, i, b), 0, 0)),
                  pl.BlockSpec((None, t, N_BRANCH * d), row),
                  br_spec, br_spec, br_spec, br_spec,
                  _full(w_br.shape), _full(w_o.shape)],
        out_specs=pl.BlockSpec((None, t, d), row),
        out_shape=jax.ShapeDtypeStruct((b, n_q_tiles * t, d), F32),
        compiler_params=_params(("arbitrary", "arbitrary")),
        name="merge",
    )(xs, mod_l, gates, *branches, w_br, w_o)


def _ffn_kernel(x_ref, mod_ref, g_ref, wg_ref, wu_ref, wd_ref, o_ref, *, d):
    x = x_ref[...]
    h = _adaln(x, g_ref[...], mod_ref[:, 3 * d:4 * d], mod_ref[:, 4 * d:5 * d]).astype(BF16)
    a = jnp.dot(h, wg_ref[...], preferred_element_type=F32)
    u = jnp.dot(h, wu_ref[...], preferred_element_type=F32)
    act = (_silu(a) * u).astype(BF16)
    y = jnp.dot(act, wd_ref[...], preferred_element_type=F32)
    o_ref[...] = x + mod_ref[:, 5 * d:6 * d] * y


def _ffn(xs, mod_l, g, wg, wu, wd, n_lat_tiles):
    b, nt, d = xs.shape
    t = ROW_TILE
    row = lambda i, r: (i, r, 0)
    const = lambda shape: pl.BlockSpec(shape, lambda i, r: (0, 0), pipeline_mode=pl.Buffered(1))
    return pl.pallas_call(
        functools.partial(_ffn_kernel, d=d),
        grid=(b, nt // t),
        in_specs=[pl.BlockSpec((None, t, d), row),
                  pl.BlockSpec((None, 1, mod_l.shape[-1]), lambda i, r: (jnp.where(r < n_lat_tiles, i, b), 0, 0)),
                  _full((1, d)), const(wg.shape), const(wu.shape), const(wd.shape)],
        out_specs=pl.BlockSpec((None, t, d), row),
        out_shape=jax.ShapeDtypeStruct(xs.shape, F32),
        compiler_params=_params(("arbitrary", "arbitrary")),
        name="ffn_dense",
    )(xs, mod_l, g, wg, wu, wd)


def _store_token_major(ref, val):
    rows = val.shape[0]
    for c in range(SUBLANES):
        ref[pl.ds(c, rows, stride=SUBLANES), :] = val[:, c * LANES:(c + 1) * LANES]


def _load_token_major(ref, rows):
    return jnp.concatenate([ref[pl.ds(c, rows, stride=SUBLANES), :] for c in range(SUBLANES)], axis=1)


def _router_kernel(x_ref, mod_ref, g_ref, wr_ref, h_ref, sel_ref, *, d):
    h = _adaln(x_ref[...], g_ref[...], mod_ref[:, 3 * d:4 * d], mod_ref[:, 4 * d:5 * d])
    _store_token_major(h_ref, h)
    logits = jnp.dot(h, wr_ref[...], preferred_element_type=F32, precision=lax.Precision.HIGHEST)
    lane = lax.broadcasted_iota(jnp.int32, logits.shape, 1)
    logits = jnp.where(lane < N_EXPERTS, logits, -jnp.inf)
    m1 = jnp.max(logits, axis=-1, keepdims=True)
    i1 = jnp.min(jnp.where(logits == m1, lane, LANES), axis=-1, keepdims=True)
    rest = jnp.where(lane == i1, -jnp.inf, logits)
    m2 = jnp.max(rest, axis=-1, keepdims=True)
    i2 = jnp.min(jnp.where(rest == m2, lane, LANES), axis=-1, keepdims=True)
    e2 = jnp.exp(m2 - m1)
    w1 = 1.0 / (1.0 + e2)
    sel_ref[...] = (jnp.where(lane == 0, w1, 0.0) + jnp.where(lane == 1, e2 * w1, 0.0)
                    + jnp.where(lane == 2, i1.astype(F32), 0.0) + jnp.where(lane == 3, i2.astype(F32), 0.0))


def _router(xs, mod_l, g, w_r, n_lat_tiles, n_row_tiles):
    b, _, d = xs.shape
    t = ROW_TILE
    row = lambda i, r: (i, r, 0)
    return pl.pallas_call(
        functools.partial(_router_kernel, d=d),
        grid=(b, n_row_tiles),
        in_specs=[pl.BlockSpec((None, t, d), row),
                  pl.BlockSpec((None, 1, mod_l.shape[-1]), lambda i, r: (jnp.where(r < n_lat_tiles, i, b), 0, 0)),
                  _full((1, d)), _full(w_r.shape)],
        out_specs=[pl.BlockSpec((None, t * SUBLANES, LANES), row), pl.BlockSpec((None, t, LANES), row)],
        out_shape=[jax.ShapeDtypeStruct((b, n_row_tiles * t * SUBLANES, LANES), F32),
                   jax.ShapeDtypeStruct((b, n_row_tiles * t, LANES), F32)],
        compiler_params=_params(("arbitrary", "arbitrary")),
        name="moe_router",
    )(xs, mod_l, g, w_r)


def _dispatch(sel, tm):
    m = sel.shape[0]
    i32 = jnp.int32
    experts = jnp.arange(N_EXPERTS, dtype=i32)

    def lookup(table, idx):
        return jnp.sum(jnp.where(idx[:, None] == experts[None, :], table[None, :], 0), axis=1)

    idx = sel[:, 2:4].astype(i32)
    e_flat = jnp.concatenate([idx[:, 0], idx[:, 1]])
    assignment = jnp.arange(2 * m, dtype=i32)
    _, order = lax.sort_key_val(e_flat, assignment)
    _, rank = lax.sort_key_val(order, assignment)
    counts = jnp.sum((e_flat[:, None] == experts[None, :]).astype(i32), axis=0)
    padded = (counts + tm - 1) // tm * tm
    pad_end = jnp.cumsum(padded)
    pad_start = pad_end - padded
    start = jnp.cumsum(counts) - counts
    n_tiles = 2 * m // tm + N_EXPERTS
    n_used = pad_end[-1] // tm
    tile = jnp.minimum(jnp.arange(n_tiles, dtype=i32), n_used - 1)
    tile_e = jnp.sum((tile[:, None] * tm >= pad_end[None, :]).astype(i32), axis=1)
    slot = jnp.arange(n_tiles * tm, dtype=i32)
    slot_e = jnp.sum((jnp.minimum(slot, pad_end[-1] - 1)[:, None] >= pad_end[None, :]).astype(i32), axis=1)
    pos = slot - lookup(pad_start, slot_e)
    valid = (pos < lookup(counts, slot_e)) & (slot < pad_end[-1])
    src = jnp.where(valid, order[jnp.clip(lookup(start, slot_e) + pos, 0, 2 * m - 1)] % m, 0)
    slot_of = lookup(pad_start - start, e_flat) + rank
    return (tile_e, n_used.reshape(1).astype(i32), (src * SUBLANES).astype(i32), (slot_of * SUBLANES).astype(i32))


def _experts_kernel(te_ref, nu_ref, src_ref, h_hbm, wg_ref, wu_ref, wd_ref, o_ref, xg_ref, xb_ref, acc_ref, sems,
                    *, n_tiles, n_steps):
    t, j = pl.program_id(0), pl.program_id(1)
    tm = xb_ref.shape[0]
    rows_per_step = tm // n_steps
    n_used = nu_ref[0]
    used = t < n_used
    buf = t % 2

    def start_row(tile, dst_buf, r):
        src = pl.multiple_of(src_ref[tile * tm + r], SUBLANES)
        dst = pl.multiple_of(r * SUBLANES, SUBLANES)
        pltpu.make_async_copy(h_hbm.at[pl.ds(src, SUBLANES)], xg_ref.at[dst_buf, pl.ds(dst, SUBLANES)],
                              sems.at[dst_buf]).start()

    def wait_rows(b_):
        pltpu.make_async_copy(h_hbm.at[pl.ds(0, tm * SUBLANES)], xg_ref.at[b_], sems.at[b_]).wait()

    @pl.when((t == 0) & (j == 0))
    def _():
        def issue(r, carry):
            start_row(0, 0, r)
            return carry

        lax.fori_loop(0, tm, issue, 0, unroll=8)

    @pl.when((j == 0) & ((t == 0) | (t - 1 < n_used)))
    def _():
        wait_rows(buf)
        xb_ref[...] = _load_token_major(xg_ref.at[buf], tm).astype(BF16)

    @pl.when(j == 0)
    def _():
        acc_ref[...] = jnp.zeros_like(acc_ref)

    @pl.when(used)
    def _():
        nxt = jnp.minimum(t + 1, n_tiles - 1)
        for r in range(rows_per_step):
            start_row(nxt, 1 - buf, j * rows_per_step + r)
        h = xb_ref[...]
        a = jnp.dot(h, wg_ref[...].astype(BF16), preferred_element_type=F32)
        u = jnp.dot(h, wu_ref[...].astype(BF16), preferred_element_type=F32)
        act = (_silu(a) * u).astype(BF16)
        acc_ref[...] += jnp.dot(act, wd_ref[...].astype(BF16), preferred_element_type=F32)

    @pl.when(j == n_steps - 1)
    def _():
        _store_token_major(o_ref, acc_ref[...])

    @pl.when(used & (t == n_tiles - 1) & (j == n_steps - 1))
    def _():
        wait_rows(1 - buf)


def _moe_ff_tile(dff):
    return MOE_FF_TILE if dff % MOE_FF_TILE == 0 else dff


def _moe_tile_rows(m, dff):
    per_expert = 2.0 * m / N_EXPERTS
    unit = (dff // _moe_ff_tile(dff)) * 2 * SUBLANES
    rows = (per_expert + 3.0 * math.sqrt(per_expert)) / MOE_TILES_PER_EXPERT
    return int(math.ceil(rows / unit)) * unit


def _experts(h2, tables, wg, wu, wd, layer, tm):
    d, dff = wg.shape[-2:]
    assert d == SUBLANES * LANES
    tile_e, n_used, src, _ = tables
    n_tiles = tile_e.shape[0]
    tf = MOE_FF_TILE if dff % MOE_FF_TILE == 0 else dff
    nj = dff // tf
    assert tm % nj == 0
    col = lambda t, j, te, nu, src: (layer, te[t], 0, jnp.where(t < nu[0], j, nj - 1))
    rowb = lambda t, j, te, nu, src: (layer, te[t], jnp.where(t < nu[0], j, nj - 1), 0)
    return pl.pallas_call(
        functools.partial(_experts_kernel, n_tiles=n_tiles, n_steps=nj),
        grid_spec=pltpu.PrefetchScalarGridSpec(
            num_scalar_prefetch=3,
            grid=(n_tiles, nj),
            in_specs=[pl.BlockSpec(memory_space=pl.ANY),
                      pl.BlockSpec((None, None, d, tf), col),
                      pl.BlockSpec((None, None, d, tf), col),
                      pl.BlockSpec((None, None, tf, d), rowb)],
            out_specs=pl.BlockSpec((tm * SUBLANES, LANES), lambda t, j, te, nu, src: (t, 0)),
            scratch_shapes=[pltpu.VMEM((2, tm * SUBLANES, LANES), F32), pltpu.VMEM((tm, d), BF16),
                            pltpu.VMEM((tm, d), F32), pltpu.SemaphoreType.DMA((2,))]),
        out_shape=jax.ShapeDtypeStruct((n_tiles * tm * SUBLANES, LANES), F32),
        compiler_params=_params(("arbitrary", "arbitrary")),
        name="moe_experts",
    )(tile_e, n_used, src, h2, wg, wu, wd)


def _combine_kernel(slot_ref, x_ref, sel_ref, mod_ref, g_ref, y_hbm, o_ref, r1_ref, r2_ref, sems,
                    *, d, m, rows_per_sample, final):
    t = x_ref.shape[0]
    base = pl.program_id(0) * rows_per_sample + pl.program_id(1) * t

    def issue(k, carry):
        dst = pl.ds(pl.multiple_of(k * SUBLANES, SUBLANES), SUBLANES)
        s1 = pl.multiple_of(slot_ref[base + k], SUBLANES)
        s2 = pl.multiple_of(slot_ref[m + base + k], SUBLANES)
        pltpu.make_async_copy(y_hbm.at[pl.ds(s1, SUBLANES)], r1_ref.at[dst], sems.at[0]).start()
        pltpu.make_async_copy(y_hbm.at[pl.ds(s2, SUBLANES)], r2_ref.at[dst], sems.at[1]).start()
        return carry

    lax.fori_loop(0, t, issue, 0, unroll=8)
    pltpu.make_async_copy(y_hbm.at[pl.ds(0, t * SUBLANES)], r1_ref, sems.at[0]).wait()
    pltpu.make_async_copy(y_hbm.at[pl.ds(0, t * SUBLANES)], r2_ref, sems.at[1]).wait()
    y = sel_ref[:, 0:1] * _load_token_major(r1_ref, t) + sel_ref[:, 1:2] * _load_token_major(r2_ref, t)
    x = x_ref[...] + mod_ref[:, 5 * d:6 * d] * y
    if final:
        x = x * lax.rsqrt(jnp.mean(x * x, axis=-1, keepdims=True) + EPS) * g_ref[...]
    o_ref[...] = x


def _combine(xs, sel, mod_l, y_sorted, slot_of, final_g, n_lat_tiles, n_row_tiles, final):
    b, _, d = xs.shape
    t = ROW_TILE
    rows = n_row_tiles * t
    row = lambda i, r, s: (i, r, 0)
    return pl.pallas_call(
        functools.partial(_combine_kernel, d=d, m=b * rows, rows_per_sample=rows, final=final),
        grid_spec=pltpu.PrefetchScalarGridSpec(
            num_scalar_prefetch=1,
            grid=(b, n_row_tiles),
            in_specs=[pl.BlockSpec((None, t, d), row),
                      pl.BlockSpec((None, t, LANES), row),
                      pl.BlockSpec((None, 1, mod_l.shape[-1]),
                                   lambda i, r, s: (jnp.where(r < n_lat_tiles, i, b), 0, 0)),
                      pl.BlockSpec((1, d), lambda i, r, s: (0, 0)),
                      pl.BlockSpec(memory_space=pl.ANY)],
            out_specs=pl.BlockSpec((None, t, d), row),
            scratch_shapes=[pltpu.VMEM((t * SUBLANES, LANES), F32), pltpu.VMEM((t * SUBLANES, LANES), F32),
                            pltpu.SemaphoreType.DMA((2,))]),
        out_shape=jax.ShapeDtypeStruct((b, rows, d), F32),
        compiler_params=_params(("arbitrary", "arbitrary")),
        name="moe_combine",
    )(slot_of, xs, sel, mod_l, final_g, y_sorted)


def _rope_tables(s_lat, n_ctx):
    rows = s_lat // GRID_W
    pos_row = jnp.repeat(jnp.arange(rows, dtype=jnp.int32), GRID_W).astype(F32)
    pos_col = jnp.tile(jnp.arange(GRID_W, dtype=jnp.int32), rows).astype(F32)
    quarter = HEAD_DIM // 4
    inv_freq = ROPE_THETA ** (-jnp.arange(quarter, dtype=F32) / quarter)
    ang_r = pos_row[:, None] * inv_freq
    ang_c = pos_col[:, None] * inv_freq
    ang = jnp.concatenate([ang_r, ang_r, ang_c, ang_c], axis=-1)
    cos, sin = jnp.cos(ang), jnp.sin(ang)
    first = (np.arange(HEAD_DIM) % (2 * quarter)) < quarter
    sin_lo = jnp.where(first, -sin, 0.0)
    sin_hi = jnp.where(first, 0.0, sin)

    def widen(tab, ctx_fill):
        tab = jnp.concatenate([tab, jnp.full((n_ctx, HEAD_DIM), ctx_fill, F32)], axis=0)
        return jnp.tile(tab, (1, LANES // HEAD_DIM))

    return widen(cos, 1.0), widen(sin_lo, 0.0), widen(sin_hi, 0.0)


def _dft_tables(n):
    k = np.arange(n, dtype=np.int64)
    ang = 2.0 * np.pi * ((k[:, None] * k[None, :]) % n).astype(np.float64) / n
    return np.cos(ang), np.sin(ang)


def _constants(s_lat, n_ctx):
    c64, s64 = _dft_tables(F_GROUP_DIM)
    eye = np.eye(F_GROUPS)
    cn, sn = _dft_tables(s_lat)
    cc, sc = _dft_tables(n_ctx)
    bf = lambda a: jnp.asarray(a, dtype=F32).astype(BF16)
    return dict(
        gmat=bf(np.kron(np.eye(512 // HEAD_DIM), np.ones((HEAD_DIM, HEAD_DIM)))),
        c64=bf(np.kron(eye, c64) / math.sqrt(F_GROUP_DIM)),
        s64=bf(np.kron(eye, s64) / math.sqrt(F_GROUP_DIM)),
        cn=bf(cn / math.sqrt(s_lat)), sn=bf(sn / math.sqrt(s_lat)),
        cc=bf(cc / math.sqrt(n_ctx)), sc=bf(sc / math.sqrt(n_ctx)),
    )


def _dup_heads(w, off, n_heads):
    cols = []
    for hd in range(n_heads):
        blk = w[:, off + hd * HEAD_DIM: off + (hd + 1) * HEAD_DIM]
        cols += [blk, blk]
    return cols


def _relayout_w_in(w):
    cols = [w[:, _OFF["gt"]:_OFF["end"]], w[:, _OFF["aq"]:_OFF["bq"]], w[:, _OFF["bq"]:_OFF["bk"]],
            w[:, _OFF["cq"]:_OFF["ck"]], w[:, _OFF["fz"]:_OFF["gt"]]]
    cols += _dup_heads(w, _OFF["bk"], B_KV) + _dup_heads(w, _OFF["bv"], B_KV)
    cols += _dup_heads(w, _OFF["ck"], C_KV) + _dup_heads(w, _OFF["cv"], C_KV)
    return jnp.concatenate(cols, axis=1).astype(BF16)


def kernel(x, c, ctx, c_ctx, w_mod, b_mod, norm1_g, norm2_g, w_in, diff_lambda, diff_subln_g, qk_norm_g,
           sink_logits, w_branch, w_out, ffn_w_gate, ffn_w_up, ffn_w_down, router_w, moe_w_gate, moe_w_up,
           moe_w_down, final_g):
    b, s_lat, d = x.shape
    n_ctx = ctx.shape[1]
    depth = w_mod.shape[0]
    nt = s_lat + n_ctx
    t = ROW_TILE
    assert s_lat % t == 0 and n_ctx % t == 0 and s_lat % n_ctx == 0 and b < MOD_ROWS
    assert w_in.shape[-1] == _OFF["end"] and d * N_BRANCH == _OFF["end"] - _OFF["gt"]
    n_lat_tiles = s_lat // t

    xs = jnp.concatenate([x, ctx], axis=1)
    cvec = jnp.concatenate([c, c_ctx[None], jnp.zeros((MOD_ROWS - b - 1, d), F32)], axis=0)
    mod = _modulation(cvec, w_mod, b_mod).reshape(depth, MOD_ROWS, 1, 6 * d)
    tabs = _rope_tables(s_lat, n_ctx)
    consts = _constants(s_lat, n_ctx)

    out = None
    for l in range(depth):
        last = l == depth - 1
        mod_l = mod[l]
        p = _inproj(xs, mod_l, norm1_g[l][None], _relayout_w_in(w_in[l]), tabs,
                    jnp.tile(qk_norm_g[l, 0], 512 // HEAD_DIM)[None], jnp.tile(qk_norm_g[l, 1], 256 // HEAD_DIM)[None],
                    consts, n_lat_tiles)
        lam_init = 0.8 - 0.6 * math.exp(-0.3 * l)
        n_q_tiles = n_lat_tiles if last else nt // t
        oa, ob, oc = _attention(p, diff_lambda[l], diff_subln_g[l][None], sink_logits[l],
                                n_lat_tiles, n_q_tiles, s_lat, lam_init)
        od = _fourier(p[10], p[11], consts, n_lat_tiles, n_q_tiles, s_lat)
        xs = _merge(xs, mod_l, p[0], (oa, ob, oc, od), w_branch[l].astype(BF16), w_out[l].astype(BF16),
                    n_lat_tiles, n_q_tiles)

        i = l // 2
        if l % 2 == 0:
            assert not last
            xs = _ffn(xs, mod_l, norm2_g[l][None], ffn_w_gate[i].astype(BF16), ffn_w_up[i].astype(BF16),
                      ffn_w_down[i].astype(BF16), n_lat_tiles)
        else:
            w_r = jnp.pad(router_w[i], ((0, 0), (0, LANES - N_EXPERTS)))
            n_row_tiles = n_q_tiles
            m = b * n_row_tiles * t
            h2, sel = _router(xs, mod_l, norm2_g[l][None], w_r, n_lat_tiles, n_row_tiles)
            tm = _moe_tile_rows(m, moe_w_gate.shape[-1])
            tables = _dispatch(sel.reshape(m, LANES), tm)
            y_sorted = _experts(h2.reshape(m * SUBLANES, LANES), tables, moe_w_gate, moe_w_up, moe_w_down, i, tm)
            res = _combine(xs, sel, mod_l, y_sorted, tables[3], final_g[None], n_lat_tiles, n_row_tiles, last)
            if last:
                out = res
            else:
                xs = res
    if out is None:
        raise NotImplementedError("final normalisation is fused into the last (mixture-of-experts) layer")
    return out
```

```python
import functools
import math

import numpy as np
import jax
import jax.numpy as jnp
from jax import lax
from jax.experimental import pallas as pl
from jax.experimental.pallas import tpu as pltpu

F32 = jnp.float32
BF16 = jnp.bfloat16

GRID_W = 64
HEAD_DIM = 64
ROPE_THETA = 10000.0
WINDOW = 128
A_HEADS = 4
B_HEADS = 8
B_KV = 2
C_HEADS = 8
C_KV = 2
F_GROUPS = 8
F_GROUP_DIM = 64
BRANCH_W = 512
N_BRANCH = 4
N_EXPERTS = 8
EPS = 1e-6
SUBLN_EPS = 1e-5
NEG_INF = -1e30
LOG2E = math.log2(math.e)

LANES = 128
SUBLANES = 8
ROW_TILE = 256
MOE_TILES_PER_EXPERT = 4
MOE_FF_TILE = 512
MOD_ROWS = 16
VMEM_LIMIT = 56 * 1024 * 1024

_OFF = dict(aq=0, ak=512, av=1024, bq=1536, bk=2048, bv=2176, cq=2304, ck=2816, cv=2944,
            fz=3072, gt=3584, end=7680)
_P = dict(gt=0, aq=4096, ak=4608, av=5120, bq=5632, cq=6144, fz=6656,
          bk=7168, bv=7424, ck=7680, cv=7936, end=8192)


def _params(sem):
    return pltpu.CompilerParams(dimension_semantics=sem, vmem_limit_bytes=VMEM_LIMIT)


def _full(shape):
    n = len(shape)
    return pl.BlockSpec(shape, lambda *_: (0,) * n)


def _silu(x):
    return x / (1.0 + jnp.exp(-x))


def _sigmoid(x):
    return 1.0 / (1.0 + jnp.exp(-x))


def _adaln(x, g, shift, scale):
    y = x * lax.rsqrt(jnp.mean(x * x, axis=-1, keepdims=True) + EPS) * g
    return y * (1.0 + scale) + shift


def _mod_kernel(c_ref, w_ref, b_ref, o_ref):
    s = _silu(c_ref[...]).astype(BF16)
    o_ref[...] = jnp.dot(s, w_ref[...].astype(BF16), preferred_element_type=F32) + b_ref[...]


def _modulation(cvec, w_mod, b_mod):
    depth, d, n = w_mod.shape
    tn = 1536 if n % 1536 == 0 else n
    return pl.pallas_call(
        _mod_kernel,
        grid=(depth, n // tn),
        in_specs=[_full((MOD_ROWS, d)),
                  pl.BlockSpec((None, d, tn), lambda l, j: (l, 0, j)),
                  pl.BlockSpec((None, 1, tn), lambda l, j: (l, 0, j))],
        out_specs=pl.BlockSpec((None, MOD_ROWS, tn), lambda l, j: (l, 0, j)),
        out_shape=jax.ShapeDtypeStruct((depth, MOD_ROWS, n), F32),
        compiler_params=_params(("arbitrary", "arbitrary")),
        name="modulation",
    )(cvec, w_mod, b_mod.reshape(depth, 1, n))


def _rope(p, cos, sin_lo, sin_hi):
    outs = []
    for c in range(p.shape[1] // LANES):
        xc = p[:, c * LANES:(c + 1) * LANES]
        outs.append(xc * cos + pltpu.roll(xc, LANES - 16, 1) * sin_lo + pltpu.roll(xc, 16, 1) * sin_hi)
    return outs[0] if len(outs) == 1 else jnp.concatenate(outs, axis=1)


def _group_rmsnorm(y, gmat, g):
    ss = jnp.dot((y * y).astype(BF16), gmat, preferred_element_type=F32)
    return y * lax.rsqrt(ss * (1.0 / HEAD_DIM) + EPS) * g


def _inproj_kernel(x_ref, mod_ref, g_ref, w_ref, cos_ref, slo_ref, shi_ref, qg_ref, kg_ref, gmat_ref,
                   c64_ref, s64_ref,
                   gt_ref, aq_ref, akt_ref, av_ref, bq_ref, bkt_ref, bv_ref, cq_ref, ckt_ref, cv_ref,
                   fzc_ref, fzs_ref, *, d):
    h = _adaln(x_ref[...], g_ref[...], mod_ref[:, 0:d], mod_ref[:, d:2 * d]).astype(BF16)
    cos, slo, shi = cos_ref[...], slo_ref[...], shi_ref[...]
    scale = HEAD_DIM ** -0.5 * LOG2E

    def proj(name, width):
        return jnp.dot(h, w_ref[:, _P[name]:_P[name] + width], preferred_element_type=F32)

    for c in range(N_BRANCH * d // 512):
        p = jnp.dot(h, w_ref[:, c * 512:(c + 1) * 512], preferred_element_type=F32)
        gt_ref[:, c * 512:(c + 1) * 512] = _sigmoid(p).astype(BF16)

    aq_ref[...] = (_rope(proj("aq", 512), cos, slo, shi) * scale).astype(BF16)
    akt_ref[...] = _rope(proj("ak", 512), cos, slo, shi).T.astype(BF16)
    av_ref[...] = proj("av", 512).astype(BF16)

    bq = _group_rmsnorm(proj("bq", 512), gmat_ref[...], qg_ref[...])
    bq_ref[...] = (_rope(bq, cos, slo, shi) * scale).astype(BF16)
    bk = _group_rmsnorm(proj("bk", 256), gmat_ref[0:256, 0:256], kg_ref[...])
    bkt_ref[...] = _rope(bk, cos, slo, shi).T.astype(BF16)
    bv_ref[...] = proj("bv", 256).astype(BF16)

    cq_ref[...] = (_rope(proj("cq", 512), cos, slo, shi) * scale).astype(BF16)
    ckt_ref[...] = _rope(proj("ck", 256), cos, slo, shi).T.astype(BF16)
    cv_ref[...] = proj("cv", 256).astype(BF16)

    fz = proj("fz", 512).astype(BF16)
    fzc_ref[...] = jnp.dot(fz, c64_ref[...], preferred_element_type=F32).astype(BF16)
    fzs_ref[...] = jnp.dot(fz, s64_ref[...], preferred_element_type=F32).astype(BF16)


def _inproj(xs, mod_l, g, w_p, tabs, qg, kg, consts, n_lat_tiles):
    b, nt, d = xs.shape
    t = ROW_TILE
    n_mod = mod_l.shape[-1]
    row = lambda i, r: (i, r, 0)
    tr = lambda i, r: (i, 0, r)
    tab = pl.BlockSpec((t, LANES), lambda i, r: (r, 0))

    def out(width):
        return pl.BlockSpec((None, t, width), row), jax.ShapeDtypeStruct((b, nt, width), BF16)

    def out_t(width):
        return pl.BlockSpec((None, width, t), tr), jax.ShapeDtypeStruct((b, width, nt), BF16)

    outs = [out(N_BRANCH * d), out(512), out_t(512), out(512), out(512), out_t(256), out(256),
            out(512), out_t(256), out(256), out(512), out(512)]
    return pl.pallas_call(
        functools.partial(_inproj_kernel, d=d),
        grid=(b, nt // t),
        in_specs=[pl.BlockSpec((None, t, d), row),
                  pl.BlockSpec((None, 1, n_mod), lambda i, r: (jnp.where(r < n_lat_tiles, i, b), 0, 0)),
                  _full((1, d)),
                  pl.BlockSpec(w_p.shape, lambda i, r: (0, 0), pipeline_mode=pl.Buffered(1)),
                  tab, tab, tab,
                  _full((1, 512)), _full((1, 256)), _full((512, 512)), _full((512, 512)), _full((512, 512))],
        out_specs=[o[0] for o in outs],
        out_shape=[o[1] for o in outs],
        compiler_params=_params(("arbitrary", "arbitrary")),
        name="inproj",
    )(xs, mod_l, g, w_p, tabs[0], tabs[1], tabs[2], qg, kg, consts["gmat"], consts["c64"], consts["s64"])


def _half_masks(shape):
    lane = lax.broadcasted_iota(jnp.int32, shape, 1)
    return lane < HEAD_DIM, lane >= HEAD_DIM


def _exp2_bf16(s):
    return jnp.exp2(s - jnp.max(s, axis=-1, keepdims=True)).astype(BF16)


def _pipelined_scores(score_fns):
    nxt = score_fns[0]()
    for i in range(len(score_fns)):
        cur = nxt
        if i + 1 < len(score_fns):
            nxt = score_fns[i + 1]()
        yield cur


def _attn_a_kernel(q_ref, kt_ref, v_ref, lp_ref, g_ref, o_ref, *, n_lat_tiles, s_lat, lam_init):
    r = pl.program_id(1)
    lp = lp_ref[...]
    lam = (jnp.exp(jnp.sum(lp[0:1] * lp[1:2], axis=-1, keepdims=True))
           - jnp.exp(jnp.sum(lp[2:3] * lp[3:4], axis=-1, keepdims=True)) + lam_init)
    g = g_ref[...] * (1.0 - lam_init)
    nt = kt_ref.shape[1]

    def body(k0):
        lo, hi = _half_masks((q_ref.shape[0], LANES))
        ones = jnp.ones((nt - k0, LANES), BF16)

        def score_fn(hd, msk):
            cs = slice(hd * LANES, (hd + 1) * LANES)
            return lambda: jnp.dot(jnp.where(msk, q_ref[:, cs], 0), kt_ref[cs, k0:nt], preferred_element_type=F32)

        fns = [score_fn(hd, msk) for hd in range(A_HEADS) for msk in (lo, hi)]
        outs = []
        for i, sc in enumerate(_pipelined_scores(fns)):
            hd = i // 2
            cs = slice(hd * LANES, (hd + 1) * LANES)
            vo = jnp.concatenate([v_ref[k0:nt, cs], ones], axis=1)
            full = jnp.dot(_exp2_bf16(sc), vo, preferred_element_type=F32)
            outs.append(full[:, 0:LANES] * (1.0 / full[:, LANES:LANES + 1]))
            if len(outs) == 2:
                o = outs[0] - lam * outs[1]
                outs = []
                o = o * lax.rsqrt(jnp.mean(o * o, axis=-1, keepdims=True) + SUBLN_EPS) * g
                o_ref[:, cs] = o.astype(BF16)

    pl.when(r < n_lat_tiles)(lambda: body(0))
    pl.when(r >= n_lat_tiles)(lambda: body(s_lat))


def _attn_b_kernel(q_ref, kt_ref, v_ref, o_ref, *, n_lat_tiles, s_lat):
    r = pl.program_id(1)
    nt = kt_ref.shape[1]

    def body(k0):
        lo, hi = _half_masks((q_ref.shape[0], LANES))
        ones = jnp.ones((nt - k0, LANES), BF16)

        def score_fn(pair, msk):
            kv = pair // (B_HEADS // B_KV // 2)
            q = q_ref[:, pair * LANES:(pair + 1) * LANES]
            return lambda: jnp.dot(jnp.where(msk, q, 0), kt_ref[kv * LANES:(kv + 1) * LANES, k0:nt],
                                   preferred_element_type=F32)

        fns = [score_fn(pair, msk) for pair in range(B_HEADS // 2) for msk in (lo, hi)]
        halves = []
        for i, sc in enumerate(_pipelined_scores(fns)):
            pair = i // 2
            kv = pair // (B_HEADS // B_KV // 2)
            vo = jnp.concatenate([v_ref[k0:nt, kv * LANES:(kv + 1) * LANES], ones], axis=1)
            full = jnp.dot(_exp2_bf16(sc), vo, preferred_element_type=F32)
            halves.append(full[:, 0:LANES] * (1.0 / full[:, LANES:LANES + 1]))
            if len(halves) == 2:
                o_ref[:, pair * LANES:(pair + 1) * LANES] = jnp.where(lo, halves[0], halves[1]).astype(BF16)
                halves = []

    pl.when(r < n_lat_tiles)(lambda: body(0))
    pl.when(r >= n_lat_tiles)(lambda: body(s_lat))


def _attn_c_kernel(sink_ref, q_ref, ktl_ref, ktm_ref, ktr_ref, ktc_ref, vl_ref, vm_ref, vr_ref, vc_ref, o_ref,
                   *, n_lat_tiles, s_lat):
    r = pl.program_id(1)
    t = q_ref.shape[0]
    n_band = t + 2 * WINDOW
    n_keys = n_band + ktc_ref.shape[1]
    u = lax.broadcasted_iota(jnp.int32, (t, n_keys), 1)
    row = lax.broadcasted_iota(jnp.int32, (t, n_keys), 0)
    j = r * t - WINDOW + u
    in_band = (u >= row) & (u <= row + 2 * WINDOW) & (j >= 0) & (j < s_lat) & (r < n_lat_tiles)
    valid = in_band | (u >= n_band)
    lo, hi = _half_masks((t, LANES))
    ones = jnp.ones((n_keys, LANES), BF16)
    kts, vos = [], []
    for kv in range(C_KV):
        ks = slice(kv * LANES, (kv + 1) * LANES)
        kts.append(jnp.concatenate([ktl_ref[ks, :], ktm_ref[ks, :], ktr_ref[ks, :], ktc_ref[ks, :]], axis=1))
        v = jnp.concatenate([vl_ref[:, ks], vm_ref[:, ks], vr_ref[:, ks], vc_ref[:, ks]], axis=0)
        vos.append(jnp.concatenate([v, ones], axis=1))

    def score_fn(pair, msk):
        kv = pair // (C_HEADS // C_KV // 2)
        q = q_ref[:, pair * LANES:(pair + 1) * LANES]
        return lambda: jnp.dot(jnp.where(msk, q, 0), kts[kv], preferred_element_type=F32)

    fns = [score_fn(pair, msk) for pair in range(C_HEADS // 2) for msk in (lo, hi)]
    halves = []
    for i, sc in enumerate(_pipelined_scores(fns)):
        pair = i // 2
        kv = pair // (C_HEADS // C_KV // 2)
        sink = sink_ref[i] * LOG2E
        sc = jnp.where(valid, sc, NEG_INF)
        m = jnp.maximum(jnp.max(sc, axis=-1, keepdims=True), sink)
        full = jnp.dot(jnp.exp2(sc - m).astype(BF16), vos[kv], preferred_element_type=F32)
        l = full[:, LANES:LANES + 1] + jnp.exp2(sink - m)
        halves.append(full[:, 0:LANES] * (1.0 / l))
        if len(halves) == 2:
            o_ref[:, pair * LANES:(pair + 1) * LANES] = jnp.where(lo, halves[0], halves[1]).astype(BF16)
            halves = []


def _attention(p, lp, subln_g, sink, n_lat_tiles, n_q_tiles, s_lat, lam_init):
    (_, aq, akt, av, bq, bkt, bv, cq, ckt, cv, _, _) = p
    b, nt, _ = aq.shape
    t = ROW_TILE
    grid = (b, n_q_tiles)
    row = lambda i, r: (i, r, 0)
    whole = lambda i, r: (i, 0, 0)
    sem = _params(("arbitrary", "arbitrary"))
    o_spec = pl.BlockSpec((None, t, 512), row)
    o_shape = jax.ShapeDtypeStruct((b, n_q_tiles * t, 512), BF16)

    oa = pl.pallas_call(
        functools.partial(_attn_a_kernel, n_lat_tiles=n_lat_tiles, s_lat=s_lat, lam_init=lam_init),
        grid=grid,
        in_specs=[pl.BlockSpec((None, t, 512), row), pl.BlockSpec((None, 512, nt), whole),
                  pl.BlockSpec((None, nt, 512), whole), _full((4, HEAD_DIM)), _full((1, 2 * HEAD_DIM))],
        out_specs=o_spec, out_shape=o_shape, compiler_params=sem, name="attn_diff",
    )(aq, akt, av, lp, subln_g)

    ob = pl.pallas_call(
        functools.partial(_attn_b_kernel, n_lat_tiles=n_lat_tiles, s_lat=s_lat),
        grid=grid,
        in_specs=[pl.BlockSpec((None, t, 512), row), pl.BlockSpec((None, 256, nt), whole),
                  pl.BlockSpec((None, nt, 256), whole)],
        out_specs=o_spec, out_shape=o_shape, compiler_params=sem, name="attn_gqa",
    )(bq, bkt, bv)

    wpt = t // WINDOW
    last = s_lat // WINDOW - 1
    left = lambda r: jnp.clip(r * wpt - 1, 0, last)
    mid = lambda r: jnp.minimum(r, n_lat_tiles - 1)
    right = lambda r: jnp.clip((r + 1) * wpt, 0, last)
    n_ctx = nt - s_lat
    ctx_blk = s_lat // n_ctx
    oc = pl.pallas_call(
        functools.partial(_attn_c_kernel, n_lat_tiles=n_lat_tiles, s_lat=s_lat),
        grid=grid,
        in_specs=[pl.BlockSpec(memory_space=pltpu.SMEM),
                  pl.BlockSpec((None, t, 512), row),
                  pl.BlockSpec((None, 256, WINDOW), lambda i, r: (i, 0, left(r))),
                  pl.BlockSpec((None, 256, t), lambda i, r: (i, 0, mid(r))),
                  pl.BlockSpec((None, 256, WINDOW), lambda i, r: (i, 0, right(r))),
                  pl.BlockSpec((None, 256, n_ctx), lambda i, r: (i, 0, ctx_blk)),
                  pl.BlockSpec((None, WINDOW, 256), lambda i, r: (i, left(r), 0)),
                  pl.BlockSpec((None, t, 256), lambda i, r: (i, mid(r), 0)),
                  pl.BlockSpec((None, WINDOW, 256), lambda i, r: (i, right(r), 0)),
                  pl.BlockSpec((None, n_ctx, 256), lambda i, r: (i, ctx_blk, 0))],
        out_specs=o_spec, out_shape=o_shape, compiler_params=sem, name="attn_window",
    )(sink, cq, ckt, ckt, ckt, ckt, cv, cv, cv, cv)
    return oa, ob, oc


def _fourier_kernel(cn_ref, sn_ref, cc_ref, sc_ref, zc_ref, zs_ref, o_ref, *, n_lat_tiles, s_lat):
    r = pl.program_id(1)
    nt = zc_ref.shape[0]

    @pl.when(r < n_lat_tiles)
    def _():
        y = (jnp.dot(cn_ref[...], zc_ref[0:s_lat, :], preferred_element_type=F32)
             - jnp.dot(sn_ref[...], zs_ref[0:s_lat, :], preferred_element_type=F32))
        o_ref[...] = y.astype(BF16)

    @pl.when(r >= n_lat_tiles)
    def _():
        y = (jnp.dot(cc_ref[...], zc_ref[s_lat:nt, :], preferred_element_type=F32)
             - jnp.dot(sc_ref[...], zs_ref[s_lat:nt, :], preferred_element_type=F32))
        o_ref[...] = y.astype(BF16)


def _fourier(fzc, fzs, consts, n_lat_tiles, n_q_tiles, s_lat):
    b, nt, w = fzc.shape
    t = ROW_TILE
    n_ctx = nt - s_lat
    lat_rows = lambda i, r: (jnp.minimum(r, n_lat_tiles - 1), 0)
    ctx_rows = lambda i, r: (jnp.maximum(r - n_lat_tiles, 0), 0)
    whole = lambda i, r: (i, 0, 0)
    return pl.pallas_call(
        functools.partial(_fourier_kernel, n_lat_tiles=n_lat_tiles, s_lat=s_lat),
        grid=(b, n_q_tiles),
        in_specs=[pl.BlockSpec((t, s_lat), lat_rows), pl.BlockSpec((t, s_lat), lat_rows),
                  pl.BlockSpec((t, n_ctx), ctx_rows), pl.BlockSpec((t, n_ctx), ctx_rows),
                  pl.BlockSpec((None, nt, w), whole), pl.BlockSpec((None, nt, w), whole)],
        out_specs=pl.BlockSpec((None, t, w), lambda i, r: (i, r, 0)),
        out_shape=jax.ShapeDtypeStruct((b, n_q_tiles * t, w), BF16),
        compiler_params=_params(("arbitrary", "arbitrary")),
        name="fourier",
    )(consts["cn"], consts["sn"], consts["cc"], consts["sc"], fzc, fzs)


def _merge_kernel(x_ref, mod_ref, gt_ref, oa_ref, ob_ref, oc_ref, od_ref, wbr_ref, wo_ref, o_ref, *, d):
    merged = None
    for i, br in enumerate((oa_ref, ob_ref, oc_ref, od_ref)):
        y = jnp.dot(br[...], wbr_ref[i], preferred_element_type=F32)
        y = gt_ref[:, i * d:(i + 1) * d].astype(F32) * y
        merged = y if merged is None else merged + y
    out = jnp.dot(merged.astype(BF16), wo_ref[...], preferred_element_type=F32)
    o_ref[...] = x_ref[...] + mod_ref[:, 2 * d:3 * d] * out


def _merge(xs, mod_l, gates, branches, w_br, w_o, n_lat_tiles, n_q_tiles):
    b, _, d = xs.shape
    t = ROW_TILE
    row = lambda i, r: (i, r, 0)
    br_spec = pl.BlockSpec((None, t, BRANCH_W), row)
    return pl.pallas_call(
        functools.partial(_merge_kernel, d=d),
        grid=(b, n_q_tiles),
        in_specs=[pl.BlockSpec((None, t, d), row),
                  pl.BlockSpec((None, 1, mod_l.shape[-1]), lambda i, r: (jnp.where(r < n_lat_tiles, i, b), 0, 0)),
                  pl.BlockSpec((None, t, N_BRANCH * d), row),
                  br_spec, br_spec, br_spec, br_spec,
                  _full(w_br.shape), _full(w_o.shape)],
        out_specs=pl.BlockSpec((None, t, d), row),
        out_shape=jax.ShapeDtypeStruct((b, n_q_tiles * t, d), F32),
        compiler_params=_params(("arbitrary", "arbitrary")),
        name="merge",
    )(xs, mod_l, gates, *branches, w_br, w_o)


def _ffn_kernel(x_ref, mod_ref, g_ref, wg_ref, wu_ref, wd_ref, o_ref, *, d):
    x = x_ref[...]
    h = _adaln(x, g_ref[...], mod_ref[:, 3 * d:4 * d], mod_ref[:, 4 * d:5 * d]).astype(BF16)
    a = jnp.dot(h, wg_ref[...], preferred_element_type=F32)
    u = jnp.dot(h, wu_ref[...], preferred_element_type=F32)
    act = (_silu(a) * u).astype(BF16)
    y = jnp.dot(act, wd_ref[...], preferred_element_type=F32)
    o_ref[...] = x + mod_ref[:, 5 * d:6 * d] * y


def _ffn(xs, mod_l, g, wg, wu, wd, n_lat_tiles):
    b, nt, d = xs.shape
    t = ROW_TILE
    row = lambda i, r: (i, r, 0)
    const = lambda shape: pl.BlockSpec(shape, lambda i, r: (0, 0), pipeline_mode=pl.Buffered(1))
    return pl.pallas_call(
        functools.partial(_ffn_kernel, d=d),
        grid=(b, nt // t),
        in_specs=[pl.BlockSpec((None, t, d), row),
                  pl.BlockSpec((None, 1, mod_l.shape[-1]), lambda i, r: (jnp.where(r < n_lat_tiles, i, b), 0, 0)),
                  _full((1, d)), const(wg.shape), const(wu.shape), const(wd.shape)],
        out_specs=pl.BlockSpec((None, t, d), row),
        out_shape=jax.ShapeDtypeStruct(xs.shape, F32),
        compiler_params=_params(("arbitrary", "arbitrary")),
        name="ffn_dense",
    )(xs, mod_l, g, wg, wu, wd)


def _store_token_major(ref, val):
    rows = val.shape[0]
    for c in range(SUBLANES):
        ref[pl.ds(c, rows, stride=SUBLANES), :] = val[:, c * LANES:(c + 1) * LANES]


def _load_token_major(ref, rows):
    return jnp.concatenate([ref[pl.ds(c, rows, stride=SUBLANES), :] for c in range(SUBLANES)], axis=1)


def _router_kernel(x_ref, mod_ref, g_ref, wr_ref, h_ref, sel_ref, *, d):
    h = _adaln(x_ref[...], g_ref[...], mod_ref[:, 3 * d:4 * d], mod_ref[:, 4 * d:5 * d])
    _store_token_major(h_ref, h)
    logits = jnp.dot(h, wr_ref[...], preferred_element_type=F32, precision=lax.Precision.HIGHEST)
    lane = lax.broadcasted_iota(jnp.int32, logits.shape, 1)
    logits = jnp.where(lane < N_EXPERTS, logits, -jnp.inf)
    m1 = jnp.max(logits, axis=-1, keepdims=True)
    i1 = jnp.min(jnp.where(logits == m1, lane, LANES), axis=-1, keepdims=True)
    rest = jnp.where(lane == i1, -jnp.inf, logits)
    m2 = jnp.max(rest, axis=-1, keepdims=True)
    i2 = jnp.min(jnp.where(rest == m2, lane, LANES), axis=-1, keepdims=True)
    e2 = jnp.exp(m2 - m1)
    w1 = 1.0 / (1.0 + e2)
    sel_ref[...] = (jnp.where(lane == 0, w1, 0.0) + jnp.where(lane == 1, e2 * w1, 0.0)
                    + jnp.where(lane == 2, i1.astype(F32), 0.0) + jnp.where(lane == 3, i2.astype(F32), 0.0))


def _router(xs, mod_l, g, w_r, n_lat_tiles, n_row_tiles):
    b, _, d = xs.shape
    t = ROW_TILE
    row = lambda i, r: (i, r, 0)
    return pl.pallas_call(
        functools.partial(_router_kernel, d=d),
        grid=(b, n_row_tiles),
        in_specs=[pl.BlockSpec((None, t, d), row),
                  pl.BlockSpec((None, 1, mod_l.shape[-1]), lambda i, r: (jnp.where(r < n_lat_tiles, i, b), 0, 0)),
                  _full((1, d)), _full(w_r.shape)],
        out_specs=[pl.BlockSpec((None, t * SUBLANES, LANES), row), pl.BlockSpec((None, t, LANES), row)],
        out_shape=[jax.ShapeDtypeStruct((b, n_row_tiles * t * SUBLANES, LANES), F32),
                   jax.ShapeDtypeStruct((b, n_row_tiles * t, LANES), F32)],
        compiler_params=_params(("arbitrary", "arbitrary")),
        name="moe_router",
    )(xs, mod_l, g, w_r)


def _dispatch(sel, tm):
    m = sel.shape[0]
    i32 = jnp.int32
    experts = jnp.arange(N_EXPERTS, dtype=i32)

    def lookup(table, idx):
        return jnp.sum(jnp.where(idx[:, None] == experts[None, :], table[None, :], 0), axis=1)

    idx = sel[:, 2:4].astype(i32)
    e_flat = jnp.concatenate([idx[:, 0], idx[:, 1]])
    assignment = jnp.arange(2 * m, dtype=i32)
    _, order = lax.sort_key_val(e_flat, assignment)
    _, rank = lax.sort_key_val(order, assignment)
    counts = jnp.sum((e_flat[:, None] == experts[None, :]).astype(i32), axis=0)
    padded = (counts + tm - 1) // tm * tm
    pad_end = jnp.cumsum(padded)
    pad_start = pad_end - padded
    start = jnp.cumsum(counts) - counts
    n_tiles = 2 * m // tm + N_EXPERTS
    n_used = pad_end[-1] // tm
    tile = jnp.minimum(jnp.arange(n_tiles, dtype=i32), n_used - 1)
    tile_e = jnp.sum((tile[:, None] * tm >= pad_end[None, :]).astype(i32), axis=1)
    slot = jnp.arange(n_tiles * tm, dtype=i32)
    slot_e = jnp.sum((jnp.minimum(slot, pad_end[-1] - 1)[:, None] >= pad_end[None, :]).astype(i32), axis=1)
    pos = slot - lookup(pad_start, slot_e)
    valid = (pos < lookup(counts, slot_e)) & (slot < pad_end[-1])
    src = jnp.where(valid, order[jnp.clip(lookup(start, slot_e) + pos, 0, 2 * m - 1)] % m, 0)
    slot_of = lookup(pad_start - start, e_flat) + rank
    return (tile_e, n_used.reshape(1).astype(i32), (src * SUBLANES).astype(i32), (slot_of * SUBLANES).astype(i32))


def _experts_kernel(te_ref, nu_ref, src_ref, h_hbm, wg_ref, wu_ref, wd_ref, o_ref, xg_ref, xb_ref, acc_ref, sems,
                    *, n_tiles, n_steps):
    t, j = pl.program_id(0), pl.program_id(1)
    tm = xb_ref.shape[0]
    rows_per_step = tm // n_steps
    n_used = nu_ref[0]
    used = t < n_used
    buf = t % 2

    def start_row(tile, dst_buf, r):
        src = pl.multiple_of(src_ref[tile * tm + r], SUBLANES)
        dst = pl.multiple_of(r * SUBLANES, SUBLANES)
        pltpu.make_async_copy(h_hbm.at[pl.ds(src, SUBLANES)], xg_ref.at[dst_buf, pl.ds(dst, SUBLANES)],
                              sems.at[dst_buf]).start()

    def wait_rows(b_):
        pltpu.make_async_copy(h_hbm.at[pl.ds(0, tm * SUBLANES)], xg_ref.at[b_], sems.at[b_]).wait()

    @pl.when((t == 0) & (j == 0))
    def _():
        def issue(r, carry):
            start_row(0, 0, r)
            return carry

        lax.fori_loop(0, tm, issue, 0, unroll=8)

    @pl.when((j == 0) & ((t == 0) | (t - 1 < n_used)))
    def _():
        wait_rows(buf)
        xb_ref[...] = _load_token_major(xg_ref.at[buf], tm).astype(BF16)

    @pl.when(j == 0)
    def _():
        acc_ref[...] = jnp.zeros_like(acc_ref)

    @pl.when(used)
    def _():
        nxt = jnp.minimum(t + 1, n_tiles - 1)
        for r in range(rows_per_step):
            start_row(nxt, 1 - buf, j * rows_per_step + r)
        h = xb_ref[...]
        a = jnp.dot(h, wg_ref[...].astype(BF16), preferred_element_type=F32)
        u = jnp.dot(h, wu_ref[...].astype(BF16), preferred_element_type=F32)
        act = (_silu(a) * u).astype(BF16)
        acc_ref[...] += jnp.dot(act, wd_ref[...].astype(BF16), preferred_element_type=F32)

    @pl.when(j == n_steps - 1)
    def _():
        _store_token_major(o_ref, acc_ref[...])

    @pl.when(used & (t == n_tiles - 1) & (j == n_steps - 1))
    def _():
        wait_rows(1 - buf)


def _moe_ff_tile(dff):
    return MOE_FF_TILE if dff % MOE_FF_TILE == 0 else dff


def _moe_tile_rows(m, dff):
    per_expert = 2.0 * m / N_EXPERTS
    unit = (dff // _moe_ff_tile(dff)) * 2 * SUBLANES
    rows = (per_expert + 3.0 * math.sqrt(per_expert)) / MOE_TILES_PER_EXPERT
    return int(math.ceil(rows / unit)) * unit


def _experts(h2, tables, wg, wu, wd, layer, tm):
    d, dff = wg.shape[-2:]
    assert d == SUBLANES * LANES
    tile_e, n_used, src, _ = tables
    n_tiles = tile_e.shape[0]
    tf = MOE_FF_TILE if dff % MOE_FF_TILE == 0 else dff
    nj = dff // tf
    assert tm % nj == 0
    col = lambda t, j, te, nu, src: (layer, te[t], 0, jnp.where(t < nu[0], j, nj - 1))
    rowb = lambda t, j, te, nu, src: (layer, te[t], jnp.where(t < nu[0], j, nj - 1), 0)
    return pl.pallas_call(
        functools.partial(_experts_kernel, n_tiles=n_tiles, n_steps=nj),
        grid_spec=pltpu.PrefetchScalarGridSpec(
            num_scalar_prefetch=3,
            grid=(n_tiles, nj),
            in_specs=[pl.BlockSpec(memory_space=pl.ANY),
                      pl.BlockSpec((None, None, d, tf), col),
                      pl.BlockSpec((None, None, d, tf), col),
                      pl.BlockSpec((None, None, tf, d), rowb)],
            out_specs=pl.BlockSpec((tm * SUBLANES, LANES), lambda t, j, te, nu, src: (t, 0)),
            scratch_shapes=[pltpu.VMEM((2, tm * SUBLANES, LANES), F32), pltpu.VMEM((tm, d), BF16),
                            pltpu.VMEM((tm, d), F32), pltpu.SemaphoreType.DMA((2,))]),
        out_shape=jax.ShapeDtypeStruct((n_tiles * tm * SUBLANES, LANES), F32),
        compiler_params=_params(("arbitrary", "arbitrary")),
        name="moe_experts",
    )(tile_e, n_used, src, h2, wg, wu, wd)


def _combine_kernel(slot_ref, x_ref, sel_ref, mod_ref, g_ref, y_hbm, o_ref, r1_ref, r2_ref, sems,
                    *, d, m, rows_per_sample, final):
    t = x_ref.shape[0]
    base = pl.program_id(0) * rows_per_sample + pl.program_id(1) * t

    def issue(k, carry):
        dst = pl.ds(pl.multiple_of(k * SUBLANES, SUBLANES), SUBLANES)
        s1 = pl.multiple_of(slot_ref[base + k], SUBLANES)
        s2 = pl.multiple_of(slot_ref[m + base + k], SUBLANES)
        pltpu.make_async_copy(y_hbm.at[pl.ds(s1, SUBLANES)], r1_ref.at[dst], sems.at[0]).start()
        pltpu.make_async_copy(y_hbm.at[pl.ds(s2, SUBLANES)], r2_ref.at[dst], sems.at[1]).start()
        return carry

    lax.fori_loop(0, t, issue, 0, unroll=8)
    pltpu.make_async_copy(y_hbm.at[pl.ds(0, t * SUBLANES)], r1_ref, sems.at[0]).wait()
    pltpu.make_async_copy(y_hbm.at[pl.ds(0, t * SUBLANES)], r2_ref, sems.at[1]).wait()
    y = sel_ref[:, 0:1] * _load_token_major(r1_ref, t) + sel_ref[:, 1:2] * _load_token_major(r2_ref, t)
    x = x_ref[...] + mod_ref[:, 5 * d:6 * d] * y
    if final:
        x = x * lax.rsqrt(jnp.mean(x * x, axis=-1, keepdims=True) + EPS) * g_ref[...]
    o_ref[...] = x


def _combine(xs, sel, mod_l, y_sorted, slot_of, final_g, n_lat_tiles, n_row_tiles, final):
    b, _, d = xs.shape
    t = ROW_TILE
    rows = n_row_tiles * t
    row = lambda i, r, s: (i, r, 0)
    return pl.pallas_call(
        functools.partial(_combine_kernel, d=d, m=b * rows, rows_per_sample=rows, final=final),
        grid_spec=pltpu.PrefetchScalarGridSpec(
            num_scalar_prefetch=1,
            grid=(b, n_row_tiles),
            in_specs=[pl.BlockSpec((None, t, d), row),
                      pl.BlockSpec((None, t, LANES), row),
                      pl.BlockSpec((None, 1, mod_l.shape[-1]),
                                   lambda i, r, s: (jnp.where(r < n_lat_tiles, i, b), 0, 0)),
                      pl.BlockSpec((1, d), lambda i, r, s: (0, 0)),
                      pl.BlockSpec(memory_space=pl.ANY)],
            out_specs=pl.BlockSpec((None, t, d), row),
            scratch_shapes=[pltpu.VMEM((t * SUBLANES, LANES), F32), pltpu.VMEM((t * SUBLANES, LANES), F32),
                            pltpu.SemaphoreType.DMA((2,))]),
        out_shape=jax.ShapeDtypeStruct((b, rows, d), F32),
        compiler_params=_params(("arbitrary", "arbitrary")),
        name="moe_combine",
    )(slot_of, xs, sel, mod_l, final_g, y_sorted)


def _rope_tables(s_lat, n_ctx):
    rows = s_lat // GRID_W
    pos_row = jnp.repeat(jnp.arange(rows, dtype=jnp.int32), GRID_W).astype(F32)
    pos_col = jnp.tile(jnp.arange(GRID_W, dtype=jnp.int32), rows).astype(F32)
    quarter = HEAD_DIM // 4
    inv_freq = ROPE_THETA ** (-jnp.arange(quarter, dtype=F32) / quarter)
    ang_r = pos_row[:, None] * inv_freq
    ang_c = pos_col[:, None] * inv_freq
    ang = jnp.concatenate([ang_r, ang_r, ang_c, ang_c], axis=-1)
    cos, sin = jnp.cos(ang), jnp.sin(ang)
    first = (np.arange(HEAD_DIM) % (2 * quarter)) < quarter
    sin_lo = jnp.where(first, -sin, 0.0)
    sin_hi = jnp.where(first, 0.0, sin)

    def widen(tab, ctx_fill):
        tab = jnp.concatenate([tab, jnp.full((n_ctx, HEAD_DIM), ctx_fill, F32)], axis=0)
        return jnp.tile(tab, (1, LANES // HEAD_DIM))

    return widen(cos, 1.0), widen(sin_lo, 0.0), widen(sin_hi, 0.0)


def _dft_tables(n):
    k = np.arange(n, dtype=np.int64)
    ang = 2.0 * np.pi * ((k[:, None] * k[None, :]) % n).astype(np.float64) / n
    return np.cos(ang), np.sin(ang)


def _constants(s_lat, n_ctx):
    c64, s64 = _dft_tables(F_GROUP_DIM)
    eye = np.eye(F_GROUPS)
    cn, sn = _dft_tables(s_lat)
    cc, sc = _dft_tables(n_ctx)
    bf = lambda a: jnp.asarray(a, dtype=F32).astype(BF16)
    return dict(
        gmat=bf(np.kron(np.eye(512 // HEAD_DIM), np.ones((HEAD_DIM, HEAD_DIM)))),
        c64=bf(np.kron(eye, c64) / math.sqrt(F_GROUP_DIM)),
        s64=bf(np.kron(eye, s64) / math.sqrt(F_GROUP_DIM)),
        cn=bf(cn / math.sqrt(s_lat)), sn=bf(sn / math.sqrt(s_lat)),
        cc=bf(cc / math.sqrt(n_ctx)), sc=bf(sc / math.sqrt(n_ctx)),
    )


def _dup_heads(w, off, n_heads):
    cols = []
    for hd in range(n_heads):
        blk = w[:, off + hd * HEAD_DIM: off + (hd + 1) * HEAD_DIM]
        cols += [blk, blk]
    return cols


def _relayout_w_in(w):
    cols = [w[:, _OFF["gt"]:_OFF["end"]], w[:, _OFF["aq"]:_OFF["bq"]], w[:, _OFF["bq"]:_OFF["bk"]],
            w[:, _OFF["cq"]:_OFF["ck"]], w[:, _OFF["fz"]:_OFF["gt"]]]
    cols += _dup_heads(w, _OFF["bk"], B_KV) + _dup_heads(w, _OFF["bv"], B_KV)
    cols += _dup_heads(w, _OFF["ck"], C_KV) + _dup_heads(w, _OFF["cv"], C_KV)
    return jnp.concatenate(cols, axis=1).astype(BF16)


def kernel(x, c, ctx, c_ctx, w_mod, b_mod, norm1_g, norm2_g, w_in, diff_lambda, diff_subln_g, qk_norm_g,
           sink_logits, w_branch, w_out, ffn_w_gate, ffn_w_up, ffn_w_down, router_w, moe_w_gate, moe_w_up,
           moe_w_down, final_g):
    b, s_lat, d = x.shape
    n_ctx = ctx.shape[1]
    depth = w_mod.shape[0]
    nt = s_lat + n_ctx
    t = ROW_TILE
    assert s_lat % t == 0 and n_ctx % t == 0 and s_lat % n_ctx == 0 and b < MOD_ROWS
    assert w_in.shape[-1] == _OFF["end"] and d * N_BRANCH == _OFF["end"] - _OFF["gt"]
    n_lat_tiles = s_lat // t

    xs = jnp.concatenate([x, ctx], axis=1)
    cvec = jnp.concatenate([c, c_ctx[None], jnp.zeros((MOD_ROWS - b - 1, d), F32)], axis=0)
    mod = _modulation(cvec, w_mod, b_mod).reshape(depth, MOD_ROWS, 1, 6 * d)
    tabs = _rope_tables(s_lat, n_ctx)
    consts = _constants(s_lat, n_ctx)

    out = None
    for l in range(depth):
        last = l == depth - 1
        mod_l = mod[l]
        p = _inproj(xs, mod_l, norm1_g[l][None], _relayout_w_in(w_in[l]), tabs,
                    jnp.tile(qk_norm_g[l, 0], 512 // HEAD_DIM)[None], jnp.tile(qk_norm_g[l, 1], 256 // HEAD_DIM)[None],
                    consts, n_lat_tiles)
        lam_init = 0.8 - 0.6 * math.exp(-0.3 * l)
        n_q_tiles = n_lat_tiles if last else nt // t
        oa, ob, oc = _attention(p, diff_lambda[l], diff_subln_g[l][None], sink_logits[l],
                                n_lat_tiles, n_q_tiles, s_lat, lam_init)
        od = _fourier(p[10], p[11], consts, n_lat_tiles, n_q_tiles, s_lat)
        xs = _merge(xs, mod_l, p[0], (oa, ob, oc, od), w_branch[l].astype(BF16), w_out[l].astype(BF16),
                    n_lat_tiles, n_q_tiles)

        i = l // 2
        if l % 2 == 0:
            assert not last
            xs = _ffn(xs, mod_l, norm2_g[l][None], ffn_w_gate[i].astype(BF16), ffn_w_up[i].astype(BF16),
                      ffn_w_down[i].astype(BF16), n_lat_tiles)
        else:
            w_r = jnp.pad(router_w[i], ((0, 0), (0, LANES - N_EXPERTS)))
            n_row_tiles = n_q_tiles
            m = b * n_row_tiles * t
            h2, sel = _router(xs, mod_l, norm2_g[l][None], w_r, n_lat_tiles, n_row_tiles)
            tm = _moe_tile_rows(m, moe_w_gate.shape[-1])
            tables = _dispatch(sel.reshape(m, LANES), tm)
            y_sorted = _experts(h2.reshape(m * SUBLANES, LANES), tables, moe_w_gate, moe_w_up, moe_w_down, i, tm)
            res = _combine(xs, sel, mod_l, y_sorted, tables[3], final_g[None], n_lat_tiles, n_row_tiles, last)
            if last:
                out = res
            else:
                xs = res
    if out is None:
        raise NotImplementedError("final normalisation is fused into the last (mixture-of-experts) layer")
    return out
```

```python
import functools
import math

import numpy as np
import jax
import jax.numpy as jnp
from jax import lax
from jax.experimental import pallas as pl
from jax.experimental.pallas import tpu as pltpu

F32 = jnp.float32
BF16 = jnp.bfloat16

GRID_W = 64
HEAD_DIM = 64
ROPE_THETA = 10000.0
WINDOW = 128
A_HEADS = 4
B_HEADS = 8
B_KV = 2
C_HEADS = 8
C_KV = 2
F_GROUPS = 8
F_GROUP_DIM = 64
BRANCH_W = 512
N_BRANCH = 4
N_EXPERTS = 8
EPS = 1e-6
SUBLN_EPS = 1e-5
NEG_INF = -1e30
LOG2E = math.log2(math.e)

LANES = 128
SUBLANES = 8
ROW_TILE = 256
WIDE_TILES = (768, 512, 256)
FFN_CHUNK = 768
MOE_TILES_PER_EXPERT = 4
MOE_FF_TILE = 512
MOD_ROWS = 16
VMEM_LIMIT = 56 * 1024 * 1024

_OFF = dict(aq=0, ak=512, av=1024, bq=1536, bk=2048, bv=2176, cq=2304, ck=2816, cv=2944,
            fz=3072, gt=3584, end=7680)
_P = dict(gt=0, aq=4096, ak=4608, av=5120, bq=5632, cq=6144, fz=6656,
          bk=7168, bv=7424, ck=7680, cv=7936, end=8192)


def _params(sem):
    return pltpu.CompilerParams(dimension_semantics=sem, vmem_limit_bytes=VMEM_LIMIT)


def _full(shape):
    n = len(shape)
    return pl.BlockSpec(shape, lambda *_: (0,) * n)


def _silu(x):
    return x / (1.0 + jnp.exp(-x))


def _sigmoid(x):
    return 1.0 / (1.0 + jnp.exp(-x))


def _adaln(x, g, shift, scale):
    y = x * lax.rsqrt(jnp.mean(x * x, axis=-1, keepdims=True) + EPS) * g
    return y * (1.0 + scale) + shift


def _mod_kernel(c_ref, w_ref, b_ref, o_ref):
    s = _silu(c_ref[...]).astype(BF16)
    o_ref[...] = jnp.dot(s, w_ref[...].astype(BF16), preferred_element_type=F32) + b_ref[...]


def _modulation(cvec, w_mod, b_mod):
    depth, d, n = w_mod.shape
    tn = 1536 if n % 1536 == 0 else n
    return pl.pallas_call(
        _mod_kernel,
        grid=(depth, n // tn),
        in_specs=[_full((MOD_ROWS, d)),
                  pl.BlockSpec((None, d, tn), lambda l, j: (l, 0, j)),
                  pl.BlockSpec((None, 1, tn), lambda l, j: (l, 0, j))],
        out_specs=pl.BlockSpec((None, MOD_ROWS, tn), lambda l, j: (l, 0, j)),
        out_shape=jax.ShapeDtypeStruct((depth, MOD_ROWS, n), F32),
        compiler_params=_params(("arbitrary", "arbitrary")),
        name="modulation",
    )(cvec, w_mod, b_mod.reshape(depth, 1, n))


def _rope(p, cos, sin_lo, sin_hi):
    outs = []
    for c in range(p.shape[1] // LANES):
        xc = p[:, c * LANES:(c + 1) * LANES]
        outs.append(xc * cos + pltpu.roll(xc, LANES - 16, 1) * sin_lo + pltpu.roll(xc, 16, 1) * sin_hi)
    return outs[0] if len(outs) == 1 else jnp.concatenate(outs, axis=1)


def _group_rmsnorm(y, gmat, g):
    ss = jnp.dot((y * y).astype(BF16), gmat, preferred_element_type=F32)
    return y * lax.rsqrt(ss * (1.0 / HEAD_DIM) + EPS) * g


def _inproj_kernel(x_ref, mod_ref, g_ref, w_ref, cos_ref, slo_ref, shi_ref, qg_ref, kg_ref, gmat_ref,
                   c64_ref, s64_ref,
                   gt_ref, aq_ref, akt_ref, av_ref, bq_ref, bkt_ref, bv_ref, cq_ref, ckt_ref, cv_ref,
                   fzc_ref, fzs_ref, *, d):
    h = _adaln(x_ref[...], g_ref[...], mod_ref[:, 0:d], mod_ref[:, d:2 * d]).astype(BF16)
    cos, slo, shi = cos_ref[...], slo_ref[...], shi_ref[...]
    scale = HEAD_DIM ** -0.5 * LOG2E

    def proj(name, width):
        return jnp.dot(h, w_ref[:, _P[name]:_P[name] + width], preferred_element_type=F32)

    for c in range(N_BRANCH * d // 512):
        p = jnp.dot(h, w_ref[:, c * 512:(c + 1) * 512], preferred_element_type=F32)
        gt_ref[:, c * 512:(c + 1) * 512] = _sigmoid(p).astype(BF16)

    aq_ref[...] = (_rope(proj("aq", 512), cos, slo, shi) * scale).astype(BF16)
    akt_ref[...] = _rope(proj("ak", 512), cos, slo, shi).T.astype(BF16)
    av_ref[...] = proj("av", 512).astype(BF16)

    bq = _group_rmsnorm(proj("bq", 512), gmat_ref[...], qg_ref[...])
    bq_ref[...] = (_rope(bq, cos, slo, shi) * scale).astype(BF16)
    bk = _group_rmsnorm(proj("bk", 256), gmat_ref[0:256, 0:256], kg_ref[...])
    bkt_ref[...] = _rope(bk, cos, slo, shi).T.astype(BF16)
    bv_ref[...] = proj("bv", 256).astype(BF16)

    cq_ref[...] = (_rope(proj("cq", 512), cos, slo, shi) * scale).astype(BF16)
    ckt_ref[...] = _rope(proj("ck", 256), cos, slo, shi).T.astype(BF16)
    cv_ref[...] = proj("cv", 256).astype(BF16)

    fz = proj("fz", 512).astype(BF16)
    fzc_ref[...] = jnp.dot(fz, c64_ref[...], preferred_element_type=F32).astype(BF16)
    fzs_ref[...] = jnp.dot(fz, s64_ref[...], preferred_element_type=F32).astype(BF16)


def _inproj(xs, mod_l, g, w_p, tabs, qg, kg, consts, n_lat_tiles):
    b, nt, d = xs.shape
    t = ROW_TILE
    n_mod = mod_l.shape[-1]
    row = lambda i, r: (i, r, 0)
    tr = lambda i, r: (i, 0, r)
    tab = pl.BlockSpec((t, LANES), lambda i, r: (r, 0))

    def out(width):
        return pl.BlockSpec((None, t, width), row), jax.ShapeDtypeStruct((b, nt, width), BF16)

    def out_t(width):
        return pl.BlockSpec((None, width, t), tr), jax.ShapeDtypeStruct((b, width, nt), BF16)

    outs = [out(N_BRANCH * d), out(512), out_t(512), out(512), out(512), out_t(256), out(256),
            out(512), out_t(256), out(256), out(512), out(512)]
    return pl.pallas_call(
        functools.partial(_inproj_kernel, d=d),
        grid=(b, nt // t),
        in_specs=[pl.BlockSpec((None, t, d), row),
                  pl.BlockSpec((None, 1, n_mod), lambda i, r: (jnp.where(r < n_lat_tiles, i, b), 0, 0)),
                  _full((1, d)),
                  pl.BlockSpec(w_p.shape, lambda i, r: (0, 0), pipeline_mode=pl.Buffered(1)),
                  tab, tab, tab,
                  _full((1, 512)), _full((1, 256)), _full((512, 512)), _full((512, 512)), _full((512, 512))],
        out_specs=[o[0] for o in outs],
        out_shape=[o[1] for o in outs],
        compiler_params=_params(("arbitrary", "arbitrary")),
        name="inproj",
    )(xs, mod_l, g, w_p, tabs[0], tabs[1], tabs[2], qg, kg, consts["gmat"], consts["c64"], consts["s64"])


def _half_masks(shape):
    lane = lax.broadcasted_iota(jnp.int32, shape, 1)
    return lane < HEAD_DIM, lane >= HEAD_DIM


def _exp2_bf16(s):
    return jnp.exp2(s - jnp.max(s, axis=-1, keepdims=True)).astype(BF16)


def _pipelined_scores(score_fns):
    nxt = score_fns[0]()
    for i in range(len(score_fns)):
        cur = nxt
        if i + 1 < len(score_fns):
            nxt = score_fns[i + 1]()
        yield cur


def _attn_a_kernel(q_ref, kt_ref, v_ref, lp_ref, g_ref, o_ref, *, n_lat_tiles, s_lat, lam_init):
    r = pl.program_id(1)
    lp = lp_ref[...]
    lam = (jnp.exp(jnp.sum(lp[0:1] * lp[1:2], axis=-1, keepdims=True))
           - jnp.exp(jnp.sum(lp[2:3] * lp[3:4], axis=-1, keepdims=True)) + lam_init)
    g = g_ref[...] * (1.0 - lam_init)
    nt = kt_ref.shape[1]

    def body(k0):
        lo, hi = _half_masks((q_ref.shape[0], LANES))
        ones = jnp.ones((nt - k0, LANES), BF16)

        def score_fn(hd, msk):
            cs = slice(hd * LANES, (hd + 1) * LANES)
            return lambda: jnp.dot(jnp.where(msk, q_ref[:, cs], 0), kt_ref[cs, k0:nt], preferred_element_type=F32)

        fns = [score_fn(hd, msk) for hd in range(A_HEADS) for msk in (lo, hi)]
        outs = []
        for i, sc in enumerate(_pipelined_scores(fns)):
            hd = i // 2
            cs = slice(hd * LANES, (hd + 1) * LANES)
            vo = jnp.concatenate([v_ref[k0:nt, cs], ones], axis=1)
            full = jnp.dot(_exp2_bf16(sc), vo, preferred_element_type=F32)
            outs.append(full[:, 0:LANES] * (1.0 / full[:, LANES:LANES + 1]))
            if len(outs) == 2:
                o = outs[0] - lam * outs[1]
                outs = []
                o = o * lax.rsqrt(jnp.mean(o * o, axis=-1, keepdims=True) + SUBLN_EPS) * g
                o_ref[:, cs] = o.astype(BF16)

    pl.when(r < n_lat_tiles)(lambda: body(0))
    pl.when(r >= n_lat_tiles)(lambda: body(s_lat))


def _attn_b_kernel(q_ref, kt_ref, v_ref, o_ref, *, n_lat_tiles, s_lat):
    r = pl.program_id(1)
    nt = kt_ref.shape[1]

    def body(k0):
        lo, hi = _half_masks((q_ref.shape[0], LANES))
        ones = jnp.ones((nt - k0, LANES), BF16)

        def score_fn(pair, msk):
            kv = pair // (B_HEADS // B_KV // 2)
            q = q_ref[:, pair * LANES:(pair + 1) * LANES]
            return lambda: jnp.dot(jnp.where(msk, q, 0), kt_ref[kv * LANES:(kv + 1) * LANES, k0:nt],
                                   preferred_element_type=F32)

        fns = [score_fn(pair, msk) for pair in range(B_HEADS // 2) for msk in (lo, hi)]
        halves = []
        for i, sc in enumerate(_pipelined_scores(fns)):
            pair = i // 2
            kv = pair // (B_HEADS // B_KV // 2)
            vo = jnp.concatenate([v_ref[k0:nt, kv * LANES:(kv + 1) * LANES], ones], axis=1)
            full = jnp.dot(_exp2_bf16(sc), vo, preferred_element_type=F32)
            halves.append(full[:, 0:LANES] * (1.0 / full[:, LANES:LANES + 1]))
            if len(halves) == 2:
                o_ref[:, pair * LANES:(pair + 1) * LANES] = jnp.where(lo, halves[0], halves[1]).astype(BF16)
                halves = []

    pl.when(r < n_lat_tiles)(lambda: body(0))
    pl.when(r >= n_lat_tiles)(lambda: body(s_lat))


def _attn_c_kernel(sink_ref, q_ref, ktl_ref, ktm_ref, ktr_ref, ktc_ref, vl_ref, vm_ref, vr_ref, vc_ref, o_ref,
                   *, n_lat_tiles, s_lat):
    r = pl.program_id(1)
    t = q_ref.shape[0]
    n_band = t + 2 * WINDOW
    n_keys = n_band + ktc_ref.shape[1]
    u = lax.broadcasted_iota(jnp.int32, (t, n_keys), 1)
    row = lax.broadcasted_iota(jnp.int32, (t, n_keys), 0)
    j = r * t - WINDOW + u
    in_band = (u >= row) & (u <= row + 2 * WINDOW) & (j >= 0) & (j < s_lat) & (r < n_lat_tiles)
    valid = in_band | (u >= n_band)
    lo, hi = _half_masks((t, LANES))
    ones = jnp.ones((n_keys, LANES), BF16)
    kts, vos = [], []
    for kv in range(C_KV):
        ks = slice(kv * LANES, (kv + 1) * LANES)
        kts.append(jnp.concatenate([ktl_ref[ks, :], ktm_ref[ks, :], ktr_ref[ks, :], ktc_ref[ks, :]], axis=1))
        v = jnp.concatenate([vl_ref[:, ks], vm_ref[:, ks], vr_ref[:, ks], vc_ref[:, ks]], axis=0)
        vos.append(jnp.concatenate([v, ones], axis=1))

    def score_fn(pair, msk):
        kv = pair // (C_HEADS // C_KV // 2)
        q = q_ref[:, pair * LANES:(pair + 1) * LANES]
        return lambda: jnp.dot(jnp.where(msk, q, 0), kts[kv], preferred_element_type=F32)

    fns = [score_fn(pair, msk) for pair in range(C_HEADS // 2) for msk in (lo, hi)]
    halves = []
    for i, sc in enumerate(_pipelined_scores(fns)):
        pair = i // 2
        kv = pair // (C_HEADS // C_KV // 2)
        sink = sink_ref[i] * LOG2E
        sc = jnp.where(valid, sc, NEG_INF)
        m = jnp.maximum(jnp.max(sc, axis=-1, keepdims=True), sink)
        full = jnp.dot(jnp.exp2(sc - m).astype(BF16), vos[kv], preferred_element_type=F32)
        l = full[:, LANES:LANES + 1] + jnp.exp2(sink - m)
        halves.append(full[:, 0:LANES] * (1.0 / l))
        if len(halves) == 2:
            o_ref[:, pair * LANES:(pair + 1) * LANES] = jnp.where(lo, halves[0], halves[1]).astype(BF16)
            halves = []


def _attention(p, lp, subln_g, sink, n_lat_tiles, n_q_tiles, s_lat, lam_init):
    (_, aq, akt, av, bq, bkt, bv, cq, ckt, cv, _, _) = p
    b, nt, _ = aq.shape
    t = ROW_TILE
    grid = (b, n_q_tiles)
    row = lambda i, r: (i, r, 0)
    whole = lambda i, r: (i, 0, 0)
    sem = _params(("arbitrary", "arbitrary"))
    o_spec = pl.BlockSpec((None, t, 512), row)
    o_shape = jax.ShapeDtypeStruct((b, n_q_tiles * t, 512), BF16)

    oa = pl.pallas_call(
        functools.partial(_attn_a_kernel, n_lat_tiles=n_lat_tiles, s_lat=s_lat, lam_init=lam_init),
        grid=grid,
        in_specs=[pl.BlockSpec((None, t, 512), row), pl.BlockSpec((None, 512, nt), whole),
                  pl.BlockSpec((None, nt, 512), whole), _full((4, HEAD_DIM)), _full((1, 2 * HEAD_DIM))],
        out_specs=o_spec, out_shape=o_shape, compiler_params=sem, name="attn_diff",
    )(aq, akt, av, lp, subln_g)

    ob = pl.pallas_call(
        functools.partial(_attn_b_kernel, n_lat_tiles=n_lat_tiles, s_lat=s_lat),
        grid=grid,
        in_specs=[pl.BlockSpec((None, t, 512), row), pl.BlockSpec((None, 256, nt), whole),
                  pl.BlockSpec((None, nt, 256), whole)],
        out_specs=o_spec, out_shape=o_shape, compiler_params=sem, name="attn_gqa",
    )(bq, bkt, bv)

    wpt = t // WINDOW
    last = s_lat // WINDOW - 1
    left = lambda r: jnp.clip(r * wpt - 1, 0, last)
    mid = lambda r: jnp.minimum(r, n_lat_tiles - 1)
    right = lambda r: jnp.clip((r + 1) * wpt, 0, last)
    n_ctx = nt - s_lat
    ctx_blk = s_lat // n_ctx
    oc = pl.pallas_call(
        functools.partial(_attn_c_kernel, n_lat_tiles=n_lat_tiles, s_lat=s_lat),
        grid=grid,
        in_specs=[pl.BlockSpec(memory_space=pltpu.SMEM),
                  pl.BlockSpec((None, t, 512), row),
                  pl.BlockSpec((None, 256, WINDOW), lambda i, r: (i, 0, left(r))),
                  pl.BlockSpec((None, 256, t), lambda i, r: (i, 0, mid(r))),
                  pl.BlockSpec((None, 256, WINDOW), lambda i, r: (i, 0, right(r))),
                  pl.BlockSpec((None, 256, n_ctx), lambda i, r: (i, 0, ctx_blk)),
                  pl.BlockSpec((None, WINDOW, 256), lambda i, r: (i, left(r), 0)),
                  pl.BlockSpec((None, t, 256), lambda i, r: (i, mid(r), 0)),
                  pl.BlockSpec((None, WINDOW, 256), lambda i, r: (i, right(r), 0)),
                  pl.BlockSpec((None, n_ctx, 256), lambda i, r: (i, ctx_blk, 0))],
        out_specs=o_spec, out_shape=o_shape, compiler_params=sem, name="attn_window",
    )(sink, cq, ckt, ckt, ckt, ckt, cv, cv, cv, cv)
    return oa, ob, oc


def _fourier_kernel(cn_ref, sn_ref, cc_ref, sc_ref, zc_ref, zs_ref, o_ref, *, n_lat_tiles, s_lat):
    r = pl.program_id(1)
    nt = zc_ref.shape[0]

    @pl.when(r < n_lat_tiles)
    def _():
        y = (jnp.dot(cn_ref[...], zc_ref[0:s_lat, :], preferred_element_type=F32)
             - jnp.dot(sn_ref[...], zs_ref[0:s_lat, :], preferred_element_type=F32))
        o_ref[...] = y.astype(BF16)

    @pl.when(r >= n_lat_tiles)
    def _():
        y = (jnp.dot(cc_ref[...], zc_ref[s_lat:nt, :], preferred_element_type=F32)
             - jnp.dot(sc_ref[...], zs_ref[s_lat:nt, :], preferred_element_type=F32))
        o_ref[...] = y.astype(BF16)


def _fourier(fzc, fzs, consts, n_lat_tiles, n_q_tiles, s_lat):
    b, nt, w = fzc.shape
    t = ROW_TILE
    n_ctx = nt - s_lat
    lat_rows = lambda i, r: (jnp.minimum(r, n_lat_tiles - 1), 0)
    ctx_rows = lambda i, r: (jnp.maximum(r - n_lat_tiles, 0), 0)
    whole = lambda i, r: (i, 0, 0)
    return pl.pallas_call(
        functools.partial(_fourier_kernel, n_lat_tiles=n_lat_tiles, s_lat=s_lat),
        grid=(b, n_q_tiles),
        in_specs=[pl.BlockSpec((t, s_lat), lat_rows), pl.BlockSpec((t, s_lat), lat_rows),
                  pl.BlockSpec((t, n_ctx), ctx_rows), pl.BlockSpec((t, n_ctx), ctx_rows),
                  pl.BlockSpec((None, nt, w), whole), pl.BlockSpec((None, nt, w), whole)],
        out_specs=pl.BlockSpec((None, t, w), lambda i, r: (i, r, 0)),
        out_shape=jax.ShapeDtypeStruct((b, n_q_tiles * t, w), BF16),
        compiler_params=_params(("arbitrary", "arbitrary")),
        name="fourier",
    )(consts["cn"], consts["sn"], consts["cc"], consts["sc"], fzc, fzs)


def _wide_tile(rows):
    for t in WIDE_TILES:
        if rows % t == 0:
            return t
    raise ValueError(rows)


def _wide_mod_specs(mod_l, b):
    n = mod_l.shape[-1]
    return [pl.BlockSpec((None, 1, n), lambda i, r: (i, 0, 0)), pl.BlockSpec((None, 1, n), lambda i, r: (b, 0, 0))]


def _row_mod(modb_ref, modc_ref, rows, s_lat, d):
    is_lat = pl.program_id(1) * rows + lax.broadcasted_iota(jnp.int32, (rows, 1), 0) < s_lat
    return lambda k: jnp.where(is_lat, modb_ref[:, k * d:(k + 1) * d], modc_ref[:, k * d:(k + 1) * d])


def _merge_kernel(x_ref, modb_ref, modc_ref, gt_ref, oa_ref, ob_ref, oc_ref, od_ref, wbr_ref, wo_ref, o_ref,
                  *, d, s_lat):
    mod = _row_mod(modb_ref, modc_ref, x_ref.shape[0], s_lat, d)
    merged = None
    for i, br in enumerate((oa_ref, ob_ref, oc_ref, od_ref)):
        y = jnp.dot(br[...], wbr_ref[i], preferred_element_type=F32)
        y = gt_ref[:, i * d:(i + 1) * d].astype(F32) * y
        merged = y if merged is None else merged + y
    out = jnp.dot(merged.astype(BF16), wo_ref[...], preferred_element_type=F32)
    o_ref[...] = x_ref[...] + mod(2) * out


def _merge(xs, mod_l, gates, branches, w_br, w_o, s_lat, rows):
    b, _, d = xs.shape
    t = _wide_tile(rows)
    row = lambda i, r: (i, r, 0)
    br_spec = pl.BlockSpec((None, t, BRANCH_W), row)
    const = lambda shape: pl.BlockSpec(shape, lambda i, r: (0,) * len(shape), pipeline_mode=pl.Buffered(1))
    return pl.pallas_call(
        functools.partial(_merge_kernel, d=d, s_lat=s_lat),
        grid=(b, rows // t),
        in_specs=[pl.BlockSpec((None, t, d), row), *_wide_mod_specs(mod_l, b),
                  pl.BlockSpec((None, t, N_BRANCH * d), row),
                  br_spec, br_spec, br_spec, br_spec,
                  const(w_br.shape), const(w_o.shape)],
        out_specs=pl.BlockSpec((None, t, d), row),
        out_shape=jax.ShapeDtypeStruct((b, rows, d), F32),
        compiler_params=_params(("arbitrary", "arbitrary")),
        name="merge",
    )(xs, mod_l, mod_l, gates, *branches, w_br, w_o)


def _ffn_kernel(x_ref, modb_ref, modc_ref, g_ref, wg_ref, wu_ref, wd_ref, o_ref, *, d, s_lat):
    mod = _row_mod(modb_ref, modc_ref, x_ref.shape[0], s_lat, d)
    x = x_ref[...]
    h = _adaln(x, g_ref[...], mod(3), mod(4)).astype(BF16)
    dff = wg_ref.shape[1]
    y = None
    for c0 in range(0, dff, FFN_CHUNK):
        c1 = min(c0 + FFN_CHUNK, dff)
        a = jnp.dot(h, wg_ref[:, c0:c1], preferred_element_type=F32)
        u = jnp.dot(h, wu_ref[:, c0:c1], preferred_element_type=F32)
        part = jnp.dot((_silu(a) * u).astype(BF16), wd_ref[c0:c1, :], preferred_element_type=F32)
        y = part if y is None else y + part
    o_ref[...] = x + mod(5) * y


def _ffn(xs, mod_l, g, wg, wu, wd, s_lat):
    b, nt, d = xs.shape
    t = _wide_tile(nt)
    row = lambda i, r: (i, r, 0)
    const = lambda shape: pl.BlockSpec(shape, lambda i, r: (0, 0), pipeline_mode=pl.Buffered(1))
    return pl.pallas_call(
        functools.partial(_ffn_kernel, d=d, s_lat=s_lat),
        grid=(b, nt // t),
        in_specs=[pl.BlockSpec((None, t, d), row), *_wide_mod_specs(mod_l, b),
                  _full((1, d)), const(wg.shape), const(wu.shape), const(wd.shape)],
        out_specs=pl.BlockSpec((None, t, d), row),
        out_shape=jax.ShapeDtypeStruct(xs.shape, F32),
        compiler_params=_params(("arbitrary", "arbitrary")),
        name="ffn_dense",
    )(xs, mod_l, mod_l, g, wg, wu, wd)


def _store_token_major(ref, val):
    rows = val.shape[0]
    for c in range(SUBLANES):
        ref[pl.ds(c, rows, stride=SUBLANES), :] = val[:, c * LANES:(c + 1) * LANES]


def _load_token_major(ref, rows):
    return jnp.concatenate([ref[pl.ds(c, rows, stride=SUBLANES), :] for c in range(SUBLANES)], axis=1)


def _router_kernel(x_ref, modb_ref, modc_ref, g_ref, wr_ref, h_ref, sel_ref, *, d, s_lat):
    mod = _row_mod(modb_ref, modc_ref, x_ref.shape[0], s_lat, d)
    h = _adaln(x_ref[...], g_ref[...], mod(3), mod(4))
    _store_token_major(h_ref, h)
    logits = jnp.dot(h, wr_ref[...], preferred_element_type=F32, precision=lax.Precision.HIGHEST)
    lane = lax.broadcasted_iota(jnp.int32, logits.shape, 1)
    logits = jnp.where(lane < N_EXPERTS, logits, -jnp.inf)
    m1 = jnp.max(logits, axis=-1, keepdims=True)
    i1 = jnp.min(jnp.where(logits == m1, lane, LANES), axis=-1, keepdims=True)
    rest = jnp.where(lane == i1, -jnp.inf, logits)
    m2 = jnp.max(rest, axis=-1, keepdims=True)
    i2 = jnp.min(jnp.where(rest == m2, lane, LANES), axis=-1, keepdims=True)
    e2 = jnp.exp(m2 - m1)
    w1 = 1.0 / (1.0 + e2)
    sel_ref[...] = (jnp.where(lane == 0, w1, 0.0) + jnp.where(lane == 1, e2 * w1, 0.0)
                    + jnp.where(lane == 2, i1.astype(F32), 0.0) + jnp.where(lane == 3, i2.astype(F32), 0.0))


def _router(xs, mod_l, g, w_r, s_lat, rows):
    b, _, d = xs.shape
    t = _wide_tile(rows)
    row = lambda i, r: (i, r, 0)
    return pl.pallas_call(
        functools.partial(_router_kernel, d=d, s_lat=s_lat),
        grid=(b, rows // t),
        in_specs=[pl.BlockSpec((None, t, d), row), *_wide_mod_specs(mod_l, b),
                  _full((1, d)), _full(w_r.shape)],
        out_specs=[pl.BlockSpec((None, t * SUBLANES, LANES), row), pl.BlockSpec((None, t, LANES), row)],
        out_shape=[jax.ShapeDtypeStruct((b, rows * SUBLANES, LANES), F32),
                   jax.ShapeDtypeStruct((b, rows, LANES), F32)],
        compiler_params=_params(("arbitrary", "arbitrary")),
        name="moe_router",
    )(xs, mod_l, mod_l, g, w_r)


def _dispatch(sel, tm):
    m = sel.shape[0]
    i32 = jnp.int32
    experts = jnp.arange(N_EXPERTS, dtype=i32)

    def lookup(table, idx):
        return jnp.sum(jnp.where(idx[:, None] == experts[None, :], table[None, :], 0), axis=1)

    idx = sel[:, 2:4].astype(i32)
    e_flat = jnp.concatenate([idx[:, 0], idx[:, 1]])
    assignment = jnp.arange(2 * m, dtype=i32)
    _, order = lax.sort_key_val(e_flat, assignment)
    _, rank = lax.sort_key_val(order, assignment)
    counts = jnp.sum((e_flat[:, None] == experts[None, :]).astype(i32), axis=0)
    padded = (counts + tm - 1) // tm * tm
    pad_end = jnp.cumsum(padded)
    pad_start = pad_end - padded
    start = jnp.cumsum(counts) - counts
    n_tiles = 2 * m // tm + N_EXPERTS
    n_used = pad_end[-1] // tm
    tile = jnp.minimum(jnp.arange(n_tiles, dtype=i32), n_used - 1)
    tile_e = jnp.sum((tile[:, None] * tm >= pad_end[None, :]).astype(i32), axis=1)
    slot = jnp.arange(n_tiles * tm, dtype=i32)
    slot_e = jnp.sum((jnp.minimum(slot, pad_end[-1] - 1)[:, None] >= pad_end[None, :]).astype(i32), axis=1)
    pos = slot - lookup(pad_start, slot_e)
    valid = (pos < lookup(counts, slot_e)) & (slot < pad_end[-1])
    src = jnp.where(valid, order[jnp.clip(lookup(start, slot_e) + pos, 0, 2 * m - 1)] % m, 0)
    slot_of = lookup(pad_start - start, e_flat) + rank
    return (tile_e, n_used.reshape(1).astype(i32), (src * SUBLANES).astype(i32), (slot_of * SUBLANES).astype(i32))


def _experts_kernel(te_ref, nu_ref, src_ref, h_hbm, wg_ref, wu_ref, wd_ref, o_ref, xg_ref, xb_ref, acc_ref, sems,
                    *, n_tiles, n_steps):
    t, j = pl.program_id(0), pl.program_id(1)
    tm = xb_ref.shape[0]
    rows_per_step = tm // n_steps
    n_used = nu_ref[0]
    used = t < n_used
    buf = t % 2

    def start_row(tile, dst_buf, r):
        src = pl.multiple_of(src_ref[tile * tm + r], SUBLANES)
        dst = pl.multiple_of(r * SUBLANES, SUBLANES)
        pltpu.make_async_copy(h_hbm.at[pl.ds(src, SUBLANES)], xg_ref.at[dst_buf, pl.ds(dst, SUBLANES)],
                              sems.at[dst_buf]).start()

    def wait_rows(b_):
        pltpu.make_async_copy(h_hbm.at[pl.ds(0, tm * SUBLANES)], xg_ref.at[b_], sems.at[b_]).wait()

    @pl.when((t == 0) & (j == 0))
    def _():
        def issue(r, carry):
            start_row(0, 0, r)
            return carry

        lax.fori_loop(0, tm, issue, 0, unroll=8)

    @pl.when((j == 0) & ((t == 0) | (t - 1 < n_used)))
    def _():
        wait_rows(buf)
        xb_ref[...] = _load_token_major(xg_ref.at[buf], tm).astype(BF16)

    @pl.when(j == 0)
    def _():
        acc_ref[...] = jnp.zeros_like(acc_ref)

    @pl.when(used)
    def _():
        nxt = jnp.minimum(t + 1, n_tiles - 1)
        for r in range(rows_per_step):
            start_row(nxt, 1 - buf, j * rows_per_step + r)
        h = xb_ref[...]
        a = jnp.dot(h, wg_ref[...].astype(BF16), preferred_element_type=F32)
        u = jnp.dot(h, wu_ref[...].astype(BF16), preferred_element_type=F32)
        act = (_silu(a) * u).astype(BF16)
        acc_ref[...] += jnp.dot(act, wd_ref[...].astype(BF16), preferred_element_type=F32)

    @pl.when(j == n_steps - 1)
    def _():
        _store_token_major(o_ref, acc_ref[...])

    @pl.when(used & (t == n_tiles - 1) & (j == n_steps - 1))
    def _():
        wait_rows(1 - buf)


def _moe_ff_tile(dff):
    return MOE_FF_TILE if dff % MOE_FF_TILE == 0 else dff


def _moe_tile_rows(m, dff):
    per_expert = 2.0 * m / N_EXPERTS
    unit = (dff // _moe_ff_tile(dff)) * 2 * SUBLANES
    rows = (per_expert + 3.0 * math.sqrt(per_expert)) / MOE_TILES_PER_EXPERT
    return int(math.ceil(rows / unit)) * unit


def _experts(h2, tables, wg, wu, wd, layer, tm):
    d, dff = wg.shape[-2:]
    assert d == SUBLANES * LANES
    tile_e, n_used, src, _ = tables
    n_tiles = tile_e.shape[0]
    tf = MOE_FF_TILE if dff % MOE_FF_TILE == 0 else dff
    nj = dff // tf
    assert tm % nj == 0
    col = lambda t, j, te, nu, src: (layer, te[t], 0, jnp.where(t < nu[0], j, nj - 1))
    rowb = lambda t, j, te, nu, src: (layer, te[t], jnp.where(t < nu[0], j, nj - 1), 0)
    return pl.pallas_call(
        functools.partial(_experts_kernel, n_tiles=n_tiles, n_steps=nj),
        grid_spec=pltpu.PrefetchScalarGridSpec(
            num_scalar_prefetch=3,
            grid=(n_tiles, nj),
            in_specs=[pl.BlockSpec(memory_space=pl.ANY),
                      pl.BlockSpec((None, None, d, tf), col),
                      pl.BlockSpec((None, None, d, tf), col),
                      pl.BlockSpec((None, None, tf, d), rowb)],
            out_specs=pl.BlockSpec((tm * SUBLANES, LANES), lambda t, j, te, nu, src: (t, 0)),
            scratch_shapes=[pltpu.VMEM((2, tm * SUBLANES, LANES), F32), pltpu.VMEM((tm, d), BF16),
                            pltpu.VMEM((tm, d), F32), pltpu.SemaphoreType.DMA((2,))]),
        out_shape=jax.ShapeDtypeStruct((n_tiles * tm * SUBLANES, LANES), F32),
        compiler_params=_params(("arbitrary", "arbitrary")),
        name="moe_experts",
    )(tile_e, n_used, src, h2, wg, wu, wd)


def _combine_kernel(slot_ref, x_ref, sel_ref, mod_ref, g_ref, y_hbm, o_ref, r1_ref, r2_ref, sems,
                    *, d, m, rows_per_sample, final):
    t = x_ref.shape[0]
    base = pl.program_id(0) * rows_per_sample + pl.program_id(1) * t

    def issue(k, carry):
        dst = pl.ds(pl.multiple_of(k * SUBLANES, SUBLANES), SUBLANES)
        s1 = pl.multiple_of(slot_ref[base + k], SUBLANES)
        s2 = pl.multiple_of(slot_ref[m + base + k], SUBLANES)
        pltpu.make_async_copy(y_hbm.at[pl.ds(s1, SUBLANES)], r1_ref.at[dst], sems.at[0]).start()
        pltpu.make_async_copy(y_hbm.at[pl.ds(s2, SUBLANES)], r2_ref.at[dst], sems.at[1]).start()
        return carry

    lax.fori_loop(0, t, issue, 0, unroll=8)
    pltpu.make_async_copy(y_hbm.at[pl.ds(0, t * SUBLANES)], r1_ref, sems.at[0]).wait()
    pltpu.make_async_copy(y_hbm.at[pl.ds(0, t * SUBLANES)], r2_ref, sems.at[1]).wait()
    y = sel_ref[:, 0:1] * _load_token_major(r1_ref, t) + sel_ref[:, 1:2] * _load_token_major(r2_ref, t)
    x = x_ref[...] + mod_ref[:, 5 * d:6 * d] * y
    if final:
        x = x * lax.rsqrt(jnp.mean(x * x, axis=-1, keepdims=True) + EPS) * g_ref[...]
    o_ref[...] = x


def _combine(xs, sel, mod_l, y_sorted, slot_of, final_g, n_lat_tiles, n_row_tiles, final):
    b, _, d = xs.shape
    t = ROW_TILE
    rows = n_row_tiles * t
    row = lambda i, r, s: (i, r, 0)
    return pl.pallas_call(
        functools.partial(_combine_kernel, d=d, m=b * rows, rows_per_sample=rows, final=final),
        grid_spec=pltpu.PrefetchScalarGridSpec(
            num_scalar_prefetch=1,
            grid=(b, n_row_tiles),
            in_specs=[pl.BlockSpec((None, t, d), row),
                      pl.BlockSpec((None, t, LANES), row),
                      pl.BlockSpec((None, 1, mod_l.shape[-1]),
                                   lambda i, r, s: (jnp.where(r < n_lat_tiles, i, b), 0, 0)),
                      pl.BlockSpec((1, d), lambda i, r, s: (0, 0)),
                      pl.BlockSpec(memory_space=pl.ANY)],
            out_specs=pl.BlockSpec((None, t, d), row),
            scratch_shapes=[pltpu.VMEM((t * SUBLANES, LANES), F32), pltpu.VMEM((t * SUBLANES, LANES), F32),
                            pltpu.SemaphoreType.DMA((2,))]),
        out_shape=jax.ShapeDtypeStruct((b, rows, d), F32),
        compiler_params=_params(("arbitrary", "arbitrary")),
        name="moe_combine",
    )(slot_of, xs, sel, mod_l, final_g, y_sorted)


def _rope_tables(s_lat, n_ctx):
    rows = s_lat // GRID_W
    pos_row = jnp.repeat(jnp.arange(rows, dtype=jnp.int32), GRID_W).astype(F32)
    pos_col = jnp.tile(jnp.arange(GRID_W, dtype=jnp.int32), rows).astype(F32)
    quarter = HEAD_DIM // 4
    inv_freq = ROPE_THETA ** (-jnp.arange(quarter, dtype=F32) / quarter)
    ang_r = pos_row[:, None] * inv_freq
    ang_c = pos_col[:, None] * inv_freq
    ang = jnp.concatenate([ang_r, ang_r, ang_c, ang_c], axis=-1)
    cos, sin = jnp.cos(ang), jnp.sin(ang)
    first = (np.arange(HEAD_DIM) % (2 * quarter)) < quarter
    sin_lo = jnp.where(first, -sin, 0.0)
    sin_hi = jnp.where(first, 0.0, sin)

    def widen(tab, ctx_fill):
        tab = jnp.concatenate([tab, jnp.full((n_ctx, HEAD_DIM), ctx_fill, F32)], axis=0)
        return jnp.tile(tab, (1, LANES // HEAD_DIM))

    return widen(cos, 1.0), widen(sin_lo, 0.0), widen(sin_hi, 0.0)


def _dft_tables(n):
    k = np.arange(n, dtype=np.int64)
    ang = 2.0 * np.pi * ((k[:, None] * k[None, :]) % n).astype(np.float64) / n
    return np.cos(ang), np.sin(ang)


def _constants(s_lat, n_ctx):
    c64, s64 = _dft_tables(F_GROUP_DIM)
    eye = np.eye(F_GROUPS)
    cn, sn = _dft_tables(s_lat)
    cc, sc = _dft_tables(n_ctx)
    bf = lambda a: jnp.asarray(a, dtype=F32).astype(BF16)
    return dict(
        gmat=bf(np.kron(np.eye(512 // HEAD_DIM), np.ones((HEAD_DIM, HEAD_DIM)))),
        c64=bf(np.kron(eye, c64) / math.sqrt(F_GROUP_DIM)),
        s64=bf(np.kron(eye, s64) / math.sqrt(F_GROUP_DIM)),
        cn=bf(cn / math.sqrt(s_lat)), sn=bf(sn / math.sqrt(s_lat)),
        cc=bf(cc / math.sqrt(n_ctx)), sc=bf(sc / math.sqrt(n_ctx)),
    )


def _dup_heads(w, off, n_heads):
    cols = []
    for hd in range(n_heads):
        blk = w[:, off + hd * HEAD_DIM: off + (hd + 1) * HEAD_DIM]
        cols += [blk, blk]
    return cols


def _relayout_w_in(w):
    cols = [w[:, _OFF["gt"]:_OFF["end"]], w[:, _OFF["aq"]:_OFF["bq"]], w[:, _OFF["bq"]:_OFF["bk"]],
            w[:, _OFF["cq"]:_OFF["ck"]], w[:, _OFF["fz"]:_OFF["gt"]]]
    cols += _dup_heads(w, _OFF["bk"], B_KV) + _dup_heads(w, _OFF["bv"], B_KV)
    cols += _dup_heads(w, _OFF["ck"], C_KV) + _dup_heads(w, _OFF["cv"], C_KV)
    return jnp.concatenate(cols, axis=1).astype(BF16)


def kernel(x, c, ctx, c_ctx, w_mod, b_mod, norm1_g, norm2_g, w_in, diff_lambda, diff_subln_g, qk_norm_g,
           sink_logits, w_branch, w_out, ffn_w_gate, ffn_w_up, ffn_w_down, router_w, moe_w_gate, moe_w_up,
           moe_w_down, final_g):
    b, s_lat, d = x.shape
    n_ctx = ctx.shape[1]
    depth = w_mod.shape[0]
    nt = s_lat + n_ctx
    t = ROW_TILE
    assert s_lat % t == 0 and n_ctx % t == 0 and s_lat % n_ctx == 0 and b < MOD_ROWS
    assert w_in.shape[-1] == _OFF["end"] and d * N_BRANCH == _OFF["end"] - _OFF["gt"]
    n_lat_tiles = s_lat // t

    xs = jnp.concatenate([x, ctx], axis=1)
    cvec = jnp.concatenate([c, c_ctx[None], jnp.zeros((MOD_ROWS - b - 1, d), F32)], axis=0)
    mod = _modulation(cvec, w_mod, b_mod).reshape(depth, MOD_ROWS, 1, 6 * d)
    tabs = _rope_tables(s_lat, n_ctx)
    consts = _constants(s_lat, n_ctx)

    out = None
    for l in range(depth):
        last = l == depth - 1
        mod_l = mod[l]
        p = _inproj(xs, mod_l, norm1_g[l][None], _relayout_w_in(w_in[l]), tabs,
                    jnp.tile(qk_norm_g[l, 0], 512 // HEAD_DIM)[None], jnp.tile(qk_norm_g[l, 1], 256 // HEAD_DIM)[None],
                    consts, n_lat_tiles)
        lam_init = 0.8 - 0.6 * math.exp(-0.3 * l)
        n_q_tiles = n_lat_tiles if last else nt // t
        oa, ob, oc = _attention(p, diff_lambda[l], diff_subln_g[l][None], sink_logits[l],
                                n_lat_tiles, n_q_tiles, s_lat, lam_init)
        od = _fourier(p[10], p[11], consts, n_lat_tiles, n_q_tiles, s_lat)
        xs = _merge(xs, mod_l, p[0], (oa, ob, oc, od), w_branch[l].astype(BF16), w_out[l].astype(BF16),
                    s_lat, n_q_tiles * t)

        i = l // 2
        if l % 2 == 0:
            assert not last
            xs = _ffn(xs, mod_l, norm2_g[l][None], ffn_w_gate[i].astype(BF16), ffn_w_up[i].astype(BF16),
                      ffn_w_down[i].astype(BF16), s_lat)
        else:
            w_r = jnp.pad(router_w[i], ((0, 0), (0, LANES - N_EXPERTS)))
            n_row_tiles = n_q_tiles
            m = b * n_row_tiles * t
            h2, sel = _router(xs, mod_l, norm2_g[l][None], w_r, s_lat, n_row_tiles * t)
            tm = _moe_tile_rows(m, moe_w_gate.shape[-1])
            tables = _dispatch(sel.reshape(m, LANES), tm)
            y_sorted = _experts(h2.reshape(m * SUBLANES, LANES), tables, moe_w_gate, moe_w_up, moe_w_down, i, tm)
            res = _combine(xs, sel, mod_l, y_sorted, tables[3], final_g[None], n_lat_tiles, n_row_tiles, last)
            if last:
                out = res
            else:
                xs = res
    if out is None:
        raise NotImplementedError("final normalisation is fused into the last (mixture-of-experts) layer")
    return out
```

```python
import functools
import math

import numpy as np
import jax
import jax.numpy as jnp
from jax import lax
from jax.experimental import pallas as pl
from jax.experimental.pallas import tpu as pltpu

F32 = jnp.float32
BF16 = jnp.bfloat16

GRID_W = 64
HEAD_DIM = 64
ROPE_THETA = 10000.0
WINDOW = 128
A_HEADS = 4
B_HEADS = 8
B_KV = 2
C_HEADS = 8
C_KV = 2
F_GROUPS = 8
F_GROUP_DIM = 64
BRANCH_W = 512
N_BRANCH = 4
N_EXPERTS = 8
EPS = 1e-6
SUBLN_EPS = 1e-5
NEG_INF = -1e30
LOG2E = math.log2(math.e)

LANES = 128
SUBLANES = 8
ROW_TILE = 256
WIDE_TILES = (768, 512, 256)
FFN_CHUNK = 768
MOE_TILES_PER_EXPERT = 4
MOE_FF_TILE = 512
MOD_ROWS = 16
VMEM_LIMIT = 56 * 1024 * 1024

_OFF = dict(aq=0, ak=512, av=1024, bq=1536, bk=2048, bv=2176, cq=2304, ck=2816, cv=2944,
            fz=3072, gt=3584, end=7680)
_P = dict(aq=0, ak=512, av=1024, bq=1536, cq=2048, fz=2560, bk=3072, bv=3328, ck=3584, cv=3840, end=4096)


def _params(sem):
    return pltpu.CompilerParams(dimension_semantics=sem, vmem_limit_bytes=VMEM_LIMIT)


def _full(shape):
    n = len(shape)
    return pl.BlockSpec(shape, lambda *_: (0,) * n)


def _silu(x):
    return x / (1.0 + jnp.exp(-x))


def _sigmoid(x):
    return 1.0 / (1.0 + jnp.exp(-x))


def _adaln(x, g, shift, scale):
    y = x * lax.rsqrt(jnp.mean(x * x, axis=-1, keepdims=True) + EPS) * g
    return y * (1.0 + scale) + shift


def _mod_kernel(c_ref, w_ref, b_ref, o_ref):
    s = _silu(c_ref[...]).astype(BF16)
    o_ref[...] = jnp.dot(s, w_ref[...].astype(BF16), preferred_element_type=F32) + b_ref[...]


def _modulation(cvec, w_mod, b_mod):
    depth, d, n = w_mod.shape
    tn = 1536 if n % 1536 == 0 else n
    return pl.pallas_call(
        _mod_kernel,
        grid=(depth, n // tn),
        in_specs=[_full((MOD_ROWS, d)),
                  pl.BlockSpec((None, d, tn), lambda l, j: (l, 0, j)),
                  pl.BlockSpec((None, 1, tn), lambda l, j: (l, 0, j))],
        out_specs=pl.BlockSpec((None, MOD_ROWS, tn), lambda l, j: (l, 0, j)),
        out_shape=jax.ShapeDtypeStruct((depth, MOD_ROWS, n), F32),
        compiler_params=_params(("arbitrary", "arbitrary")),
        name="modulation",
    )(cvec, w_mod, b_mod.reshape(depth, 1, n))


def _rope(p, cos, sin_lo, sin_hi):
    outs = []
    for c in range(p.shape[1] // LANES):
        xc = p[:, c * LANES:(c + 1) * LANES]
        outs.append(xc * cos + pltpu.roll(xc, LANES - 16, 1) * sin_lo + pltpu.roll(xc, 16, 1) * sin_hi)
    return outs[0] if len(outs) == 1 else jnp.concatenate(outs, axis=1)


def _group_rmsnorm(y, gmat, g):
    ss = jnp.dot((y * y).astype(BF16), gmat, preferred_element_type=F32)
    return y * lax.rsqrt(ss * (1.0 / HEAD_DIM) + EPS) * g


def _inproj_kernel(x_ref, modb_ref, modc_ref, g_ref, w_ref, cos_ref, slo_ref, shi_ref, qg_ref, kg_ref, gmat_ref,
                   c64_ref, s64_ref,
                   aq_ref, akt_ref, av_ref, bq_ref, bkt_ref, bv_ref, cq_ref, ckt_ref, cv_ref,
                   fzc_ref, fzs_ref, *, d, s_lat):
    mod = _row_mod(modb_ref, modc_ref, x_ref.shape[0], s_lat, d)
    h = _adaln(x_ref[...], g_ref[...], mod(0), mod(1)).astype(BF16)
    cos, slo, shi = cos_ref[...], slo_ref[...], shi_ref[...]
    scale = HEAD_DIM ** -0.5 * LOG2E

    def proj(name, width):
        return jnp.dot(h, w_ref[:, _P[name]:_P[name] + width], preferred_element_type=F32)

    aq_ref[...] = (_rope(proj("aq", 512), cos, slo, shi) * scale).astype(BF16)
    akt_ref[...] = _rope(proj("ak", 512), cos, slo, shi).T.astype(BF16)
    av_ref[...] = proj("av", 512).astype(BF16)

    bq = _group_rmsnorm(proj("bq", 512), gmat_ref[...], qg_ref[...])
    bq_ref[...] = (_rope(bq, cos, slo, shi) * scale).astype(BF16)
    bk = _group_rmsnorm(proj("bk", 256), gmat_ref[0:256, 0:256], kg_ref[...])
    bkt_ref[...] = _rope(bk, cos, slo, shi).T.astype(BF16)
    bv_ref[...] = proj("bv", 256).astype(BF16)

    cq_ref[...] = (_rope(proj("cq", 512), cos, slo, shi) * scale).astype(BF16)
    ckt_ref[...] = _rope(proj("ck", 256), cos, slo, shi).T.astype(BF16)
    cv_ref[...] = proj("cv", 256).astype(BF16)

    fz = proj("fz", 512).astype(BF16)
    fzc_ref[...] = jnp.dot(fz, c64_ref[...], preferred_element_type=F32).astype(BF16)
    fzs_ref[...] = jnp.dot(fz, s64_ref[...], preferred_element_type=F32).astype(BF16)


def _inproj(xs, mod_l, g, w_p, tabs, qg, kg, consts, s_lat):
    b, nt, d = xs.shape
    t = _wide_tile(nt)
    row = lambda i, r: (i, r, 0)
    tr = lambda i, r: (i, 0, r)
    tab = pl.BlockSpec((t, LANES), lambda i, r: (r, 0))

    def out(width):
        return pl.BlockSpec((None, t, width), row), jax.ShapeDtypeStruct((b, nt, width), BF16)

    def out_t(width):
        return pl.BlockSpec((None, width, t), tr), jax.ShapeDtypeStruct((b, width, nt), BF16)

    outs = [out(512), out_t(512), out(512), out(512), out_t(256), out(256),
            out(512), out_t(256), out(256), out(512), out(512)]
    return pl.pallas_call(
        functools.partial(_inproj_kernel, d=d, s_lat=s_lat),
        grid=(b, nt // t),
        in_specs=[pl.BlockSpec((None, t, d), row), *_wide_mod_specs(mod_l, b),
                  _full((1, d)),
                  pl.BlockSpec(w_p.shape, lambda i, r: (0, 0), pipeline_mode=pl.Buffered(1)),
                  tab, tab, tab,
                  _full((1, 512)), _full((1, 256)), _full((512, 512)), _full((512, 512)), _full((512, 512))],
        out_specs=[o[0] for o in outs],
        out_shape=[o[1] for o in outs],
        compiler_params=_params(("arbitrary", "arbitrary")),
        name="inproj",
    )(xs, mod_l, mod_l, g, w_p, tabs[0], tabs[1], tabs[2], qg, kg, consts["gmat"], consts["c64"], consts["s64"])


def _half_masks(shape):
    lane = lax.broadcasted_iota(jnp.int32, shape, 1)
    return lane < HEAD_DIM, lane >= HEAD_DIM


def _exp2_bf16(s):
    return jnp.exp2(s - jnp.max(s, axis=-1, keepdims=True)).astype(BF16)


def _pipelined_scores(score_fns):
    nxt = score_fns[0]()
    for i in range(len(score_fns)):
        cur = nxt
        if i + 1 < len(score_fns):
            nxt = score_fns[i + 1]()
        yield cur


def _attn_a_kernel(q_ref, kt_ref, v_ref, lp_ref, g_ref, o_ref, *, n_lat_tiles, s_lat, lam_init):
    r = pl.program_id(1)
    lp = lp_ref[...]
    lam = (jnp.exp(jnp.sum(lp[0:1] * lp[1:2], axis=-1, keepdims=True))
           - jnp.exp(jnp.sum(lp[2:3] * lp[3:4], axis=-1, keepdims=True)) + lam_init)
    g = g_ref[...] * (1.0 - lam_init)
    nt = kt_ref.shape[1]

    def body(k0):
        lo, hi = _half_masks((q_ref.shape[0], LANES))
        ones = jnp.ones((nt - k0, LANES), BF16)

        def score_fn(hd, msk):
            cs = slice(hd * LANES, (hd + 1) * LANES)
            return lambda: jnp.dot(jnp.where(msk, q_ref[:, cs], 0), kt_ref[cs, k0:nt], preferred_element_type=F32)

        fns = [score_fn(hd, msk) for hd in range(A_HEADS) for msk in (lo, hi)]
        outs = []
        for i, sc in enumerate(_pipelined_scores(fns)):
            hd = i // 2
            cs = slice(hd * LANES, (hd + 1) * LANES)
            vo = jnp.concatenate([v_ref[k0:nt, cs], ones], axis=1)
            full = jnp.dot(_exp2_bf16(sc), vo, preferred_element_type=F32)
            outs.append(full[:, 0:LANES] * (1.0 / full[:, LANES:LANES + 1]))
            if len(outs) == 2:
                o = outs[0] - lam * outs[1]
                outs = []
                o = o * lax.rsqrt(jnp.mean(o * o, axis=-1, keepdims=True) + SUBLN_EPS) * g
                o_ref[:, cs] = o.astype(BF16)

    pl.when(r < n_lat_tiles)(lambda: body(0))
    pl.when(r >= n_lat_tiles)(lambda: body(s_lat))


def _attn_b_kernel(q_ref, kt_ref, v_ref, o_ref, *, n_lat_tiles, s_lat):
    r = pl.program_id(1)
    nt = kt_ref.shape[1]

    def body(k0):
        lo, hi = _half_masks((q_ref.shape[0], LANES))
        ones = jnp.ones((nt - k0, LANES), BF16)

        def score_fn(pair, msk):
            kv = pair // (B_HEADS // B_KV // 2)
            q = q_ref[:, pair * LANES:(pair + 1) * LANES]
            return lambda: jnp.dot(jnp.where(msk, q, 0), kt_ref[kv * LANES:(kv + 1) * LANES, k0:nt],
                                   preferred_element_type=F32)

        fns = [score_fn(pair, msk) for pair in range(B_HEADS // 2) for msk in (lo, hi)]
        halves = []
        for i, sc in enumerate(_pipelined_scores(fns)):
            pair = i // 2
            kv = pair // (B_HEADS // B_KV // 2)
            vo = jnp.concatenate([v_ref[k0:nt, kv * LANES:(kv + 1) * LANES], ones], axis=1)
            full = jnp.dot(_exp2_bf16(sc), vo, preferred_element_type=F32)
            halves.append(full[:, 0:LANES] * (1.0 / full[:, LANES:LANES + 1]))
            if len(halves) == 2:
                o_ref[:, pair * LANES:(pair + 1) * LANES] = jnp.where(lo, halves[0], halves[1]).astype(BF16)
                halves = []

    pl.when(r < n_lat_tiles)(lambda: body(0))
    pl.when(r >= n_lat_tiles)(lambda: body(s_lat))


def _attn_c_kernel(sink_ref, q_ref, ktl_ref, ktm_ref, ktr_ref, ktc_ref, vl_ref, vm_ref, vr_ref, vc_ref, o_ref,
                   *, n_lat_tiles, s_lat):
    r = pl.program_id(1)
    t = q_ref.shape[0]
    n_band = t + 2 * WINDOW
    n_keys = n_band + ktc_ref.shape[1]
    u = lax.broadcasted_iota(jnp.int32, (t, n_keys), 1)
    row = lax.broadcasted_iota(jnp.int32, (t, n_keys), 0)
    j = r * t - WINDOW + u
    in_band = (u >= row) & (u <= row + 2 * WINDOW) & (j >= 0) & (j < s_lat) & (r < n_lat_tiles)
    valid = in_band | (u >= n_band)
    lo, hi = _half_masks((t, LANES))
    ones = jnp.ones((n_keys, LANES), BF16)
    kts, vos = [], []
    for kv in range(C_KV):
        ks = slice(kv * LANES, (kv + 1) * LANES)
        kts.append(jnp.concatenate([ktl_ref[ks, :], ktm_ref[ks, :], ktr_ref[ks, :], ktc_ref[ks, :]], axis=1))
        v = jnp.concatenate([vl_ref[:, ks], vm_ref[:, ks], vr_ref[:, ks], vc_ref[:, ks]], axis=0)
        vos.append(jnp.concatenate([v, ones], axis=1))

    def score_fn(pair, msk):
        kv = pair // (C_HEADS // C_KV // 2)
        q = q_ref[:, pair * LANES:(pair + 1) * LANES]
        return lambda: jnp.dot(jnp.where(msk, q, 0), kts[kv], preferred_element_type=F32)

    fns = [score_fn(pair, msk) for pair in range(C_HEADS // 2) for msk in (lo, hi)]
    halves = []
    for i, sc in enumerate(_pipelined_scores(fns)):
        pair = i // 2
        kv = pair // (C_HEADS // C_KV // 2)
        sink = sink_ref[i] * LOG2E
        sc = jnp.where(valid, sc, NEG_INF)
        m = jnp.maximum(jnp.max(sc, axis=-1, keepdims=True), sink)
        full = jnp.dot(jnp.exp2(sc - m).astype(BF16), vos[kv], preferred_element_type=F32)
        l = full[:, LANES:LANES + 1] + jnp.exp2(sink - m)
        halves.append(full[:, 0:LANES] * (1.0 / l))
        if len(halves) == 2:
            o_ref[:, pair * LANES:(pair + 1) * LANES] = jnp.where(lo, halves[0], halves[1]).astype(BF16)
            halves = []


def _attention(p, lp, subln_g, sink, n_lat_tiles, n_q_tiles, s_lat, lam_init):
    (_, aq, akt, av, bq, bkt, bv, cq, ckt, cv, _, _) = p
    b, nt, _ = aq.shape
    t = ROW_TILE
    grid = (b, n_q_tiles)
    row = lambda i, r: (i, r, 0)
    whole = lambda i, r: (i, 0, 0)
    sem = _params(("arbitrary", "arbitrary"))
    o_spec = pl.BlockSpec((None, t, 512), row)
    o_shape = jax.ShapeDtypeStruct((b, n_q_tiles * t, 512), BF16)

    oa = pl.pallas_call(
        functools.partial(_attn_a_kernel, n_lat_tiles=n_lat_tiles, s_lat=s_lat, lam_init=lam_init),
        grid=grid,
        in_specs=[pl.BlockSpec((None, t, 512), row), pl.BlockSpec((None, 512, nt), whole),
                  pl.BlockSpec((None, nt, 512), whole), _full((4, HEAD_DIM)), _full((1, 2 * HEAD_DIM))],
        out_specs=o_spec, out_shape=o_shape, compiler_params=sem, name="attn_diff",
    )(aq, akt, av, lp, subln_g)

    ob = pl.pallas_call(
        functools.partial(_attn_b_kernel, n_lat_tiles=n_lat_tiles, s_lat=s_lat),
        grid=grid,
        in_specs=[pl.BlockSpec((None, t, 512), row), pl.BlockSpec((None, 256, nt), whole),
                  pl.BlockSpec((None, nt, 256), whole)],
        out_specs=o_spec, out_shape=o_shape, compiler_params=sem, name="attn_gqa",
    )(bq, bkt, bv)

    wpt = t // WINDOW
    last = s_lat // WINDOW - 1
    left = lambda r: jnp.clip(r * wpt - 1, 0, last)
    mid = lambda r: jnp.minimum(r, n_lat_tiles - 1)
    right = lambda r: jnp.clip((r + 1) * wpt, 0, last)
    n_ctx = nt - s_lat
    ctx_blk = s_lat // n_ctx
    oc = pl.pallas_call(
        functools.partial(_attn_c_kernel, n_lat_tiles=n_lat_tiles, s_lat=s_lat),
        grid=grid,
        in_specs=[pl.BlockSpec(memory_space=pltpu.SMEM),
                  pl.BlockSpec((None, t, 512), row),
                  pl.BlockSpec((None, 256, WINDOW), lambda i, r: (i, 0, left(r))),
                  pl.BlockSpec((None, 256, t), lambda i, r: (i, 0, mid(r))),
                  pl.BlockSpec((None, 256, WINDOW), lambda i, r: (i, 0, right(r))),
                  pl.BlockSpec((None, 256, n_ctx), lambda i, r: (i, 0, ctx_blk)),
                  pl.BlockSpec((None, WINDOW, 256), lambda i, r: (i, left(r), 0)),
                  pl.BlockSpec((None, t, 256), lambda i, r: (i, mid(r), 0)),
                  pl.BlockSpec((None, WINDOW, 256), lambda i, r: (i, right(r), 0)),
                  pl.BlockSpec((None, n_ctx, 256), lambda i, r: (i, ctx_blk, 0))],
        out_specs=o_spec, out_shape=o_shape, compiler_params=sem, name="attn_window",
    )(sink, cq, ckt, ckt, ckt, ckt, cv, cv, cv, cv)
    return oa, ob, oc


def _fourier_kernel(cn_ref, sn_ref, cc_ref, sc_ref, zc_ref, zs_ref, o_ref, *, n_lat_tiles, s_lat):
    r = pl.program_id(1)
    nt = zc_ref.shape[0]

    @pl.when(r < n_lat_tiles)
    def _():
        y = (jnp.dot(cn_ref[...], zc_ref[0:s_lat, :], preferred_element_type=F32)
             - jnp.dot(sn_ref[...], zs_ref[0:s_lat, :], preferred_element_type=F32))
        o_ref[...] = y.astype(BF16)

    @pl.when(r >= n_lat_tiles)
    def _():
        y = (jnp.dot(cc_ref[...], zc_ref[s_lat:nt, :], preferred_element_type=F32)
             - jnp.dot(sc_ref[...], zs_ref[s_lat:nt, :], preferred_element_type=F32))
        o_ref[...] = y.astype(BF16)


def _fourier(fzc, fzs, consts, n_lat_tiles, n_q_tiles, s_lat):
    b, nt, w = fzc.shape
    t = ROW_TILE
    n_ctx = nt - s_lat
    lat_rows = lambda i, r: (jnp.minimum(r, n_lat_tiles - 1), 0)
    ctx_rows = lambda i, r: (jnp.maximum(r - n_lat_tiles, 0), 0)
    whole = lambda i, r: (i, 0, 0)
    return pl.pallas_call(
        functools.partial(_fourier_kernel, n_lat_tiles=n_lat_tiles, s_lat=s_lat),
        grid=(b, n_q_tiles),
        in_specs=[pl.BlockSpec((t, s_lat), lat_rows), pl.BlockSpec((t, s_lat), lat_rows),
                  pl.BlockSpec((t, n_ctx), ctx_rows), pl.BlockSpec((t, n_ctx), ctx_rows),
                  pl.BlockSpec((None, nt, w), whole), pl.BlockSpec((None, nt, w), whole)],
        out_specs=pl.BlockSpec((None, t, w), lambda i, r: (i, r, 0)),
        out_shape=jax.ShapeDtypeStruct((b, n_q_tiles * t, w), BF16),
        compiler_params=_params(("arbitrary", "arbitrary")),
        name="fourier",
    )(consts["cn"], consts["sn"], consts["cc"], consts["sc"], fzc, fzs)


def _wide_tile(rows):
    for t in WIDE_TILES:
        if rows % t == 0:
            return t
    raise ValueError(rows)


def _wide_mod_specs(mod_l, b):
    n = mod_l.shape[-1]
    return [pl.BlockSpec((None, 1, n), lambda i, r: (i, 0, 0)), pl.BlockSpec((None, 1, n), lambda i, r: (b, 0, 0))]


def _row_mod(modb_ref, modc_ref, rows, s_lat, d):
    is_lat = pl.program_id(1) * rows + lax.broadcasted_iota(jnp.int32, (rows, 1), 0) < s_lat
    return lambda k: jnp.where(is_lat, modb_ref[:, k * d:(k + 1) * d], modc_ref[:, k * d:(k + 1) * d])


def _merge_kernel(x_ref, modb_ref, modc_ref, g_ref, wgt_ref, oa_ref, ob_ref, oc_ref, od_ref, wbr_ref, wo_ref,
                  o_ref, *, d, s_lat):
    mod = _row_mod(modb_ref, modc_ref, x_ref.shape[0], s_lat, d)
    x = x_ref[...]
    h = _adaln(x, g_ref[...], mod(0), mod(1)).astype(BF16)
    merged = None
    for i, br in enumerate((oa_ref, ob_ref, oc_ref, od_ref)):
        gate = _sigmoid(jnp.dot(h, wgt_ref[:, i * d:(i + 1) * d], preferred_element_type=F32))
        y = gate * jnp.dot(br[...], wbr_ref[i], preferred_element_type=F32)
        merged = y if merged is None else merged + y
    out = jnp.dot(merged.astype(BF16), wo_ref[...], preferred_element_type=F32)
    o_ref[...] = x + mod(2) * out


def _merge(xs, mod_l, g, w_gates, branches, w_br, w_o, s_lat, rows):
    b, _, d = xs.shape
    t = _wide_tile(rows)
    row = lambda i, r: (i, r, 0)
    br_spec = pl.BlockSpec((None, t, BRANCH_W), row)
    const = lambda shape: pl.BlockSpec(shape, lambda i, r: (0,) * len(shape), pipeline_mode=pl.Buffered(1))
    return pl.pallas_call(
        functools.partial(_merge_kernel, d=d, s_lat=s_lat),
        grid=(b, rows // t),
        in_specs=[pl.BlockSpec((None, t, d), row), *_wide_mod_specs(mod_l, b),
                  _full((1, d)), const(w_gates.shape),
                  br_spec, br_spec, br_spec, br_spec,
                  const(w_br.shape), const(w_o.shape)],
        out_specs=pl.BlockSpec((None, t, d), row),
        out_shape=jax.ShapeDtypeStruct((b, rows, d), F32),
        compiler_params=_params(("arbitrary", "arbitrary")),
        name="merge",
    )(xs, mod_l, mod_l, g, w_gates, *branches, w_br, w_o)


def _ffn_kernel(x_ref, modb_ref, modc_ref, g_ref, wg_ref, wu_ref, wd_ref, o_ref, *, d, s_lat):
    mod = _row_mod(modb_ref, modc_ref, x_ref.shape[0], s_lat, d)
    x = x_ref[...]
    h = _adaln(x, g_ref[...], mod(3), mod(4)).astype(BF16)
    dff = wg_ref.shape[1]
    y = None
    for c0 in range(0, dff, FFN_CHUNK):
        c1 = min(c0 + FFN_CHUNK, dff)
        a = jnp.dot(h, wg_ref[:, c0:c1], preferred_element_type=F32)
        u = jnp.dot(h, wu_ref[:, c0:c1], preferred_element_type=F32)
        part = jnp.dot((_silu(a) * u).astype(BF16), wd_ref[c0:c1, :], preferred_element_type=F32)
        y = part if y is None else y + part
    o_ref[...] = x + mod(5) * y


def _ffn(xs, mod_l, g, wg, wu, wd, s_lat):
    b, nt, d = xs.shape
    t = _wide_tile(nt)
    row = lambda i, r: (i, r, 0)
    const = lambda shape: pl.BlockSpec(shape, lambda i, r: (0, 0), pipeline_mode=pl.Buffered(1))
    return pl.pallas_call(
        functools.partial(_ffn_kernel, d=d, s_lat=s_lat),
        grid=(b, nt // t),
        in_specs=[pl.BlockSpec((None, t, d), row), *_wide_mod_specs(mod_l, b),
                  _full((1, d)), const(wg.shape), const(wu.shape), const(wd.shape)],
        out_specs=pl.BlockSpec((None, t, d), row),
        out_shape=jax.ShapeDtypeStruct(xs.shape, F32),
        compiler_params=_params(("arbitrary", "arbitrary")),
        name="ffn_dense",
    )(xs, mod_l, mod_l, g, wg, wu, wd)


def _store_token_major(ref, val):
    rows = val.shape[0]
    for c in range(SUBLANES):
        ref[pl.ds(c, rows, stride=SUBLANES), :] = val[:, c * LANES:(c + 1) * LANES]


def _load_token_major(ref, rows):
    return jnp.concatenate([ref[pl.ds(c, rows, stride=SUBLANES), :] for c in range(SUBLANES)], axis=1)


def _router_kernel(x_ref, modb_ref, modc_ref, g_ref, wr_ref, h_ref, sel_ref, *, d, s_lat):
    mod = _row_mod(modb_ref, modc_ref, x_ref.shape[0], s_lat, d)
    h = _adaln(x_ref[...], g_ref[...], mod(3), mod(4))
    _store_token_major(h_ref, h)
    logits = jnp.dot(h, wr_ref[...], preferred_element_type=F32, precision=lax.Precision.HIGHEST)
    lane = lax.broadcasted_iota(jnp.int32, logits.shape, 1)
    logits = jnp.where(lane < N_EXPERTS, logits, -jnp.inf)
    m1 = jnp.max(logits, axis=-1, keepdims=True)
    i1 = jnp.min(jnp.where(logits == m1, lane, LANES), axis=-1, keepdims=True)
    rest = jnp.where(lane == i1, -jnp.inf, logits)
    m2 = jnp.max(rest, axis=-1, keepdims=True)
    i2 = jnp.min(jnp.where(rest == m2, lane, LANES), axis=-1, keepdims=True)
    e2 = jnp.exp(m2 - m1)
    w1 = 1.0 / (1.0 + e2)
    sel_ref[...] = (jnp.where(lane == 0, w1, 0.0) + jnp.where(lane == 1, e2 * w1, 0.0)
                    + jnp.where(lane == 2, i1.astype(F32), 0.0) + jnp.where(lane == 3, i2.astype(F32), 0.0))


def _router(xs, mod_l, g, w_r, s_lat, rows):
    b, _, d = xs.shape
    t = _wide_tile(rows)
    row = lambda i, r: (i, r, 0)
    return pl.pallas_call(
        functools.partial(_router_kernel, d=d, s_lat=s_lat),
        grid=(b, rows // t),
        in_specs=[pl.BlockSpec((None, t, d), row), *_wide_mod_specs(mod_l, b),
                  _full((1, d)), _full(w_r.shape)],
        out_specs=[pl.BlockSpec((None, t * SUBLANES, LANES), row), pl.BlockSpec((None, t, LANES), row)],
        out_shape=[jax.ShapeDtypeStruct((b, rows * SUBLANES, LANES), F32),
                   jax.ShapeDtypeStruct((b, rows, LANES), F32)],
        compiler_params=_params(("arbitrary", "arbitrary")),
        name="moe_router",
    )(xs, mod_l, mod_l, g, w_r)


def _dispatch(sel, tm):
    m = sel.shape[0]
    i32 = jnp.int32
    experts = jnp.arange(N_EXPERTS, dtype=i32)

    def lookup(table, idx):
        return jnp.sum(jnp.where(idx[:, None] == experts[None, :], table[None, :], 0), axis=1)

    idx = sel[:, 2:4].astype(i32)
    e_flat = jnp.concatenate([idx[:, 0], idx[:, 1]])
    assignment = jnp.arange(2 * m, dtype=i32)
    _, order = lax.sort_key_val(e_flat, assignment)
    _, rank = lax.sort_key_val(order, assignment)
    counts = jnp.sum((e_flat[:, None] == experts[None, :]).astype(i32), axis=0)
    padded = (counts + tm - 1) // tm * tm
    pad_end = jnp.cumsum(padded)
    pad_start = pad_end - padded
    start = jnp.cumsum(counts) - counts
    n_tiles = 2 * m // tm + N_EXPERTS
    n_used = pad_end[-1] // tm
    tile = jnp.minimum(jnp.arange(n_tiles, dtype=i32), n_used - 1)
    tile_e = jnp.sum((tile[:, None] * tm >= pad_end[None, :]).astype(i32), axis=1)
    slot = jnp.arange(n_tiles * tm, dtype=i32)
    slot_e = jnp.sum((jnp.minimum(slot, pad_end[-1] - 1)[:, None] >= pad_end[None, :]).astype(i32), axis=1)
    pos = slot - lookup(pad_start, slot_e)
    valid = (pos < lookup(counts, slot_e)) & (slot < pad_end[-1])
    src = jnp.where(valid, order[jnp.clip(lookup(start, slot_e) + pos, 0, 2 * m - 1)] % m, 0)
    slot_of = lookup(pad_start - start, e_flat) + rank
    return (tile_e, n_used.reshape(1).astype(i32), (src * SUBLANES).astype(i32), (slot_of * SUBLANES).astype(i32))


def _experts_kernel(te_ref, nu_ref, src_ref, h_hbm, wg_ref, wu_ref, wd_ref, o_ref, xg_ref, xb_ref, acc_ref, sems,
                    *, n_tiles, n_steps):
    t, j = pl.program_id(0), pl.program_id(1)
    tm = xb_ref.shape[0]
    rows_per_step = tm // n_steps
    n_used = nu_ref[0]
    used = t < n_used
    buf = t % 2

    def start_row(tile, dst_buf, r):
        src = pl.multiple_of(src_ref[tile * tm + r], SUBLANES)
        dst = pl.multiple_of(r * SUBLANES, SUBLANES)
        pltpu.make_async_copy(h_hbm.at[pl.ds(src, SUBLANES)], xg_ref.at[dst_buf, pl.ds(dst, SUBLANES)],
                              sems.at[dst_buf]).start()

    def wait_rows(b_):
        pltpu.make_async_copy(h_hbm.at[pl.ds(0, tm * SUBLANES)], xg_ref.at[b_], sems.at[b_]).wait()

    @pl.when((t == 0) & (j == 0))
    def _():
        def issue(r, carry):
            start_row(0, 0, r)
            return carry

        lax.fori_loop(0, tm, issue, 0, unroll=8)

    @pl.when((j == 0) & ((t == 0) | (t - 1 < n_used)))
    def _():
        wait_rows(buf)
        xb_ref[...] = _load_token_major(xg_ref.at[buf], tm).astype(BF16)

    @pl.when(j == 0)
    def _():
        acc_ref[...] = jnp.zeros_like(acc_ref)

    @pl.when(used)
    def _():
        nxt = jnp.minimum(t + 1, n_tiles - 1)
        for r in range(rows_per_step):
            start_row(nxt, 1 - buf, j * rows_per_step + r)
        h = xb_ref[...]
        a = jnp.dot(h, wg_ref[...].astype(BF16), preferred_element_type=F32)
        u = jnp.dot(h, wu_ref[...].astype(BF16), preferred_element_type=F32)
        act = (_silu(a) * u).astype(BF16)
        acc_ref[...] += jnp.dot(act, wd_ref[...].astype(BF16), preferred_element_type=F32)

    @pl.when(j == n_steps - 1)
    def _():
        _store_token_major(o_ref, acc_ref[...])

    @pl.when(used & (t == n_tiles - 1) & (j == n_steps - 1))
    def _():
        wait_rows(1 - buf)


def _moe_ff_tile(dff):
    return MOE_FF_TILE if dff % MOE_FF_TILE == 0 else dff


def _moe_tile_rows(m, dff):
    per_expert = 2.0 * m / N_EXPERTS
    unit = (dff // _moe_ff_tile(dff)) * 2 * SUBLANES
    rows = (per_expert + 3.0 * math.sqrt(per_expert)) / MOE_TILES_PER_EXPERT
    return int(math.ceil(rows / unit)) * unit


def _experts(h2, tables, wg, wu, wd, layer, tm):
    d, dff = wg.shape[-2:]
    assert d == SUBLANES * LANES
    tile_e, n_used, src, _ = tables
    n_tiles = tile_e.shape[0]
    tf = MOE_FF_TILE if dff % MOE_FF_TILE == 0 else dff
    nj = dff // tf
    assert tm % nj == 0
    col = lambda t, j, te, nu, src: (layer, te[t], 0, jnp.where(t < nu[0], j, nj - 1))
    rowb = lambda t, j, te, nu, src: (layer, te[t], jnp.where(t < nu[0], j, nj - 1), 0)
    return pl.pallas_call(
        functools.partial(_experts_kernel, n_tiles=n_tiles, n_steps=nj),
        grid_spec=pltpu.PrefetchScalarGridSpec(
            num_scalar_prefetch=3,
            grid=(n_tiles, nj),
            in_specs=[pl.BlockSpec(memory_space=pl.ANY),
                      pl.BlockSpec((None, None, d, tf), col),
                      pl.BlockSpec((None, None, d, tf), col),
                      pl.BlockSpec((None, None, tf, d), rowb)],
            out_specs=pl.BlockSpec((tm * SUBLANES, LANES), lambda t, j, te, nu, src: (t, 0)),
            scratch_shapes=[pltpu.VMEM((2, tm * SUBLANES, LANES), F32), pltpu.VMEM((tm, d), BF16),
                            pltpu.VMEM((tm, d), F32), pltpu.SemaphoreType.DMA((2,))]),
        out_shape=jax.ShapeDtypeStruct((n_tiles * tm * SUBLANES, LANES), F32),
        compiler_params=_params(("arbitrary", "arbitrary")),
        name="moe_experts",
    )(tile_e, n_used, src, h2, wg, wu, wd)


def _combine_kernel(slot_ref, x_ref, sel_ref, mod_ref, g_ref, y_hbm, o_ref, r1_ref, r2_ref, sems,
                    *, d, m, rows_per_sample, final):
    t = x_ref.shape[0]
    base = pl.program_id(0) * rows_per_sample + pl.program_id(1) * t

    def issue(k, carry):
        dst = pl.ds(pl.multiple_of(k * SUBLANES, SUBLANES), SUBLANES)
        s1 = pl.multiple_of(slot_ref[base + k], SUBLANES)
        s2 = pl.multiple_of(slot_ref[m + base + k], SUBLANES)
        pltpu.make_async_copy(y_hbm.at[pl.ds(s1, SUBLANES)], r1_ref.at[dst], sems.at[0]).start()
        pltpu.make_async_copy(y_hbm.at[pl.ds(s2, SUBLANES)], r2_ref.at[dst], sems.at[1]).start()
        return carry

    lax.fori_loop(0, t, issue, 0, unroll=8)
    pltpu.make_async_copy(y_hbm.at[pl.ds(0, t * SUBLANES)], r1_ref, sems.at[0]).wait()
    pltpu.make_async_copy(y_hbm.at[pl.ds(0, t * SUBLANES)], r2_ref, sems.at[1]).wait()
    y = sel_ref[:, 0:1] * _load_token_major(r1_ref, t) + sel_ref[:, 1:2] * _load_token_major(r2_ref, t)
    x = x_ref[...] + mod_ref[:, 5 * d:6 * d] * y
    if final:
        x = x * lax.rsqrt(jnp.mean(x * x, axis=-1, keepdims=True) + EPS) * g_ref[...]
    o_ref[...] = x


def _combine(xs, sel, mod_l, y_sorted, slot_of, final_g, n_lat_tiles, n_row_tiles, final):
    b, _, d = xs.shape
    t = ROW_TILE
    rows = n_row_tiles * t
    row = lambda i, r, s: (i, r, 0)
    return pl.pallas_call(
        functools.partial(_combine_kernel, d=d, m=b * rows, rows_per_sample=rows, final=final),
        grid_spec=pltpu.PrefetchScalarGridSpec(
            num_scalar_prefetch=1,
            grid=(b, n_row_tiles),
            in_specs=[pl.BlockSpec((None, t, d), row),
                      pl.BlockSpec((None, t, LANES), row),
                      pl.BlockSpec((None, 1, mod_l.shape[-1]),
                                   lambda i, r, s: (jnp.where(r < n_lat_tiles, i, b), 0, 0)),
                      pl.BlockSpec((1, d), lambda i, r, s: (0, 0)),
                      pl.BlockSpec(memory_space=pl.ANY)],
            out_specs=pl.BlockSpec((None, t, d), row),
            scratch_shapes=[pltpu.VMEM((t * SUBLANES, LANES), F32), pltpu.VMEM((t * SUBLANES, LANES), F32),
                            pltpu.SemaphoreType.DMA((2,))]),
        out_shape=jax.ShapeDtypeStruct((b, rows, d), F32),
        compiler_params=_params(("arbitrary", "arbitrary")),
        name="moe_combine",
    )(slot_of, xs, sel, mod_l, final_g, y_sorted)


def _rope_tables(s_lat, n_ctx):
    rows = s_lat // GRID_W
    pos_row = jnp.repeat(jnp.arange(rows, dtype=jnp.int32), GRID_W).astype(F32)
    pos_col = jnp.tile(jnp.arange(GRID_W, dtype=jnp.int32), rows).astype(F32)
    quarter = HEAD_DIM // 4
    inv_freq = ROPE_THETA ** (-jnp.arange(quarter, dtype=F32) / quarter)
    ang_r = pos_row[:, None] * inv_freq
    ang_c = pos_col[:, None] * inv_freq
    ang = jnp.concatenate([ang_r, ang_r, ang_c, ang_c], axis=-1)
    cos, sin = jnp.cos(ang), jnp.sin(ang)
    first = (np.arange(HEAD_DIM) % (2 * quarter)) < quarter
    sin_lo = jnp.where(first, -sin, 0.0)
    sin_hi = jnp.where(first, 0.0, sin)

    def widen(tab, ctx_fill):
        tab = jnp.concatenate([tab, jnp.full((n_ctx, HEAD_DIM), ctx_fill, F32)], axis=0)
        return jnp.tile(tab, (1, LANES // HEAD_DIM))

    return widen(cos, 1.0), widen(sin_lo, 0.0), widen(sin_hi, 0.0)


def _dft_tables(n):
    k = np.arange(n, dtype=np.int64)
    ang = 2.0 * np.pi * ((k[:, None] * k[None, :]) % n).astype(np.float64) / n
    return np.cos(ang), np.sin(ang)


def _constants(s_lat, n_ctx):
    c64, s64 = _dft_tables(F_GROUP_DIM)
    eye = np.eye(F_GROUPS)
    cn, sn = _dft_tables(s_lat)
    cc, sc = _dft_tables(n_ctx)
    bf = lambda a: jnp.asarray(a, dtype=F32).astype(BF16)
    return dict(
        gmat=bf(np.kron(np.eye(512 // HEAD_DIM), np.ones((HEAD_DIM, HEAD_DIM)))),
        c64=bf(np.kron(eye, c64) / math.sqrt(F_GROUP_DIM)),
        s64=bf(np.kron(eye, s64) / math.sqrt(F_GROUP_DIM)),
        cn=bf(cn / math.sqrt(s_lat)), sn=bf(sn / math.sqrt(s_lat)),
        cc=bf(cc / math.sqrt(n_ctx)), sc=bf(sc / math.sqrt(n_ctx)),
    )


def _dup_heads(w, off, n_heads):
    cols = []
    for hd in range(n_heads):
        blk = w[:, off + hd * HEAD_DIM: off + (hd + 1) * HEAD_DIM]
        cols += [blk, blk]
    return cols


def _relayout_w_in(w):
    cols = [w[:, _OFF["aq"]:_OFF["bq"]], w[:, _OFF["bq"]:_OFF["bk"]],
            w[:, _OFF["cq"]:_OFF["ck"]], w[:, _OFF["fz"]:_OFF["gt"]]]
    cols += _dup_heads(w, _OFF["bk"], B_KV) + _dup_heads(w, _OFF["bv"], B_KV)
    cols += _dup_heads(w, _OFF["ck"], C_KV) + _dup_heads(w, _OFF["cv"], C_KV)
    return jnp.concatenate(cols, axis=1).astype(BF16)


def kernel(x, c, ctx, c_ctx, w_mod, b_mod, norm1_g, norm2_g, w_in, diff_lambda, diff_subln_g, qk_norm_g,
           sink_logits, w_branch, w_out, ffn_w_gate, ffn_w_up, ffn_w_down, router_w, moe_w_gate, moe_w_up,
           moe_w_down, final_g):
    b, s_lat, d = x.shape
    n_ctx = ctx.shape[1]
    depth = w_mod.shape[0]
    nt = s_lat + n_ctx
    t = ROW_TILE
    assert s_lat % t == 0 and n_ctx % t == 0 and s_lat % n_ctx == 0 and b < MOD_ROWS
    assert w_in.shape[-1] == _OFF["end"] and d * N_BRANCH == _OFF["end"] - _OFF["gt"]
    n_lat_tiles = s_lat // t

    xs = jnp.concatenate([x, ctx], axis=1)
    cvec = jnp.concatenate([c, c_ctx[None], jnp.zeros((MOD_ROWS - b - 1, d), F32)], axis=0)
    mod = _modulation(cvec, w_mod, b_mod).reshape(depth, MOD_ROWS, 1, 6 * d)
    tabs = _rope_tables(s_lat, n_ctx)
    consts = _constants(s_lat, n_ctx)

    out = None
    for l in range(depth):
        last = l == depth - 1
        mod_l = mod[l]
        p = _inproj(xs, mod_l, norm1_g[l][None], _relayout_w_in(w_in[l]), tabs,
                    jnp.tile(qk_norm_g[l, 0], 512 // HEAD_DIM)[None], jnp.tile(qk_norm_g[l, 1], 256 // HEAD_DIM)[None],
                    consts, s_lat)
        p = (None, *p)
        w_gates = w_in[l][:, _OFF["gt"]:_OFF["end"]].astype(BF16)
        lam_init = 0.8 - 0.6 * math.exp(-0.3 * l)
        n_q_tiles = n_lat_tiles if last else nt // t
        oa, ob, oc = _attention(p, diff_lambda[l], diff_subln_g[l][None], sink_logits[l],
                                n_lat_tiles, n_q_tiles, s_lat, lam_init)
        od = _fourier(p[10], p[11], consts, n_lat_tiles, n_q_tiles, s_lat)
        xs = _merge(xs, mod_l, norm1_g[l][None], w_gates, (oa, ob, oc, od), w_branch[l].astype(BF16),
                    w_out[l].astype(BF16), s_lat, n_q_tiles * t)

        i = l // 2
        if l % 2 == 0:
            assert not last
            xs = _ffn(xs, mod_l, norm2_g[l][None], ffn_w_gate[i].astype(BF16), ffn_w_up[i].astype(BF16),
                      ffn_w_down[i].astype(BF16), s_lat)
        else:
            w_r = jnp.pad(router_w[i], ((0, 0), (0, LANES - N_EXPERTS)))
            n_row_tiles = n_q_tiles
            m = b * n_row_tiles * t
            h2, sel = _router(xs, mod_l, norm2_g[l][None], w_r, s_lat, n_row_tiles * t)
            tm = _moe_tile_rows(m, moe_w_gate.shape[-1])
            tables = _dispatch(sel.reshape(m, LANES), tm)
            y_sorted = _experts(h2.reshape(m * SUBLANES, LANES), tables, moe_w_gate, moe_w_up, moe_w_down, i, tm)
            res = _combine(xs, sel, mod_l, y_sorted, tables[3], final_g[None], n_lat_tiles, n_row_tiles, last)
            if last:
                out = res
            else:
                xs = res
    if out is None:
        raise NotImplementedError("final normalisation is fused into the last (mixture-of-experts) layer")
    return out
```

```python
import functools
import math

import numpy as np
import jax
import jax.numpy as jnp
from jax import lax
from jax.experimental import pallas as pl
from jax.experimental.pallas import tpu as pltpu

F32 = jnp.float32
BF16 = jnp.bfloat16

GRID_W = 64
HEAD_DIM = 64
ROPE_THETA = 10000.0
WINDOW = 128
A_HEADS = 4
B_HEADS = 8
B_KV = 2
C_HEADS = 8
C_KV = 2
F_GROUPS = 8
F_GROUP_DIM = 64
BRANCH_W = 512
N_BRANCH = 4
N_EXPERTS = 8
EPS = 1e-6
SUBLN_EPS = 1e-5
NEG_INF = -1e30
LOG2E = math.log2(math.e)

LANES = 128
SUBLANES = 8
ROW_TILE = 256
WIDE_TILES = (768, 512, 256)
FFN_CHUNK = 768
MOE_TILES_PER_EXPERT = 4
MOE_FF_TILE = 512
MOD_ROWS = 16
VMEM_LIMIT = 56 * 1024 * 1024

_OFF = dict(aq=0, ak=512, av=1024, bq=1536, bk=2048, bv=2176, cq=2304, ck=2816, cv=2944,
            fz=3072, gt=3584, end=7680)
_P = dict(aq=0, ak=512, av=1024, bq=1536, cq=2048, fz=2560, bk=3072, bv=3328, ck=3584, cv=3840, end=4096)


def _params(sem):
    return pltpu.CompilerParams(dimension_semantics=sem, vmem_limit_bytes=VMEM_LIMIT)


def _full(shape):
    n = len(shape)
    return pl.BlockSpec(shape, lambda *_: (0,) * n)


def _silu(x):
    return x / (1.0 + jnp.exp(-x))


def _sigmoid(x):
    return 1.0 / (1.0 + jnp.exp(-x))


def _adaln(x, g, shift, scale):
    y = x * lax.rsqrt(jnp.mean(x * x, axis=-1, keepdims=True) + EPS) * g
    return y * (1.0 + scale) + shift


def _mod_kernel(c_ref, w_ref, b_ref, o_ref):
    s = _silu(c_ref[...]).astype(BF16)
    o_ref[...] = jnp.dot(s, w_ref[...].astype(BF16), preferred_element_type=F32) + b_ref[...]


def _modulation(cvec, w_mod, b_mod):
    depth, d, n = w_mod.shape
    tn = 1536 if n % 1536 == 0 else n
    return pl.pallas_call(
        _mod_kernel,
        grid=(depth, n // tn),
        in_specs=[_full((MOD_ROWS, d)),
                  pl.BlockSpec((None, d, tn), lambda l, j: (l, 0, j)),
                  pl.BlockSpec((None, 1, tn), lambda l, j: (l, 0, j))],
        out_specs=pl.BlockSpec((None, MOD_ROWS, tn), lambda l, j: (l, 0, j)),
        out_shape=jax.ShapeDtypeStruct((depth, MOD_ROWS, n), F32),
        compiler_params=_params(("arbitrary", "arbitrary")),
        name="modulation",
    )(cvec, w_mod, b_mod.reshape(depth, 1, n))


def _rope(p, cos, sin_lo, sin_hi):
    outs = []
    for c in range(p.shape[1] // LANES):
        xc = p[:, c * LANES:(c + 1) * LANES]
        outs.append(xc * cos + pltpu.roll(xc, LANES - 16, 1) * sin_lo + pltpu.roll(xc, 16, 1) * sin_hi)
    return outs[0] if len(outs) == 1 else jnp.concatenate(outs, axis=1)


def _group_rmsnorm(y, gmat, g):
    ss = jnp.dot((y * y).astype(BF16), gmat, preferred_element_type=F32)
    return y * lax.rsqrt(ss * (1.0 / HEAD_DIM) + EPS) * g


def _inproj_kernel(x_ref, modb_ref, modc_ref, g_ref, w_ref, cos_ref, slo_ref, shi_ref, qg_ref, kg_ref, gmat_ref,
                   c64_ref, s64_ref,
                   aq_ref, akt_ref, av_ref, bq_ref, bkt_ref, bv_ref, cq_ref, ckt_ref, cv_ref,
                   fzc_ref, fzs_ref, *, d, s_lat):
    mod = _row_mod(modb_ref, modc_ref, x_ref.shape[0], s_lat, d)
    h = _adaln(x_ref[...], g_ref[...], mod(0), mod(1)).astype(BF16)
    cos, slo, shi = cos_ref[...], slo_ref[...], shi_ref[...]
    scale = HEAD_DIM ** -0.5 * LOG2E

    def proj(name, width):
        return jnp.dot(h, w_ref[:, _P[name]:_P[name] + width], preferred_element_type=F32)

    aq_ref[...] = (_rope(proj("aq", 512), cos, slo, shi) * scale).astype(BF16)
    akt_ref[...] = _rope(proj("ak", 512), cos, slo, shi).T.astype(BF16)
    av_ref[...] = proj("av", 512).astype(BF16)

    bq = _group_rmsnorm(proj("bq", 512), gmat_ref[...], qg_ref[...])
    bq_ref[...] = (_rope(bq, cos, slo, shi) * scale).astype(BF16)
    bk = _group_rmsnorm(proj("bk", 256), gmat_ref[0:256, 0:256], kg_ref[...])
    bkt_ref[...] = _rope(bk, cos, slo, shi).T.astype(BF16)
    bv_ref[...] = proj("bv", 256).astype(BF16)

    cq_ref[...] = (_rope(proj("cq", 512), cos, slo, shi) * scale).astype(BF16)
    ckt_ref[...] = _rope(proj("ck", 256), cos, slo, shi).T.astype(BF16)
    cv_ref[...] = proj("cv", 256).astype(BF16)

    fz = proj("fz", 512).astype(BF16)
    fzc_ref[...] = jnp.dot(fz, c64_ref[...], preferred_element_type=F32).astype(BF16)
    fzs_ref[...] = jnp.dot(fz, s64_ref[...], preferred_element_type=F32).astype(BF16)


def _inproj(xs, mod_l, g, w_p, tabs, qg, kg, consts, s_lat):
    b, nt, d = xs.shape
    t = _wide_tile(nt)
    row = lambda i, r: (i, r, 0)
    tr = lambda i, r: (i, 0, r)
    tab = pl.BlockSpec((t, LANES), lambda i, r: (r, 0))

    def out(width):
        return pl.BlockSpec((None, t, width), row), jax.ShapeDtypeStruct((b, nt, width), BF16)

    def out_t(width):
        return pl.BlockSpec((None, width, t), tr), jax.ShapeDtypeStruct((b, width, nt), BF16)

    outs = [out(512), out_t(512), out(512), out(512), out_t(256), out(256),
            out(512), out_t(256), out(256), out(512), out(512)]
    return pl.pallas_call(
        functools.partial(_inproj_kernel, d=d, s_lat=s_lat),
        grid=(b, nt // t),
        in_specs=[pl.BlockSpec((None, t, d), row), *_wide_mod_specs(mod_l, b),
                  _full((1, d)),
                  pl.BlockSpec(w_p.shape, lambda i, r: (0, 0), pipeline_mode=pl.Buffered(1)),
                  tab, tab, tab,
                  _full((1, 512)), _full((1, 256)), _full((512, 512)), _full((512, 512)), _full((512, 512))],
        out_specs=[o[0] for o in outs],
        out_shape=[o[1] for o in outs],
        compiler_params=_params(("arbitrary", "arbitrary")),
        name="inproj",
    )(xs, mod_l, mod_l, g, w_p, tabs[0], tabs[1], tabs[2], qg, kg, consts["gmat"], consts["c64"], consts["s64"])


def _half_masks(shape):
    lane = lax.broadcasted_iota(jnp.int32, shape, 1)
    return lane < HEAD_DIM, lane >= HEAD_DIM


def _exp2_bf16(s):
    return jnp.exp2(s - jnp.max(s, axis=-1, keepdims=True)).astype(BF16)


def _pipelined_scores(score_fns):
    nxt = score_fns[0]()
    for i in range(len(score_fns)):
        cur = nxt
        if i + 1 < len(score_fns):
            nxt = score_fns[i + 1]()
        yield cur


def _attn_a_kernel(q_ref, kt_ref, v_ref, lp_ref, g_ref, o_ref, *, n_lat_tiles, s_lat, lam_init):
    r = pl.program_id(1)
    lp = lp_ref[...]
    lam = (jnp.exp(jnp.sum(lp[0:1] * lp[1:2], axis=-1, keepdims=True))
           - jnp.exp(jnp.sum(lp[2:3] * lp[3:4], axis=-1, keepdims=True)) + lam_init)
    g = g_ref[...] * (1.0 - lam_init)
    nt = kt_ref.shape[1]

    def body(k0):
        lo, hi = _half_masks((q_ref.shape[0], LANES))
        ones = jnp.ones((nt - k0, LANES), BF16)

        def score_fn(hd, msk):
            cs = slice(hd * LANES, (hd + 1) * LANES)
            return lambda: jnp.dot(jnp.where(msk, q_ref[:, cs], 0), kt_ref[cs, k0:nt], preferred_element_type=F32)

        fns = [score_fn(hd, msk) for hd in range(A_HEADS) for msk in (lo, hi)]
        outs = []
        for i, sc in enumerate(_pipelined_scores(fns)):
            hd = i // 2
            cs = slice(hd * LANES, (hd + 1) * LANES)
            vo = jnp.concatenate([v_ref[k0:nt, cs], ones], axis=1)
            full = jnp.dot(_exp2_bf16(sc), vo, preferred_element_type=F32)
            outs.append(full[:, 0:LANES] * (1.0 / full[:, LANES:LANES + 1]))
            if len(outs) == 2:
                o = outs[0] - lam * outs[1]
                outs = []
                o = o * lax.rsqrt(jnp.mean(o * o, axis=-1, keepdims=True) + SUBLN_EPS) * g
                o_ref[:, cs] = o.astype(BF16)

    pl.when(r < n_lat_tiles)(lambda: body(0))
    pl.when(r >= n_lat_tiles)(lambda: body(s_lat))


def _attn_b_kernel(q_ref, kt_ref, v_ref, o_ref, *, n_lat_tiles, s_lat):
    r = pl.program_id(1)
    nt = kt_ref.shape[1]

    def body(k0):
        lo, hi = _half_masks((q_ref.shape[0], LANES))
        ones = jnp.ones((nt - k0, LANES), BF16)

        def score_fn(pair, msk):
            kv = pair // (B_HEADS // B_KV // 2)
            q = q_ref[:, pair * LANES:(pair + 1) * LANES]
            return lambda: jnp.dot(jnp.where(msk, q, 0), kt_ref[kv * LANES:(kv + 1) * LANES, k0:nt],
                                   preferred_element_type=F32)

        fns = [score_fn(pair, msk) for pair in range(B_HEADS // 2) for msk in (lo, hi)]
        halves = []
        for i, sc in enumerate(_pipelined_scores(fns)):
            pair = i // 2
            kv = pair // (B_HEADS // B_KV // 2)
            vo = jnp.concatenate([v_ref[k0:nt, kv * LANES:(kv + 1) * LANES], ones], axis=1)
            full = jnp.dot(_exp2_bf16(sc), vo, preferred_element_type=F32)
            halves.append(full[:, 0:LANES] * (1.0 / full[:, LANES:LANES + 1]))
            if len(halves) == 2:
                o_ref[:, pair * LANES:(pair + 1) * LANES] = jnp.where(lo, halves[0], halves[1]).astype(BF16)
                halves = []

    pl.when(r < n_lat_tiles)(lambda: body(0))
    pl.when(r >= n_lat_tiles)(lambda: body(s_lat))


def _attn_c_kernel(sink_ref, q_ref, ktl_ref, ktm_ref, ktr_ref, ktc_ref, vl_ref, vm_ref, vr_ref, vc_ref, o_ref,
                   *, n_lat_tiles, s_lat):
    r = pl.program_id(1)
    t = q_ref.shape[0]
    n_band = t + 2 * WINDOW
    n_keys = n_band + ktc_ref.shape[1]
    u = lax.broadcasted_iota(jnp.int32, (t, n_keys), 1)
    row = lax.broadcasted_iota(jnp.int32, (t, n_keys), 0)
    j = r * t - WINDOW + u
    in_band = (u >= row) & (u <= row + 2 * WINDOW) & (j >= 0) & (j < s_lat) & (r < n_lat_tiles)
    valid = in_band | (u >= n_band)
    lo, hi = _half_masks((t, LANES))
    ones = jnp.ones((n_keys, LANES), BF16)
    kts, vos = [], []
    for kv in range(C_KV):
        ks = slice(kv * LANES, (kv + 1) * LANES)
        kts.append(jnp.concatenate([ktl_ref[ks, :], ktm_ref[ks, :], ktr_ref[ks, :], ktc_ref[ks, :]], axis=1))
        v = jnp.concatenate([vl_ref[:, ks], vm_ref[:, ks], vr_ref[:, ks], vc_ref[:, ks]], axis=0)
        vos.append(jnp.concatenate([v, ones], axis=1))

    def score_fn(pair, msk):
        kv = pair // (C_HEADS // C_KV // 2)
        q = q_ref[:, pair * LANES:(pair + 1) * LANES]
        return lambda: jnp.dot(jnp.where(msk, q, 0), kts[kv], preferred_element_type=F32)

    fns = [score_fn(pair, msk) for pair in range(C_HEADS // 2) for msk in (lo, hi)]
    halves = []
    for i, sc in enumerate(_pipelined_scores(fns)):
        pair = i // 2
        kv = pair // (C_HEADS // C_KV // 2)
        sink = sink_ref[i] * LOG2E
        sc = jnp.where(valid, sc, NEG_INF)
        m = jnp.maximum(jnp.max(sc, axis=-1, keepdims=True), sink)
        full = jnp.dot(jnp.exp2(sc - m).astype(BF16), vos[kv], preferred_element_type=F32)
        l = full[:, LANES:LANES + 1] + jnp.exp2(sink - m)
        halves.append(full[:, 0:LANES] * (1.0 / l))
        if len(halves) == 2:
            o_ref[:, pair * LANES:(pair + 1) * LANES] = jnp.where(lo, halves[0], halves[1]).astype(BF16)
            halves = []


def _attention(p, lp, subln_g, sink, n_lat_tiles, n_q_tiles, s_lat, lam_init):
    (_, aq, akt, av, bq, bkt, bv, cq, ckt, cv, _, _) = p
    b, nt, _ = aq.shape
    t = ROW_TILE
    grid = (b, n_q_tiles)
    row = lambda i, r: (i, r, 0)
    whole = lambda i, r: (i, 0, 0)
    sem = _params(("arbitrary", "arbitrary"))
    o_spec = pl.BlockSpec((None, t, 512), row)
    o_shape = jax.ShapeDtypeStruct((b, n_q_tiles * t, 512), BF16)

    oa = pl.pallas_call(
        functools.partial(_attn_a_kernel, n_lat_tiles=n_lat_tiles, s_lat=s_lat, lam_init=lam_init),
        grid=grid,
        in_specs=[pl.BlockSpec((None, t, 512), row), pl.BlockSpec((None, 512, nt), whole),
                  pl.BlockSpec((None, nt, 512), whole), _full((4, HEAD_DIM)), _full((1, 2 * HEAD_DIM))],
        out_specs=o_spec, out_shape=o_shape, compiler_params=sem, name="attn_diff",
    )(aq, akt, av, lp, subln_g)

    ob = pl.pallas_call(
        functools.partial(_attn_b_kernel, n_lat_tiles=n_lat_tiles, s_lat=s_lat),
        grid=grid,
        in_specs=[pl.BlockSpec((None, t, 512), row), pl.BlockSpec((None, 256, nt), whole),
                  pl.BlockSpec((None, nt, 256), whole)],
        out_specs=o_spec, out_shape=o_shape, compiler_params=sem, name="attn_gqa",
    )(bq, bkt, bv)

    wpt = t // WINDOW
    last = s_lat // WINDOW - 1
    left = lambda r: jnp.clip(r * wpt - 1, 0, last)
    mid = lambda r: jnp.minimum(r, n_lat_tiles - 1)
    right = lambda r: jnp.clip((r + 1) * wpt, 0, last)
    n_ctx = nt - s_lat
    ctx_blk = s_lat // n_ctx
    oc = pl.pallas_call(
        functools.partial(_attn_c_kernel, n_lat_tiles=n_lat_tiles, s_lat=s_lat),
        grid=grid,
        in_specs=[pl.BlockSpec(memory_space=pltpu.SMEM),
                  pl.BlockSpec((None, t, 512), row),
                  pl.BlockSpec((None, 256, WINDOW), lambda i, r: (i, 0, left(r))),
                  pl.BlockSpec((None, 256, t), lambda i, r: (i, 0, mid(r))),
                  pl.BlockSpec((None, 256, WINDOW), lambda i, r: (i, 0, right(r))),
                  pl.BlockSpec((None, 256, n_ctx), lambda i, r: (i, 0, ctx_blk)),
                  pl.BlockSpec((None, WINDOW, 256), lambda i, r: (i, left(r), 0)),
                  pl.BlockSpec((None, t, 256), lambda i, r: (i, mid(r), 0)),
                  pl.BlockSpec((None, WINDOW, 256), lambda i, r: (i, right(r), 0)),
                  pl.BlockSpec((None, n_ctx, 256), lambda i, r: (i, ctx_blk, 0))],
        out_specs=o_spec, out_shape=o_shape, compiler_params=sem, name="attn_window",
    )(sink, cq, ckt, ckt, ckt, ckt, cv, cv, cv, cv)
    return oa, ob, oc


def _fourier_kernel(cn_ref, sn_ref, cc_ref, sc_ref, zc_ref, zs_ref, o_ref, *, s_lat, rows):
    r = pl.program_id(1)
    t = o_ref.shape[0]
    nt = zc_ref.shape[0]
    n_full = s_lat // t
    lat_rows = s_lat - n_full * t

    def latent(n):
        return (jnp.dot(cn_ref[0:n, :], zc_ref[0:s_lat, :], preferred_element_type=F32)
                - jnp.dot(sn_ref[0:n, :], zs_ref[0:s_lat, :], preferred_element_type=F32)).astype(BF16)

    @pl.when(r < n_full)
    def _():
        o_ref[...] = latent(t)

    if rows > s_lat:
        @pl.when(r >= n_full)
        def _():
            if lat_rows:
                o_ref[0:lat_rows, :] = latent(lat_rows)
            y = (jnp.dot(cc_ref[...], zc_ref[s_lat:nt, :], preferred_element_type=F32)
                 - jnp.dot(sc_ref[...], zs_ref[s_lat:nt, :], preferred_element_type=F32))
            o_ref[lat_rows:t, :] = y.astype(BF16)


def _fourier(fzc, fzs, consts, s_lat, rows):
    b, nt, w = fzc.shape
    t = _wide_tile(rows)
    n_ctx = nt - s_lat
    assert rows == s_lat or (rows == nt and nt - s_lat // t * t == t)
    tile = lambda i, r: (r, 0)
    whole = lambda i, r: (i, 0, 0)
    return pl.pallas_call(
        functools.partial(_fourier_kernel, s_lat=s_lat, rows=rows),
        grid=(b, rows // t),
        in_specs=[pl.BlockSpec((t, s_lat), tile), pl.BlockSpec((t, s_lat), tile),
                  _full((n_ctx, n_ctx)), _full((n_ctx, n_ctx)),
                  pl.BlockSpec((None, nt, w), whole), pl.BlockSpec((None, nt, w), whole)],
        out_specs=pl.BlockSpec((None, t, w), lambda i, r: (i, r, 0)),
        out_shape=jax.ShapeDtypeStruct((b, rows, w), BF16),
        compiler_params=_params(("arbitrary", "arbitrary")),
        name="fourier",
    )(consts["cn"], consts["sn"], consts["cc"], consts["sc"], fzc, fzs)


def _wide_tile(rows):
    for t in WIDE_TILES:
        if rows % t == 0:
            return t
    raise ValueError(rows)


def _wide_mod_specs(mod_l, b):
    n = mod_l.shape[-1]
    return [pl.BlockSpec((None, 1, n), lambda i, r: (i, 0, 0)), pl.BlockSpec((None, 1, n), lambda i, r: (b, 0, 0))]


def _row_mod(modb_ref, modc_ref, rows, s_lat, d):
    is_lat = pl.program_id(1) * rows + lax.broadcasted_iota(jnp.int32, (rows, 1), 0) < s_lat
    return lambda k: jnp.where(is_lat, modb_ref[:, k * d:(k + 1) * d], modc_ref[:, k * d:(k + 1) * d])


def _merge_kernel(x_ref, modb_ref, modc_ref, g_ref, wgt_ref, oa_ref, ob_ref, oc_ref, od_ref, wbr_ref, wo_ref,
                  o_ref, *, d, s_lat):
    mod = _row_mod(modb_ref, modc_ref, x_ref.shape[0], s_lat, d)
    x = x_ref[...]
    h = _adaln(x, g_ref[...], mod(0), mod(1)).astype(BF16)
    merged = None
    for i, br in enumerate((oa_ref, ob_ref, oc_ref, od_ref)):
        gate = _sigmoid(jnp.dot(h, wgt_ref[:, i * d:(i + 1) * d], preferred_element_type=F32))
        y = gate * jnp.dot(br[...], wbr_ref[i], preferred_element_type=F32)
        merged = y if merged is None else merged + y
    out = jnp.dot(merged.astype(BF16), wo_ref[...], preferred_element_type=F32)
    o_ref[...] = x + mod(2) * out


def _merge(xs, mod_l, g, w_gates, branches, w_br, w_o, s_lat, rows):
    b, _, d = xs.shape
    t = _wide_tile(rows)
    row = lambda i, r: (i, r, 0)
    br_spec = pl.BlockSpec((None, t, BRANCH_W), row)
    const = lambda shape: pl.BlockSpec(shape, lambda i, r: (0,) * len(shape), pipeline_mode=pl.Buffered(1))
    return pl.pallas_call(
        functools.partial(_merge_kernel, d=d, s_lat=s_lat),
        grid=(b, rows // t),
        in_specs=[pl.BlockSpec((None, t, d), row), *_wide_mod_specs(mod_l, b),
                  _full((1, d)), const(w_gates.shape),
                  br_spec, br_spec, br_spec, br_spec,
                  const(w_br.shape), const(w_o.shape)],
        out_specs=pl.BlockSpec((None, t, d), row),
        out_shape=jax.ShapeDtypeStruct((b, rows, d), F32),
        compiler_params=_params(("arbitrary", "arbitrary")),
        name="merge",
    )(xs, mod_l, mod_l, g, w_gates, *branches, w_br, w_o)


def _ffn_kernel(x_ref, modb_ref, modc_ref, g_ref, wg_ref, wu_ref, wd_ref, o_ref, *, d, s_lat):
    mod = _row_mod(modb_ref, modc_ref, x_ref.shape[0], s_lat, d)
    x = x_ref[...]
    h = _adaln(x, g_ref[...], mod(3), mod(4)).astype(BF16)
    dff = wg_ref.shape[1]
    y = None
    for c0 in range(0, dff, FFN_CHUNK):
        c1 = min(c0 + FFN_CHUNK, dff)
        a = jnp.dot(h, wg_ref[:, c0:c1], preferred_element_type=F32)
        u = jnp.dot(h, wu_ref[:, c0:c1], preferred_element_type=F32)
        part = jnp.dot((_silu(a) * u).astype(BF16), wd_ref[c0:c1, :], preferred_element_type=F32)
        y = part if y is None else y + part
    o_ref[...] = x + mod(5) * y


def _ffn(xs, mod_l, g, wg, wu, wd, s_lat):
    b, nt, d = xs.shape
    t = _wide_tile(nt)
    row = lambda i, r: (i, r, 0)
    const = lambda shape: pl.BlockSpec(shape, lambda i, r: (0, 0), pipeline_mode=pl.Buffered(1))
    return pl.pallas_call(
        functools.partial(_ffn_kernel, d=d, s_lat=s_lat),
        grid=(b, nt // t),
        in_specs=[pl.BlockSpec((None, t, d), row), *_wide_mod_specs(mod_l, b),
                  _full((1, d)), const(wg.shape), const(wu.shape), const(wd.shape)],
        out_specs=pl.BlockSpec((None, t, d), row),
        out_shape=jax.ShapeDtypeStruct(xs.shape, F32),
        compiler_params=_params(("arbitrary", "arbitrary")),
        name="ffn_dense",
    )(xs, mod_l, mod_l, g, wg, wu, wd)


def _store_token_major(ref, val):
    rows = val.shape[0]
    for c in range(SUBLANES):
        ref[pl.ds(c, rows, stride=SUBLANES), :] = val[:, c * LANES:(c + 1) * LANES]


def _load_token_major(ref, rows):
    return jnp.concatenate([ref[pl.ds(c, rows, stride=SUBLANES), :] for c in range(SUBLANES)], axis=1)


def _router_kernel(x_ref, modb_ref, modc_ref, g_ref, wr_ref, h_ref, sel_ref, *, d, s_lat):
    mod = _row_mod(modb_ref, modc_ref, x_ref.shape[0], s_lat, d)
    h = _adaln(x_ref[...], g_ref[...], mod(3), mod(4))
    _store_token_major(h_ref, h)
    logits = jnp.dot(h, wr_ref[...], preferred_element_type=F32, precision=lax.Precision.HIGHEST)
    lane = lax.broadcasted_iota(jnp.int32, logits.shape, 1)
    logits = jnp.where(lane < N_EXPERTS, logits, -jnp.inf)
    m1 = jnp.max(logits, axis=-1, keepdims=True)
    i1 = jnp.min(jnp.where(logits == m1, lane, LANES), axis=-1, keepdims=True)
    rest = jnp.where(lane == i1, -jnp.inf, logits)
    m2 = jnp.max(rest, axis=-1, keepdims=True)
    i2 = jnp.min(jnp.where(rest == m2, lane, LANES), axis=-1, keepdims=True)
    e2 = jnp.exp(m2 - m1)
    w1 = 1.0 / (1.0 + e2)
    sel_ref[...] = (jnp.where(lane == 0, w1, 0.0) + jnp.where(lane == 1, e2 * w1, 0.0)
                    + jnp.where(lane == 2, i1.astype(F32), 0.0) + jnp.where(lane == 3, i2.astype(F32), 0.0))


def _router(xs, mod_l, g, w_r, s_lat, rows):
    b, _, d = xs.shape
    t = _wide_tile(rows)
    row = lambda i, r: (i, r, 0)
    return pl.pallas_call(
        functools.partial(_router_kernel, d=d, s_lat=s_lat),
        grid=(b, rows // t),
        in_specs=[pl.BlockSpec((None, t, d), row), *_wide_mod_specs(mod_l, b),
                  _full((1, d)), _full(w_r.shape)],
        out_specs=[pl.BlockSpec((None, t * SUBLANES, LANES), row), pl.BlockSpec((None, t, LANES), row)],
        out_shape=[jax.ShapeDtypeStruct((b, rows * SUBLANES, LANES), F32),
                   jax.ShapeDtypeStruct((b, rows, LANES), F32)],
        compiler_params=_params(("arbitrary", "arbitrary")),
        name="moe_router",
    )(xs, mod_l, mod_l, g, w_r)


def _dispatch(sel, tm):
    m = sel.shape[0]
    i32 = jnp.int32
    experts = jnp.arange(N_EXPERTS, dtype=i32)

    def lookup(table, idx):
        return jnp.sum(jnp.where(idx[:, None] == experts[None, :], table[None, :], 0), axis=1)

    idx = sel[:, 2:4].astype(i32)
    e_flat = jnp.concatenate([idx[:, 0], idx[:, 1]])
    assignment = jnp.arange(2 * m, dtype=i32)
    _, order = lax.sort_key_val(e_flat, assignment)
    _, rank = lax.sort_key_val(order, assignment)
    counts = jnp.sum((e_flat[:, None] == experts[None, :]).astype(i32), axis=0)
    padded = (counts + tm - 1) // tm * tm
    pad_end = jnp.cumsum(padded)
    pad_start = pad_end - padded
    start = jnp.cumsum(counts) - counts
    n_tiles = 2 * m // tm + N_EXPERTS
    n_used = pad_end[-1] // tm
    tile = jnp.minimum(jnp.arange(n_tiles, dtype=i32), n_used - 1)
    tile_e = jnp.sum((tile[:, None] * tm >= pad_end[None, :]).astype(i32), axis=1)
    slot = jnp.arange(n_tiles * tm, dtype=i32)
    slot_e = jnp.sum((jnp.minimum(slot, pad_end[-1] - 1)[:, None] >= pad_end[None, :]).astype(i32), axis=1)
    pos = slot - lookup(pad_start, slot_e)
    valid = (pos < lookup(counts, slot_e)) & (slot < pad_end[-1])
    src = jnp.where(valid, order[jnp.clip(lookup(start, slot_e) + pos, 0, 2 * m - 1)] % m, 0)
    slot_of = lookup(pad_start - start, e_flat) + rank
    return (tile_e, n_used.reshape(1).astype(i32), (src * SUBLANES).astype(i32), (slot_of * SUBLANES).astype(i32))


def _experts_kernel(te_ref, nu_ref, src_ref, h_hbm, wg_ref, wu_ref, wd_ref, o_ref, xg_ref, xb_ref, acc_ref, sems,
                    *, n_tiles, n_steps):
    t, j = pl.program_id(0), pl.program_id(1)
    tm = xb_ref.shape[0]
    rows_per_step = tm // n_steps
    n_used = nu_ref[0]
    used = t < n_used
    buf = t % 2

    def start_row(tile, dst_buf, r):
        src = pl.multiple_of(src_ref[tile * tm + r], SUBLANES)
        dst = pl.multiple_of(r * SUBLANES, SUBLANES)
        pltpu.make_async_copy(h_hbm.at[pl.ds(src, SUBLANES)], xg_ref.at[dst_buf, pl.ds(dst, SUBLANES)],
                              sems.at[dst_buf]).start()

    def wait_rows(b_):
        pltpu.make_async_copy(h_hbm.at[pl.ds(0, tm * SUBLANES)], xg_ref.at[b_], sems.at[b_]).wait()

    @pl.when((t == 0) & (j == 0))
    def _():
        def issue(r, carry):
            start_row(0, 0, r)
            return carry

        lax.fori_loop(0, tm, issue, 0, unroll=8)

    @pl.when((j == 0) & ((t == 0) | (t - 1 < n_used)))
    def _():
        wait_rows(buf)
        xb_ref[...] = _load_token_major(xg_ref.at[buf], tm).astype(BF16)

    @pl.when(j == 0)
    def _():
        acc_ref[...] = jnp.zeros_like(acc_ref)

    @pl.when(used)
    def _():
        nxt = jnp.minimum(t + 1, n_tiles - 1)
        for r in range(rows_per_step):
            start_row(nxt, 1 - buf, j * rows_per_step + r)
        h = xb_ref[...]
        a = jnp.dot(h, wg_ref[...].astype(BF16), preferred_element_type=F32)
        u = jnp.dot(h, wu_ref[...].astype(BF16), preferred_element_type=F32)
        act = (_silu(a) * u).astype(BF16)
        acc_ref[...] += jnp.dot(act, wd_ref[...].astype(BF16), preferred_element_type=F32)

    @pl.when(j == n_steps - 1)
    def _():
        _store_token_major(o_ref, acc_ref[...])

    @pl.when(used & (t == n_tiles - 1) & (j == n_steps - 1))
    def _():
        wait_rows(1 - buf)


def _moe_ff_tile(dff):
    return MOE_FF_TILE if dff % MOE_FF_TILE == 0 else dff


def _moe_tile_rows(m, dff):
    per_expert = 2.0 * m / N_EXPERTS
    unit = (dff // _moe_ff_tile(dff)) * 2 * SUBLANES
    rows = (per_expert + 3.0 * math.sqrt(per_expert)) / MOE_TILES_PER_EXPERT
    return int(math.ceil(rows / unit)) * unit


def _experts(h2, tables, wg, wu, wd, layer, tm):
    d, dff = wg.shape[-2:]
    assert d == SUBLANES * LANES
    tile_e, n_used, src, _ = tables
    n_tiles = tile_e.shape[0]
    tf = MOE_FF_TILE if dff % MOE_FF_TILE == 0 else dff
    nj = dff // tf
    assert tm % nj == 0
    col = lambda t, j, te, nu, src: (layer, te[t], 0, jnp.where(t < nu[0], j, nj - 1))
    rowb = lambda t, j, te, nu, src: (layer, te[t], jnp.where(t < nu[0], j, nj - 1), 0)
    return pl.pallas_call(
        functools.partial(_experts_kernel, n_tiles=n_tiles, n_steps=nj),
        grid_spec=pltpu.PrefetchScalarGridSpec(
            num_scalar_prefetch=3,
            grid=(n_tiles, nj),
            in_specs=[pl.BlockSpec(memory_space=pl.ANY),
                      pl.BlockSpec((None, None, d, tf), col),
                      pl.BlockSpec((None, None, d, tf), col),
                      pl.BlockSpec((None, None, tf, d), rowb)],
            out_specs=pl.BlockSpec((tm * SUBLANES, LANES), lambda t, j, te, nu, src: (t, 0)),
            scratch_shapes=[pltpu.VMEM((2, tm * SUBLANES, LANES), F32), pltpu.VMEM((tm, d), BF16),
                            pltpu.VMEM((tm, d), F32), pltpu.SemaphoreType.DMA((2,))]),
        out_shape=jax.ShapeDtypeStruct((n_tiles * tm * SUBLANES, LANES), F32),
        compiler_params=_params(("arbitrary", "arbitrary")),
        name="moe_experts",
    )(tile_e, n_used, src, h2, wg, wu, wd)


def _combine_kernel(slot_ref, x_ref, sel_ref, mod_ref, g_ref, y_hbm, o_ref, r1_ref, r2_ref, sems,
                    *, d, m, rows_per_sample, n_steps, final):
    t = x_ref.shape[0]
    n_r = rows_per_sample // t
    step = pl.program_id(0) * n_r + pl.program_id(1)
    buf = step % 2

    def start_tile(tile, dst_buf):
        base = tile * t

        def issue(k, carry):
            dst = pl.ds(pl.multiple_of(k * SUBLANES, SUBLANES), SUBLANES)
            s1 = pl.multiple_of(slot_ref[base + k], SUBLANES)
            s2 = pl.multiple_of(slot_ref[m + base + k], SUBLANES)
            pltpu.make_async_copy(y_hbm.at[pl.ds(s1, SUBLANES)], r1_ref.at[dst_buf, dst], sems.at[dst_buf, 0]).start()
            pltpu.make_async_copy(y_hbm.at[pl.ds(s2, SUBLANES)], r2_ref.at[dst_buf, dst], sems.at[dst_buf, 1]).start()
            return carry

        lax.fori_loop(0, t, issue, 0, unroll=8)

    @pl.when(step == 0)
    def _():
        start_tile(0, 0)

    pltpu.make_async_copy(y_hbm.at[pl.ds(0, t * SUBLANES)], r1_ref.at[buf], sems.at[buf, 0]).wait()
    pltpu.make_async_copy(y_hbm.at[pl.ds(0, t * SUBLANES)], r2_ref.at[buf], sems.at[buf, 1]).wait()

    @pl.when(step + 1 < n_steps)
    def _():
        start_tile(step + 1, 1 - buf)

    y = (sel_ref[:, 0:1] * _load_token_major(r1_ref.at[buf], t)
         + sel_ref[:, 1:2] * _load_token_major(r2_ref.at[buf], t))
    x = x_ref[...] + mod_ref[:, 5 * d:6 * d] * y
    if final:
        x = x * lax.rsqrt(jnp.mean(x * x, axis=-1, keepdims=True) + EPS) * g_ref[...]
    o_ref[...] = x


def _combine(xs, sel, mod_l, y_sorted, slot_of, final_g, n_lat_tiles, n_row_tiles, final):
    b, _, d = xs.shape
    t = ROW_TILE
    rows = n_row_tiles * t
    row = lambda i, r, s: (i, r, 0)
    return pl.pallas_call(
        functools.partial(_combine_kernel, d=d, m=b * rows, rows_per_sample=rows, n_steps=b * n_row_tiles,
                          final=final),
        grid_spec=pltpu.PrefetchScalarGridSpec(
            num_scalar_prefetch=1,
            grid=(b, n_row_tiles),
            in_specs=[pl.BlockSpec((None, t, d), row),
                      pl.BlockSpec((None, t, LANES), row),
                      pl.BlockSpec((None, 1, mod_l.shape[-1]),
                                   lambda i, r, s: (jnp.where(r < n_lat_tiles, i, b), 0, 0)),
                      pl.BlockSpec((1, d), lambda i, r, s: (0, 0)),
                      pl.BlockSpec(memory_space=pl.ANY)],
            out_specs=pl.BlockSpec((None, t, d), row),
            scratch_shapes=[pltpu.VMEM((2, t * SUBLANES, LANES), F32), pltpu.VMEM((2, t * SUBLANES, LANES), F32),
                            pltpu.SemaphoreType.DMA((2, 2))]),
        out_shape=jax.ShapeDtypeStruct((b, rows, d), F32),
        compiler_params=_params(("arbitrary", "arbitrary")),
        name="moe_combine",
    )(slot_of, xs, sel, mod_l, final_g, y_sorted)


def _rope_tables(s_lat, n_ctx):
    rows = s_lat // GRID_W
    pos_row = jnp.repeat(jnp.arange(rows, dtype=jnp.int32), GRID_W).astype(F32)
    pos_col = jnp.tile(jnp.arange(GRID_W, dtype=jnp.int32), rows).astype(F32)
    quarter = HEAD_DIM // 4
    inv_freq = ROPE_THETA ** (-jnp.arange(quarter, dtype=F32) / quarter)
    ang_r = pos_row[:, None] * inv_freq
    ang_c = pos_col[:, None] * inv_freq
    ang = jnp.concatenate([ang_r, ang_r, ang_c, ang_c], axis=-1)
    cos, sin = jnp.cos(ang), jnp.sin(ang)
    first = (np.arange(HEAD_DIM) % (2 * quarter)) < quarter
    sin_lo = jnp.where(first, -sin, 0.0)
    sin_hi = jnp.where(first, 0.0, sin)

    def widen(tab, ctx_fill):
        tab = jnp.concatenate([tab, jnp.full((n_ctx, HEAD_DIM), ctx_fill, F32)], axis=0)
        return jnp.tile(tab, (1, LANES // HEAD_DIM))

    return widen(cos, 1.0), widen(sin_lo, 0.0), widen(sin_hi, 0.0)


def _dft_tables(n):
    k = np.arange(n, dtype=np.int64)
    ang = 2.0 * np.pi * ((k[:, None] * k[None, :]) % n).astype(np.float64) / n
    return np.cos(ang), np.sin(ang)


def _constants(s_lat, n_ctx):
    c64, s64 = _dft_tables(F_GROUP_DIM)
    eye = np.eye(F_GROUPS)
    cn, sn = _dft_tables(s_lat)
    cc, sc = _dft_tables(n_ctx)
    bf = lambda a: jnp.asarray(a, dtype=F32).astype(BF16)
    pad = np.zeros((n_ctx, s_lat))
    cn, sn = np.concatenate([cn, pad]), np.concatenate([sn, pad])
    return dict(
        gmat=bf(np.kron(np.eye(512 // HEAD_DIM), np.ones((HEAD_DIM, HEAD_DIM)))),
        c64=bf(np.kron(eye, c64) / math.sqrt(F_GROUP_DIM)),
        s64=bf(np.kron(eye, s64) / math.sqrt(F_GROUP_DIM)),
        cn=bf(cn / math.sqrt(s_lat)), sn=bf(sn / math.sqrt(s_lat)),
        cc=bf(cc / math.sqrt(n_ctx)), sc=bf(sc / math.sqrt(n_ctx)),
    )


def _dup_heads(w, off, n_heads):
    cols = []
    for hd in range(n_heads):
        blk = w[:, off + hd * HEAD_DIM: off + (hd + 1) * HEAD_DIM]
        cols += [blk, blk]
    return cols


def _relayout_w_in(w):
    cols = [w[:, _OFF["aq"]:_OFF["bq"]], w[:, _OFF["bq"]:_OFF["bk"]],
            w[:, _OFF["cq"]:_OFF["ck"]], w[:, _OFF["fz"]:_OFF["gt"]]]
    cols += _dup_heads(w, _OFF["bk"], B_KV) + _dup_heads(w, _OFF["bv"], B_KV)
    cols += _dup_heads(w, _OFF["ck"], C_KV) + _dup_heads(w, _OFF["cv"], C_KV)
    return jnp.concatenate(cols, axis=1).astype(BF16)


def kernel(x, c, ctx, c_ctx, w_mod, b_mod, norm1_g, norm2_g, w_in, diff_lambda, diff_subln_g, qk_norm_g,
           sink_logits, w_branch, w_out, ffn_w_gate, ffn_w_up, ffn_w_down, router_w, moe_w_gate, moe_w_up,
           moe_w_down, final_g):
    b, s_lat, d = x.shape
    n_ctx = ctx.shape[1]
    depth = w_mod.shape[0]
    nt = s_lat + n_ctx
    t = ROW_TILE
    assert s_lat % t == 0 and n_ctx % t == 0 and s_lat % n_ctx == 0 and b < MOD_ROWS
    assert w_in.shape[-1] == _OFF["end"] and d * N_BRANCH == _OFF["end"] - _OFF["gt"]
    n_lat_tiles = s_lat // t

    xs = jnp.concatenate([x, ctx], axis=1)
    cvec = jnp.concatenate([c, c_ctx[None], jnp.zeros((MOD_ROWS - b - 1, d), F32)], axis=0)
    mod = _modulation(cvec, w_mod, b_mod).reshape(depth, MOD_ROWS, 1, 6 * d)
    tabs = _rope_tables(s_lat, n_ctx)
    consts = _constants(s_lat, n_ctx)

    out = None
    for l in range(depth):
        last = l == depth - 1
        mod_l = mod[l]
        p = _inproj(xs, mod_l, norm1_g[l][None], _relayout_w_in(w_in[l]), tabs,
                    jnp.tile(qk_norm_g[l, 0], 512 // HEAD_DIM)[None], jnp.tile(qk_norm_g[l, 1], 256 // HEAD_DIM)[None],
                    consts, s_lat)
        p = (None, *p)
        w_gates = w_in[l][:, _OFF["gt"]:_OFF["end"]].astype(BF16)
        lam_init = 0.8 - 0.6 * math.exp(-0.3 * l)
        n_q_tiles = n_lat_tiles if last else nt // t
        oa, ob, oc = _attention(p, diff_lambda[l], diff_subln_g[l][None], sink_logits[l],
                                n_lat_tiles, n_q_tiles, s_lat, lam_init)
        od = _fourier(p[10], p[11], consts, s_lat, n_q_tiles * t)
        xs = _merge(xs, mod_l, norm1_g[l][None], w_gates, (oa, ob, oc, od), w_branch[l].astype(BF16),
                    w_out[l].astype(BF16), s_lat, n_q_tiles * t)

        i = l // 2
        if l % 2 == 0:
            assert not last
            xs = _ffn(xs, mod_l, norm2_g[l][None], ffn_w_gate[i].astype(BF16), ffn_w_up[i].astype(BF16),
                      ffn_w_down[i].astype(BF16), s_lat)
        else:
            w_r = jnp.pad(router_w[i], ((0, 0), (0, LANES - N_EXPERTS)))
            n_row_tiles = n_q_tiles
            m = b * n_row_tiles * t
            h2, sel = _router(xs, mod_l, norm2_g[l][None], w_r, s_lat, n_row_tiles * t)
            tm = _moe_tile_rows(m, moe_w_gate.shape[-1])
            tables = _dispatch(sel.reshape(m, LANES), tm)
            y_sorted = _experts(h2.reshape(m * SUBLANES, LANES), tables, moe_w_gate, moe_w_up, moe_w_down, i, tm)
            res = _combine(xs, sel, mod_l, y_sorted, tables[3], final_g[None], n_lat_tiles, n_row_tiles, last)
            if last:
                out = res
            else:
                xs = res
    if out is None:
        raise NotImplementedError("final normalisation is fused into the last (mixture-of-experts) layer")
    return out
```

```python
import functools
import math

import numpy as np
import jax
import jax.numpy as jnp
from jax import lax
from jax.experimental import pallas as pl
from jax.experimental.pallas import tpu as pltpu

F32 = jnp.float32
BF16 = jnp.bfloat16

GRID_W = 64
HEAD_DIM = 64
ROPE_THETA = 10000.0
WINDOW = 128
A_HEADS = 4
B_HEADS = 8
B_KV = 2
C_HEADS = 8
C_KV = 2
F_GROUPS = 8
F_GROUP_DIM = 64
BRANCH_W = 512
N_BRANCH = 4
N_EXPERTS = 8
EPS = 1e-6
SUBLN_EPS = 1e-5
NEG_INF = -1e30
LOG2E = math.log2(math.e)

LANES = 128
SUBLANES = 8
ROW_TILE = 256
WIDE_TILES = (768, 512, 256)
FFN_CHUNK = 768
MOE_TILES_PER_EXPERT = 4
MOE_FF_TILE = 512
MOD_ROWS = 16
VMEM_LIMIT = 56 * 1024 * 1024

_OFF = dict(aq=0, ak=512, av=1024, bq=1536, bk=2048, bv=2176, cq=2304, ck=2816, cv=2944,
            fz=3072, gt=3584, end=7680)
_P = dict(aq=0, ak=512, av=1024, bq=1536, cq=2048, fz=2560, bk=3072, bv=3328, ck=3584, cv=3840, end=4096)


def _params(sem):
    return pltpu.CompilerParams(dimension_semantics=sem, vmem_limit_bytes=VMEM_LIMIT)


def _full(shape):
    n = len(shape)
    return pl.BlockSpec(shape, lambda *_: (0,) * n)


def _silu(x):
    return x / (1.0 + jnp.exp(-x))


def _sigmoid(x):
    return 1.0 / (1.0 + jnp.exp(-x))


def _adaln(x, g, shift, scale):
    y = x * lax.rsqrt(jnp.mean(x * x, axis=-1, keepdims=True) + EPS) * g
    return y * (1.0 + scale) + shift


def _mod_kernel(c_ref, w_ref, b_ref, o_ref):
    s = _silu(c_ref[...]).astype(BF16)
    o_ref[...] = jnp.dot(s, w_ref[...].astype(BF16), preferred_element_type=F32) + b_ref[...]


def _modulation(cvec, w_mod, b_mod):
    depth, d, n = w_mod.shape
    tn = 1536 if n % 1536 == 0 else n
    return pl.pallas_call(
        _mod_kernel,
        grid=(depth, n // tn),
        in_specs=[_full((MOD_ROWS, d)),
                  pl.BlockSpec((None, d, tn), lambda l, j: (l, 0, j)),
                  pl.BlockSpec((None, 1, tn), lambda l, j: (l, 0, j))],
        out_specs=pl.BlockSpec((None, MOD_ROWS, tn), lambda l, j: (l, 0, j)),
        out_shape=jax.ShapeDtypeStruct((depth, MOD_ROWS, n), F32),
        compiler_params=_params(("arbitrary", "arbitrary")),
        name="modulation",
    )(cvec, w_mod, b_mod.reshape(depth, 1, n))


def _rope(p, cos, sin_lo, sin_hi):
    outs = []
    for c in range(p.shape[1] // LANES):
        xc = p[:, c * LANES:(c + 1) * LANES]
        outs.append(xc * cos + pltpu.roll(xc, LANES - 16, 1) * sin_lo + pltpu.roll(xc, 16, 1) * sin_hi)
    return outs[0] if len(outs) == 1 else jnp.concatenate(outs, axis=1)


def _group_rmsnorm(y, gmat, g):
    ss = jnp.dot((y * y).astype(BF16), gmat, preferred_element_type=F32)
    return y * lax.rsqrt(ss * (1.0 / HEAD_DIM) + EPS) * g


def _inproj_kernel(x_ref, modb_ref, modc_ref, g_ref, w_ref, cos_ref, slo_ref, shi_ref, qg_ref, kg_ref, gmat_ref,
                   c64_ref, s64_ref,
                   aq_ref, akt_ref, av_ref, bq_ref, bkt_ref, bv_ref, cq_ref, ckt_ref, cv_ref,
                   fzc_ref, fzs_ref, *, d, s_lat):
    mod = _row_mod(modb_ref, modc_ref, x_ref.shape[0], s_lat, d)
    h = _adaln(x_ref[...], g_ref[...], mod(0), mod(1)).astype(BF16)
    cos, slo, shi = cos_ref[...], slo_ref[...], shi_ref[...]
    scale = HEAD_DIM ** -0.5 * LOG2E

    def proj(name, width):
        return jnp.dot(h, w_ref[:, _P[name]:_P[name] + width], preferred_element_type=F32)

    aq_ref[...] = (_rope(proj("aq", 512), cos, slo, shi) * scale).astype(BF16)
    akt_ref[...] = _rope(proj("ak", 512), cos, slo, shi).T.astype(BF16)
    av_ref[...] = proj("av", 512).astype(BF16)

    bq = _group_rmsnorm(proj("bq", 512), gmat_ref[...], qg_ref[...])
    bq_ref[...] = (_rope(bq, cos, slo, shi) * scale).astype(BF16)
    bk = _group_rmsnorm(proj("bk", 256), gmat_ref[0:256, 0:256], kg_ref[...])
    bkt_ref[...] = _rope(bk, cos, slo, shi).T.astype(BF16)
    bv_ref[...] = proj("bv", 256).astype(BF16)

    cq_ref[...] = (_rope(proj("cq", 512), cos, slo, shi) * scale).astype(BF16)
    ckt_ref[...] = _rope(proj("ck", 256), cos, slo, shi).T.astype(BF16)
    cv_ref[...] = proj("cv", 256).astype(BF16)

    fz = proj("fz", 512).astype(BF16)
    fzc_ref[...] = jnp.dot(fz, c64_ref[...], preferred_element_type=F32).astype(BF16)
    fzs_ref[...] = jnp.dot(fz, s64_ref[...], preferred_element_type=F32).astype(BF16)


def _inproj(xs, mod_l, g, w_p, tabs, qg, kg, consts, s_lat):
    b, nt, d = xs.shape
    t = _wide_tile(nt)
    row = lambda i, r: (i, r, 0)
    tr = lambda i, r: (i, 0, r)
    tab = pl.BlockSpec((t, LANES), lambda i, r: (r, 0))

    def out(width):
        return pl.BlockSpec((None, t, width), row), jax.ShapeDtypeStruct((b, nt, width), BF16)

    def out_t(width):
        return pl.BlockSpec((None, width, t), tr), jax.ShapeDtypeStruct((b, width, nt), BF16)

    outs = [out(512), out_t(512), out(512), out(512), out_t(256), out(256),
            out(512), out_t(256), out(256), out(512), out(512)]
    return pl.pallas_call(
        functools.partial(_inproj_kernel, d=d, s_lat=s_lat),
        grid=(b, nt // t),
        in_specs=[pl.BlockSpec((None, t, d), row), *_wide_mod_specs(mod_l, b),
                  _full((1, d)),
                  pl.BlockSpec(w_p.shape, lambda i, r: (0, 0), pipeline_mode=pl.Buffered(1)),
                  tab, tab, tab,
                  _full((1, 512)), _full((1, 256)), _full((512, 512)), _full((512, 512)), _full((512, 512))],
        out_specs=[o[0] for o in outs],
        out_shape=[o[1] for o in outs],
        compiler_params=_params(("arbitrary", "arbitrary")),
        name="inproj",
    )(xs, mod_l, mod_l, g, w_p, tabs[0], tabs[1], tabs[2], qg, kg, consts["gmat"], consts["c64"], consts["s64"])


def _half_masks(shape):
    lane = lax.broadcasted_iota(jnp.int32, shape, 1)
    return lane < HEAD_DIM, lane >= HEAD_DIM


def _exp2_bf16(s):
    return jnp.exp2(s - jnp.max(s, axis=-1, keepdims=True)).astype(BF16)


def _pipelined_scores(score_fns):
    nxt = score_fns[0]()
    for i in range(len(score_fns)):
        cur = nxt
        if i + 1 < len(score_fns):
            nxt = score_fns[i + 1]()
        yield cur


def _attn_a_kernel(q_ref, kt_ref, v_ref, lp_ref, g_ref, o_ref, *, n_lat_tiles, s_lat, lam_init):
    r = pl.program_id(1)
    lp = lp_ref[...]
    lam = (jnp.exp(jnp.sum(lp[0:1] * lp[1:2], axis=-1, keepdims=True))
           - jnp.exp(jnp.sum(lp[2:3] * lp[3:4], axis=-1, keepdims=True)) + lam_init)
    g = g_ref[...] * (1.0 - lam_init)
    nt = kt_ref.shape[1]

    def body(k0):
        lo, hi = _half_masks((q_ref.shape[0], LANES))
        ones = jnp.ones((nt - k0, LANES), BF16)

        def score_fn(hd, msk):
            cs = slice(hd * LANES, (hd + 1) * LANES)
            return lambda: jnp.dot(jnp.where(msk, q_ref[:, cs], 0), kt_ref[cs, k0:nt], preferred_element_type=F32)

        fns = [score_fn(hd, msk) for hd in range(A_HEADS) for msk in (lo, hi)]
        outs = []
        for i, sc in enumerate(_pipelined_scores(fns)):
            hd = i // 2
            cs = slice(hd * LANES, (hd + 1) * LANES)
            vo = jnp.concatenate([v_ref[k0:nt, cs], ones], axis=1)
            full = jnp.dot(_exp2_bf16(sc), vo, preferred_element_type=F32)
            outs.append(full[:, 0:LANES] * (1.0 / full[:, LANES:LANES + 1]))
            if len(outs) == 2:
                o = outs[0] - lam * outs[1]
                outs = []
                o = o * lax.rsqrt(jnp.mean(o * o, axis=-1, keepdims=True) + SUBLN_EPS) * g
                o_ref[:, cs] = o.astype(BF16)

    pl.when(r < n_lat_tiles)(lambda: body(0))
    pl.when(r >= n_lat_tiles)(lambda: body(s_lat))


def _attn_b_kernel(q_ref, kt_ref, v_ref, o_ref, *, n_lat_tiles, s_lat):
    r = pl.program_id(1)
    nt = kt_ref.shape[1]

    def body(k0):
        lo, hi = _half_masks((q_ref.shape[0], LANES))
        ones = jnp.ones((nt - k0, LANES), BF16)

        def score_fn(pair, msk):
            kv = pair // (B_HEADS // B_KV // 2)
            q = q_ref[:, pair * LANES:(pair + 1) * LANES]
            return lambda: jnp.dot(jnp.where(msk, q, 0), kt_ref[kv * LANES:(kv + 1) * LANES, k0:nt],
                                   preferred_element_type=F32)

        fns = [score_fn(pair, msk) for pair in range(B_HEADS // 2) for msk in (lo, hi)]
        halves = []
        for i, sc in enumerate(_pipelined_scores(fns)):
            pair = i // 2
            kv = pair // (B_HEADS // B_KV // 2)
            vo = jnp.concatenate([v_ref[k0:nt, kv * LANES:(kv + 1) * LANES], ones], axis=1)
            full = jnp.dot(_exp2_bf16(sc), vo, preferred_element_type=F32)
            halves.append(full[:, 0:LANES] * (1.0 / full[:, LANES:LANES + 1]))
            if len(halves) == 2:
                o_ref[:, pair * LANES:(pair + 1) * LANES] = jnp.where(lo, halves[0], halves[1]).astype(BF16)
                halves = []

    pl.when(r < n_lat_tiles)(lambda: body(0))
    pl.when(r >= n_lat_tiles)(lambda: body(s_lat))


def _attn_c_kernel(sink_ref, q_ref, ktl_ref, ktm_ref, ktr_ref, ktc_ref, vl_ref, vm_ref, vr_ref, vc_ref, o_ref,
                   *, n_lat_tiles, s_lat):
    r = pl.program_id(1)
    t = q_ref.shape[0]
    n_band = t + 2 * WINDOW
    n_keys = n_band + ktc_ref.shape[1]
    u = lax.broadcasted_iota(jnp.int32, (t, n_keys), 1)
    row = lax.broadcasted_iota(jnp.int32, (t, n_keys), 0)
    j = r * t - WINDOW + u
    in_band = (u >= row) & (u <= row + 2 * WINDOW) & (j >= 0) & (j < s_lat) & (r < n_lat_tiles)
    valid = in_band | (u >= n_band)
    lo, hi = _half_masks((t, LANES))
    ones = jnp.ones((n_keys, LANES), BF16)
    kts, vos = [], []
    for kv in range(C_KV):
        ks = slice(kv * LANES, (kv + 1) * LANES)
        kts.append(jnp.concatenate([ktl_ref[ks, :], ktm_ref[ks, :], ktr_ref[ks, :], ktc_ref[ks, :]], axis=1))
        v = jnp.concatenate([vl_ref[:, ks], vm_ref[:, ks], vr_ref[:, ks], vc_ref[:, ks]], axis=0)
        vos.append(jnp.concatenate([v, ones], axis=1))

    def score_fn(pair, msk):
        kv = pair // (C_HEADS // C_KV // 2)
        q = q_ref[:, pair * LANES:(pair + 1) * LANES]
        return lambda: jnp.dot(jnp.where(msk, q, 0), kts[kv], preferred_element_type=F32)

    fns = [score_fn(pair, msk) for pair in range(C_HEADS // 2) for msk in (lo, hi)]
    halves = []
    for i, sc in enumerate(_pipelined_scores(fns)):
        pair = i // 2
        kv = pair // (C_HEADS // C_KV // 2)
        sink = sink_ref[i] * LOG2E
        sc = jnp.where(valid, sc, NEG_INF)
        m = jnp.maximum(jnp.max(sc, axis=-1, keepdims=True), sink)
        full = jnp.dot(jnp.exp2(sc - m).astype(BF16), vos[kv], preferred_element_type=F32)
        l = full[:, LANES:LANES + 1] + jnp.exp2(sink - m)
        halves.append(full[:, 0:LANES] * (1.0 / l))
        if len(halves) == 2:
            o_ref[:, pair * LANES:(pair + 1) * LANES] = jnp.where(lo, halves[0], halves[1]).astype(BF16)
            halves = []


def _attention(p, lp, subln_g, sink, n_lat_tiles, n_q_tiles, s_lat, lam_init):
    (_, aq, akt, av, bq, bkt, bv, cq, ckt, cv, _, _) = p
    b, nt, _ = aq.shape
    t = ROW_TILE
    grid = (b, n_q_tiles)
    row = lambda i, r: (i, r, 0)
    whole = lambda i, r: (i, 0, 0)
    sem = _params(("arbitrary", "arbitrary"))
    o_spec = pl.BlockSpec((None, t, 512), row)
    o_shape = jax.ShapeDtypeStruct((b, n_q_tiles * t, 512), BF16)

    oa = pl.pallas_call(
        functools.partial(_attn_a_kernel, n_lat_tiles=n_lat_tiles, s_lat=s_lat, lam_init=lam_init),
        grid=grid,
        in_specs=[pl.BlockSpec((None, t, 512), row), pl.BlockSpec((None, 512, nt), whole),
                  pl.BlockSpec((None, nt, 512), whole), _full((4, HEAD_DIM)), _full((1, 2 * HEAD_DIM))],
        out_specs=o_spec, out_shape=o_shape, compiler_params=sem, name="attn_diff",
    )(aq, akt, av, lp, subln_g)

    ob = pl.pallas_call(
        functools.partial(_attn_b_kernel, n_lat_tiles=n_lat_tiles, s_lat=s_lat),
        grid=grid,
        in_specs=[pl.BlockSpec((None, t, 512), row), pl.BlockSpec((None, 256, nt), whole),
                  pl.BlockSpec((None, nt, 256), whole)],
        out_specs=o_spec, out_shape=o_shape, compiler_params=sem, name="attn_gqa",
    )(bq, bkt, bv)

    wpt = t // WINDOW
    last = s_lat // WINDOW - 1
    left = lambda r: jnp.clip(r * wpt - 1, 0, last)
    mid = lambda r: jnp.minimum(r, n_lat_tiles - 1)
    right = lambda r: jnp.clip((r + 1) * wpt, 0, last)
    n_ctx = nt - s_lat
    ctx_blk = s_lat // n_ctx
    oc = pl.pallas_call(
        functools.partial(_attn_c_kernel, n_lat_tiles=n_lat_tiles, s_lat=s_lat),
        grid=grid,
        in_specs=[pl.BlockSpec(memory_space=pltpu.SMEM),
                  pl.BlockSpec((None, t, 512), row),
                  pl.BlockSpec((None, 256, WINDOW), lambda i, r: (i, 0, left(r))),
                  pl.BlockSpec((None, 256, t), lambda i, r: (i, 0, mid(r))),
                  pl.BlockSpec((None, 256, WINDOW), lambda i, r: (i, 0, right(r))),
                  pl.BlockSpec((None, 256, n_ctx), lambda i, r: (i, 0, ctx_blk)),
                  pl.BlockSpec((None, WINDOW, 256), lambda i, r: (i, left(r), 0)),
                  pl.BlockSpec((None, t, 256), lambda i, r: (i, mid(r), 0)),
                  pl.BlockSpec((None, WINDOW, 256), lambda i, r: (i, right(r), 0)),
                  pl.BlockSpec((None, n_ctx, 256), lambda i, r: (i, ctx_blk, 0))],
        out_specs=o_spec, out_shape=o_shape, compiler_params=sem, name="attn_window",
    )(sink, cq, ckt, ckt, ckt, ckt, cv, cv, cv, cv)
    return oa, ob, oc


def _fourier_kernel(cn_ref, sn_ref, cc_ref, sc_ref, zc_ref, zs_ref, o_ref, *, s_lat, rows):
    r = pl.program_id(1)
    t = o_ref.shape[0]
    nt = zc_ref.shape[0]
    n_full = s_lat // t
    lat_rows = s_lat - n_full * t

    def latent(n):
        return (jnp.dot(cn_ref[0:n, :], zc_ref[0:s_lat, :], preferred_element_type=F32)
                - jnp.dot(sn_ref[0:n, :], zs_ref[0:s_lat, :], preferred_element_type=F32)).astype(BF16)

    @pl.when(r < n_full)
    def _():
        o_ref[...] = latent(t)

    if rows > s_lat:
        @pl.when(r >= n_full)
        def _():
            if lat_rows:
                o_ref[0:lat_rows, :] = latent(lat_rows)
            y = (jnp.dot(cc_ref[...], zc_ref[s_lat:nt, :], preferred_element_type=F32)
                 - jnp.dot(sc_ref[...], zs_ref[s_lat:nt, :], preferred_element_type=F32))
            o_ref[lat_rows:t, :] = y.astype(BF16)


def _fourier(fzc, fzs, consts, s_lat, rows):
    b, nt, w = fzc.shape
    t = _wide_tile(rows)
    n_ctx = nt - s_lat
    assert rows == s_lat or (rows == nt and nt - s_lat // t * t == t)
    tile = lambda i, r: (r, 0)
    whole = lambda i, r: (i, 0, 0)
    return pl.pallas_call(
        functools.partial(_fourier_kernel, s_lat=s_lat, rows=rows),
        grid=(b, rows // t),
        in_specs=[pl.BlockSpec((t, s_lat), tile), pl.BlockSpec((t, s_lat), tile),
                  _full((n_ctx, n_ctx)), _full((n_ctx, n_ctx)),
                  pl.BlockSpec((None, nt, w), whole), pl.BlockSpec((None, nt, w), whole)],
        out_specs=pl.BlockSpec((None, t, w), lambda i, r: (i, r, 0)),
        out_shape=jax.ShapeDtypeStruct((b, rows, w), BF16),
        compiler_params=_params(("arbitrary", "arbitrary")),
        name="fourier",
    )(consts["cn"], consts["sn"], consts["cc"], consts["sc"], fzc, fzs)


def _wide_tile(rows):
    for t in WIDE_TILES:
        if rows % t == 0:
            return t
    raise ValueError(rows)


def _wide_mod_specs(mod_l, b):
    n = mod_l.shape[-1]
    return [pl.BlockSpec((None, 1, n), lambda i, r: (i, 0, 0)), pl.BlockSpec((None, 1, n), lambda i, r: (b, 0, 0))]


def _row_mod(modb_ref, modc_ref, rows, s_lat, d):
    is_lat = pl.program_id(1) * rows + lax.broadcasted_iota(jnp.int32, (rows, 1), 0) < s_lat
    return lambda k: jnp.where(is_lat, modb_ref[:, k * d:(k + 1) * d], modc_ref[:, k * d:(k + 1) * d])


def _merge_kernel(x_ref, modb_ref, modc_ref, g_ref, wgt_ref, oa_ref, ob_ref, oc_ref, od_ref, wbr_ref, wo_ref,
                  o_ref, *, d, s_lat):
    mod = _row_mod(modb_ref, modc_ref, x_ref.shape[0], s_lat, d)
    x = x_ref[...]
    h = _adaln(x, g_ref[...], mod(0), mod(1)).astype(BF16)
    merged = None
    for i, br in enumerate((oa_ref, ob_ref, oc_ref, od_ref)):
        gate = _sigmoid(jnp.dot(h, wgt_ref[:, i * d:(i + 1) * d], preferred_element_type=F32))
        y = gate * jnp.dot(br[...], wbr_ref[i], preferred_element_type=F32)
        merged = y if merged is None else merged + y
    out = jnp.dot(merged.astype(BF16), wo_ref[...], preferred_element_type=F32)
    o_ref[...] = x + mod(2) * out


def _merge(xs, mod_l, g, w_gates, branches, w_br, w_o, s_lat, rows):
    b, _, d = xs.shape
    t = _wide_tile(rows)
    row = lambda i, r: (i, r, 0)
    br_spec = pl.BlockSpec((None, t, BRANCH_W), row)
    const = lambda shape: pl.BlockSpec(shape, lambda i, r: (0,) * len(shape), pipeline_mode=pl.Buffered(1))
    return pl.pallas_call(
        functools.partial(_merge_kernel, d=d, s_lat=s_lat),
        grid=(b, rows // t),
        in_specs=[pl.BlockSpec((None, t, d), row), *_wide_mod_specs(mod_l, b),
                  _full((1, d)), const(w_gates.shape),
                  br_spec, br_spec, br_spec, br_spec,
                  const(w_br.shape), const(w_o.shape)],
        out_specs=pl.BlockSpec((None, t, d), row),
        out_shape=jax.ShapeDtypeStruct((b, rows, d), F32),
        compiler_params=_params(("arbitrary", "arbitrary")),
        name="merge",
    )(xs, mod_l, mod_l, g, w_gates, *branches, w_br, w_o)


def _ffn_kernel(x_ref, modb_ref, modc_ref, g_ref, wg_ref, wu_ref, wd_ref, o_ref, *, d, s_lat):
    mod = _row_mod(modb_ref, modc_ref, x_ref.shape[0], s_lat, d)
    x = x_ref[...]
    h = _adaln(x, g_ref[...], mod(3), mod(4)).astype(BF16)
    dff = wg_ref.shape[1]
    y = None
    for c0 in range(0, dff, FFN_CHUNK):
        c1 = min(c0 + FFN_CHUNK, dff)
        a = jnp.dot(h, wg_ref[:, c0:c1], preferred_element_type=F32)
        u = jnp.dot(h, wu_ref[:, c0:c1], preferred_element_type=F32)
        part = jnp.dot((_silu(a) * u).astype(BF16), wd_ref[c0:c1, :], preferred_element_type=F32)
        y = part if y is None else y + part
    o_ref[...] = x + mod(5) * y


def _ffn(xs, mod_l, g, wg, wu, wd, s_lat):
    b, nt, d = xs.shape
    t = _wide_tile(nt)
    row = lambda i, r: (i, r, 0)
    const = lambda shape: pl.BlockSpec(shape, lambda i, r: (0, 0), pipeline_mode=pl.Buffered(1))
    return pl.pallas_call(
        functools.partial(_ffn_kernel, d=d, s_lat=s_lat),
        grid=(b, nt // t),
        in_specs=[pl.BlockSpec((None, t, d), row), *_wide_mod_specs(mod_l, b),
                  _full((1, d)), const(wg.shape), const(wu.shape), const(wd.shape)],
        out_specs=pl.BlockSpec((None, t, d), row),
        out_shape=jax.ShapeDtypeStruct(xs.shape, F32),
        compiler_params=_params(("arbitrary", "arbitrary")),
        name="ffn_dense",
    )(xs, mod_l, mod_l, g, wg, wu, wd)


def _store_token_major(ref, val):
    rows = val.shape[0]
    for c in range(SUBLANES):
        ref[pl.ds(c, rows, stride=SUBLANES), :] = val[:, c * LANES:(c + 1) * LANES]


def _load_token_major(ref, rows):
    return jnp.concatenate([ref[pl.ds(c, rows, stride=SUBLANES), :] for c in range(SUBLANES)], axis=1)


def _router_kernel(x_ref, modb_ref, modc_ref, g_ref, wr_ref, h_ref, sel_ref, *, d, s_lat):
    mod = _row_mod(modb_ref, modc_ref, x_ref.shape[0], s_lat, d)
    h = _adaln(x_ref[...], g_ref[...], mod(3), mod(4))
    _store_token_major(h_ref, h)
    logits = jnp.dot(h, wr_ref[...], preferred_element_type=F32, precision=lax.Precision.HIGHEST)
    lane = lax.broadcasted_iota(jnp.int32, logits.shape, 1)
    logits = jnp.where(lane < N_EXPERTS, logits, -jnp.inf)
    m1 = jnp.max(logits, axis=-1, keepdims=True)
    i1 = jnp.min(jnp.where(logits == m1, lane, LANES), axis=-1, keepdims=True)
    rest = jnp.where(lane == i1, -jnp.inf, logits)
    m2 = jnp.max(rest, axis=-1, keepdims=True)
    i2 = jnp.min(jnp.where(rest == m2, lane, LANES), axis=-1, keepdims=True)
    e2 = jnp.exp(m2 - m1)
    w1 = 1.0 / (1.0 + e2)
    sel_ref[...] = (jnp.where(lane == 0, w1, 0.0) + jnp.where(lane == 1, e2 * w1, 0.0)
                    + jnp.where(lane == 2, i1.astype(F32), 0.0) + jnp.where(lane == 3, i2.astype(F32), 0.0))


def _router(xs, mod_l, g, w_r, s_lat, rows):
    b, _, d = xs.shape
    t = _wide_tile(rows)
    row = lambda i, r: (i, r, 0)
    return pl.pallas_call(
        functools.partial(_router_kernel, d=d, s_lat=s_lat),
        grid=(b, rows // t),
        in_specs=[pl.BlockSpec((None, t, d), row), *_wide_mod_specs(mod_l, b),
                  _full((1, d)), _full(w_r.shape)],
        out_specs=[pl.BlockSpec((None, t * SUBLANES, LANES), row), pl.BlockSpec((None, t, LANES), row)],
        out_shape=[jax.ShapeDtypeStruct((b, rows * SUBLANES, LANES), F32),
                   jax.ShapeDtypeStruct((b, rows, LANES), F32)],
        compiler_params=_params(("arbitrary", "arbitrary")),
        name="moe_router",
    )(xs, mod_l, mod_l, g, w_r)


def _dispatch(sel, tm):
    m = sel.shape[0]
    i32 = jnp.int32
    experts = jnp.arange(N_EXPERTS, dtype=i32)

    def lookup(table, idx):
        return jnp.sum(jnp.where(idx[:, None] == experts[None, :], table[None, :], 0), axis=1)

    idx = sel[:, 2:4].astype(i32)
    e_flat = jnp.concatenate([idx[:, 0], idx[:, 1]])
    assignment = jnp.arange(2 * m, dtype=i32)
    _, order = lax.sort_key_val(e_flat, assignment)
    _, rank = lax.sort_key_val(order, assignment)
    counts = jnp.sum((e_flat[:, None] == experts[None, :]).astype(i32), axis=0)
    padded = (counts + tm - 1) // tm * tm
    pad_end = jnp.cumsum(padded)
    pad_start = pad_end - padded
    start = jnp.cumsum(counts) - counts
    n_tiles = 2 * m // tm + N_EXPERTS
    n_used = pad_end[-1] // tm
    tile = jnp.minimum(jnp.arange(n_tiles, dtype=i32), n_used - 1)
    tile_e = jnp.sum((tile[:, None] * tm >= pad_end[None, :]).astype(i32), axis=1)
    slot = jnp.arange(n_tiles * tm, dtype=i32)
    slot_e = jnp.sum((jnp.minimum(slot, pad_end[-1] - 1)[:, None] >= pad_end[None, :]).astype(i32), axis=1)
    pos = slot - lookup(pad_start, slot_e)
    valid = (pos < lookup(counts, slot_e)) & (slot < pad_end[-1])
    src = jnp.where(valid, order[jnp.clip(lookup(start, slot_e) + pos, 0, 2 * m - 1)] % m, 0)
    slot_of = lookup(pad_start - start, e_flat) + rank
    return (tile_e, n_used.reshape(1).astype(i32), (src * SUBLANES).astype(i32), (slot_of * SUBLANES).astype(i32))


def _experts_kernel(te_ref, nu_ref, src_ref, h_hbm, wg_ref, wu_ref, wd_ref, o_ref, xg_ref, xb_ref, acc_ref, sems,
                    *, n_tiles, n_steps):
    t, j = pl.program_id(0), pl.program_id(1)
    tm = xb_ref.shape[0]
    rows_per_step = tm // n_steps
    n_used = nu_ref[0]
    used = t < n_used
    buf = t % 2

    def start_row(tile, dst_buf, r):
        src = pl.multiple_of(src_ref[tile * tm + r], SUBLANES)
        dst = pl.multiple_of(r * SUBLANES, SUBLANES)
        pltpu.make_async_copy(h_hbm.at[pl.ds(src, SUBLANES)], xg_ref.at[dst_buf, pl.ds(dst, SUBLANES)],
                              sems.at[dst_buf]).start()

    def wait_rows(b_):
        pltpu.make_async_copy(h_hbm.at[pl.ds(0, tm * SUBLANES)], xg_ref.at[b_], sems.at[b_]).wait()

    @pl.when((t == 0) & (j == 0))
    def _():
        def issue(r, carry):
            start_row(0, 0, r)
            return carry

        lax.fori_loop(0, tm, issue, 0, unroll=8)

    @pl.when((j == 0) & ((t == 0) | (t - 1 < n_used)))
    def _():
        wait_rows(buf)
        xb_ref[...] = _load_token_major(xg_ref.at[buf], tm).astype(BF16)

    @pl.when(j == 0)
    def _():
        acc_ref[...] = jnp.zeros_like(acc_ref)

    @pl.when(used)
    def _():
        nxt = jnp.minimum(t + 1, n_tiles - 1)
        for r in range(rows_per_step):
            start_row(nxt, 1 - buf, j * rows_per_step + r)
        h = xb_ref[...]
        a = jnp.dot(h, wg_ref[...].astype(BF16), preferred_element_type=F32)
        u = jnp.dot(h, wu_ref[...].astype(BF16), preferred_element_type=F32)
        act = (_silu(a) * u).astype(BF16)
        acc_ref[...] += jnp.dot(act, wd_ref[...].astype(BF16), preferred_element_type=F32)

    @pl.when(j == n_steps - 1)
    def _():
        _store_token_major(o_ref, acc_ref[...])

    @pl.when(used & (t == n_tiles - 1) & (j == n_steps - 1))
    def _():
        wait_rows(1 - buf)


def _moe_ff_tile(dff):
    return MOE_FF_TILE if dff % MOE_FF_TILE == 0 else dff


def _moe_tile_rows(m, dff):
    per_expert = 2.0 * m / N_EXPERTS
    unit = (dff // _moe_ff_tile(dff)) * 2 * SUBLANES
    rows = (per_expert + 3.0 * math.sqrt(per_expert)) / MOE_TILES_PER_EXPERT
    return int(math.ceil(rows / unit)) * unit


def _experts(h2, tables, wg, wu, wd, layer, tm):
    d, dff = wg.shape[-2:]
    assert d == SUBLANES * LANES
    tile_e, n_used, src, _ = tables
    n_tiles = tile_e.shape[0]
    tf = MOE_FF_TILE if dff % MOE_FF_TILE == 0 else dff
    nj = dff // tf
    assert tm % nj == 0
    col = lambda t, j, te, nu, src: (layer, te[t], 0, jnp.where(t < nu[0], j, nj - 1))
    rowb = lambda t, j, te, nu, src: (layer, te[t], jnp.where(t < nu[0], j, nj - 1), 0)
    return pl.pallas_call(
        functools.partial(_experts_kernel, n_tiles=n_tiles, n_steps=nj),
        grid_spec=pltpu.PrefetchScalarGridSpec(
            num_scalar_prefetch=3,
            grid=(n_tiles, nj),
            in_specs=[pl.BlockSpec(memory_space=pl.ANY),
                      pl.BlockSpec((None, None, d, tf), col),
                      pl.BlockSpec((None, None, d, tf), col),
                      pl.BlockSpec((None, None, tf, d), rowb)],
            out_specs=pl.BlockSpec((tm * SUBLANES, LANES), lambda t, j, te, nu, src: (t, 0)),
            scratch_shapes=[pltpu.VMEM((2, tm * SUBLANES, LANES), F32), pltpu.VMEM((tm, d), BF16),
                            pltpu.VMEM((tm, d), F32), pltpu.SemaphoreType.DMA((2,))]),
        out_shape=jax.ShapeDtypeStruct((n_tiles * tm * SUBLANES, LANES), F32),
        compiler_params=_params(("arbitrary", "arbitrary")),
        name="moe_experts",
    )(tile_e, n_used, src, h2, wg, wu, wd)


def _combine_kernel(slot_ref, x_ref, sel_ref, mod_ref, g_ref, y_hbm, o_ref, r1_ref, r2_ref, sems,
                    *, d, m, rows_per_sample, n_steps, final):
    t = x_ref.shape[0]
    n_r = rows_per_sample // t
    step = pl.program_id(0) * n_r + pl.program_id(1)
    buf = step % 2

    def start_tile(tile, dst_buf):
        base = tile * t

        def issue(k, carry):
            dst = pl.ds(pl.multiple_of(k * SUBLANES, SUBLANES), SUBLANES)
            s1 = pl.multiple_of(slot_ref[base + k], SUBLANES)
            s2 = pl.multiple_of(slot_ref[m + base + k], SUBLANES)
            pltpu.make_async_copy(y_hbm.at[pl.ds(s1, SUBLANES)], r1_ref.at[dst_buf, dst],
                                  sems.at[dst_buf, 0]).start(priority=0)
            pltpu.make_async_copy(y_hbm.at[pl.ds(s2, SUBLANES)], r2_ref.at[dst_buf, dst],
                                  sems.at[dst_buf, 1]).start(priority=1)
            return carry

        lax.fori_loop(0, t, issue, 0, unroll=8)

    @pl.when(step == 0)
    def _():
        start_tile(0, 0)

    pltpu.make_async_copy(y_hbm.at[pl.ds(0, t * SUBLANES)], r1_ref.at[buf], sems.at[buf, 0]).wait()
    pltpu.make_async_copy(y_hbm.at[pl.ds(0, t * SUBLANES)], r2_ref.at[buf], sems.at[buf, 1]).wait()

    @pl.when(step + 1 < n_steps)
    def _():
        start_tile(step + 1, 1 - buf)

    y = (sel_ref[:, 0:1] * _load_token_major(r1_ref.at[buf], t)
         + sel_ref[:, 1:2] * _load_token_major(r2_ref.at[buf], t))
    x = x_ref[...] + mod_ref[:, 5 * d:6 * d] * y
    if final:
        x = x * lax.rsqrt(jnp.mean(x * x, axis=-1, keepdims=True) + EPS) * g_ref[...]
    o_ref[...] = x


def _combine(xs, sel, mod_l, y_sorted, slot_of, final_g, n_lat_tiles, n_row_tiles, final):
    b, _, d = xs.shape
    t = ROW_TILE
    rows = n_row_tiles * t
    row = lambda i, r, s: (i, r, 0)
    return pl.pallas_call(
        functools.partial(_combine_kernel, d=d, m=b * rows, rows_per_sample=rows, n_steps=b * n_row_tiles,
                          final=final),
        grid_spec=pltpu.PrefetchScalarGridSpec(
            num_scalar_prefetch=1,
            grid=(b, n_row_tiles),
            in_specs=[pl.BlockSpec((None, t, d), row),
                      pl.BlockSpec((None, t, LANES), row),
                      pl.BlockSpec((None, 1, mod_l.shape[-1]),
                                   lambda i, r, s: (jnp.where(r < n_lat_tiles, i, b), 0, 0)),
                      pl.BlockSpec((1, d), lambda i, r, s: (0, 0)),
                      pl.BlockSpec(memory_space=pl.ANY)],
            out_specs=pl.BlockSpec((None, t, d), row),
            scratch_shapes=[pltpu.VMEM((2, t * SUBLANES, LANES), F32), pltpu.VMEM((2, t * SUBLANES, LANES), F32),
                            pltpu.SemaphoreType.DMA((2, 2))]),
        out_shape=jax.ShapeDtypeStruct((b, rows, d), F32),
        compiler_params=_params(("arbitrary", "arbitrary")),
        name="moe_combine",
    )(slot_of, xs, sel, mod_l, final_g, y_sorted)


def _rope_tables(s_lat, n_ctx):
    rows = s_lat // GRID_W
    pos_row = jnp.repeat(jnp.arange(rows, dtype=jnp.int32), GRID_W).astype(F32)
    pos_col = jnp.tile(jnp.arange(GRID_W, dtype=jnp.int32), rows).astype(F32)
    quarter = HEAD_DIM // 4
    inv_freq = ROPE_THETA ** (-jnp.arange(quarter, dtype=F32) / quarter)
    ang_r = pos_row[:, None] * inv_freq
    ang_c = pos_col[:, None] * inv_freq
    ang = jnp.concatenate([ang_r, ang_r, ang_c, ang_c], axis=-1)
    cos, sin = jnp.cos(ang), jnp.sin(ang)
    first = (np.arange(HEAD_DIM) % (2 * quarter)) < quarter
    sin_lo = jnp.where(first, -sin, 0.0)
    sin_hi = jnp.where(first, 0.0, sin)

    def widen(tab, ctx_fill):
        tab = jnp.concatenate([tab, jnp.full((n_ctx, HEAD_DIM), ctx_fill, F32)], axis=0)
        return jnp.tile(tab, (1, LANES // HEAD_DIM))

    return widen(cos, 1.0), widen(sin_lo, 0.0), widen(sin_hi, 0.0)


def _dft_tables(n):
    k = np.arange(n, dtype=np.int64)
    ang = 2.0 * np.pi * ((k[:, None] * k[None, :]) % n).astype(np.float64) / n
    return np.cos(ang), np.sin(ang)


def _constants(s_lat, n_ctx):
    c64, s64 = _dft_tables(F_GROUP_DIM)
    eye = np.eye(F_GROUPS)
    cn, sn = _dft_tables(s_lat)
    cc, sc = _dft_tables(n_ctx)
    bf = lambda a: jnp.asarray(a, dtype=F32).astype(BF16)
    pad = np.zeros((n_ctx, s_lat))
    cn, sn = np.concatenate([cn, pad]), np.concatenate([sn, pad])
    return dict(
        gmat=bf(np.kron(np.eye(512 // HEAD_DIM), np.ones((HEAD_DIM, HEAD_DIM)))),
        c64=bf(np.kron(eye, c64) / math.sqrt(F_GROUP_DIM)),
        s64=bf(np.kron(eye, s64) / math.sqrt(F_GROUP_DIM)),
        cn=bf(cn / math.sqrt(s_lat)), sn=bf(sn / math.sqrt(s_lat)),
        cc=bf(cc / math.sqrt(n_ctx)), sc=bf(sc / math.sqrt(n_ctx)),
    )


def _dup_heads(w, off, n_heads):
    cols = []
    for hd in range(n_heads):
        blk = w[:, off + hd * HEAD_DIM: off + (hd + 1) * HEAD_DIM]
        cols += [blk, blk]
    return cols


def _relayout_w_in(w):
    cols = [w[:, _OFF["aq"]:_OFF["bq"]], w[:, _OFF["bq"]:_OFF["bk"]],
            w[:, _OFF["cq"]:_OFF["ck"]], w[:, _OFF["fz"]:_OFF["gt"]]]
    cols += _dup_heads(w, _OFF["bk"], B_KV) + _dup_heads(w, _OFF["bv"], B_KV)
    cols += _dup_heads(w, _OFF["ck"], C_KV) + _dup_heads(w, _OFF["cv"], C_KV)
    return jnp.concatenate(cols, axis=1).astype(BF16)


def kernel(x, c, ctx, c_ctx, w_mod, b_mod, norm1_g, norm2_g, w_in, diff_lambda, diff_subln_g, qk_norm_g,
           sink_logits, w_branch, w_out, ffn_w_gate, ffn_w_up, ffn_w_down, router_w, moe_w_gate, moe_w_up,
           moe_w_down, final_g):
    b, s_lat, d = x.shape
    n_ctx = ctx.shape[1]
    depth = w_mod.shape[0]
    nt = s_lat + n_ctx
    t = ROW_TILE
    assert s_lat % t == 0 and n_ctx % t == 0 and s_lat % n_ctx == 0 and b < MOD_ROWS
    assert w_in.shape[-1] == _OFF["end"] and d * N_BRANCH == _OFF["end"] - _OFF["gt"]
    n_lat_tiles = s_lat // t

    xs = jnp.concatenate([x, ctx], axis=1)
    cvec = jnp.concatenate([c, c_ctx[None], jnp.zeros((MOD_ROWS - b - 1, d), F32)], axis=0)
    mod = _modulation(cvec, w_mod, b_mod).reshape(depth, MOD_ROWS, 1, 6 * d)
    tabs = _rope_tables(s_lat, n_ctx)
    consts = _constants(s_lat, n_ctx)

    out = None
    for l in range(depth):
        last = l == depth - 1
        mod_l = mod[l]
        p = _inproj(xs, mod_l, norm1_g[l][None], _relayout_w_in(w_in[l]), tabs,
                    jnp.tile(qk_norm_g[l, 0], 512 // HEAD_DIM)[None], jnp.tile(qk_norm_g[l, 1], 256 // HEAD_DIM)[None],
                    consts, s_lat)
        p = (None, *p)
        w_gates = w_in[l][:, _OFF["gt"]:_OFF["end"]].astype(BF16)
        lam_init = 0.8 - 0.6 * math.exp(-0.3 * l)
        n_q_tiles = n_lat_tiles if last else nt // t
        oa, ob, oc = _attention(p, diff_lambda[l], diff_subln_g[l][None], sink_logits[l],
                                n_lat_tiles, n_q_tiles, s_lat, lam_init)
        od = _fourier(p[10], p[11], consts, s_lat, n_q_tiles * t)
        xs = _merge(xs, mod_l, norm1_g[l][None], w_gates, (oa, ob, oc, od), w_branch[l].astype(BF16),
                    w_out[l].astype(BF16), s_lat, n_q_tiles * t)

        i = l // 2
        if l % 2 == 0:
            assert not last
            xs = _ffn(xs, mod_l, norm2_g[l][None], ffn_w_gate[i].astype(BF16), ffn_w_up[i].astype(BF16),
                      ffn_w_down[i].astype(BF16), s_lat)
        else:
            w_r = jnp.pad(router_w[i], ((0, 0), (0, LANES - N_EXPERTS)))
            n_row_tiles = n_q_tiles
            m = b * n_row_tiles * t
            h2, sel = _router(xs, mod_l, norm2_g[l][None], w_r, s_lat, n_row_tiles * t)
            tm = _moe_tile_rows(m, moe_w_gate.shape[-1])
            tables = _dispatch(sel.reshape(m, LANES), tm)
            y_sorted = _experts(h2.reshape(m * SUBLANES, LANES), tables, moe_w_gate, moe_w_up, moe_w_down, i, tm)
            res = _combine(xs, sel, mod_l, y_sorted, tables[3], final_g[None], n_lat_tiles, n_row_tiles, last)
            if last:
                out = res
            else:
                xs = res
    if out is None:
        raise NotImplementedError("final normalisation is fused into the last (mixture-of-experts) layer")
    return out
```

```python
import functools
import math

import numpy as np
import jax
import jax.numpy as jnp
from jax import lax
from jax.experimental import pallas as pl
from jax.experimental.pallas import tpu as pltpu

F32 = jnp.float32
BF16 = jnp.bfloat16

GRID_W = 64
HEAD_DIM = 64
ROPE_THETA = 10000.0
WINDOW = 128
A_HEADS = 4
B_HEADS = 8
B_KV = 2
C_HEADS = 8
C_KV = 2
F_GROUPS = 8
F_GROUP_DIM = 64
BRANCH_W = 512
N_BRANCH = 4
N_EXPERTS = 8
EPS = 1e-6
SUBLN_EPS = 1e-5
NEG_INF = -1e30
LOG2E = math.log2(math.e)

LANES = 128
SUBLANES = 8
ROW_TILE = 256
WIDE_TILES = (768, 512, 256)
FFN_CHUNK = 768
MOE_TILES_PER_EXPERT = 4
MOE_FF_TILE = 512
MOD_ROWS = 16
VMEM_LIMIT = 56 * 1024 * 1024

_OFF = dict(aq=0, ak=512, av=1024, bq=1536, bk=2048, bv=2176, cq=2304, ck=2816, cv=2944,
            fz=3072, gt=3584, end=7680)
_P = dict(aq=0, ak=512, av=1024, bq=1536, cq=2048, fz=2560, bk=3072, bv=3328, ck=3584, cv=3840, end=4096)


def _params(sem):
    return pltpu.CompilerParams(dimension_semantics=sem, vmem_limit_bytes=VMEM_LIMIT)


def _full(shape):
    n = len(shape)
    return pl.BlockSpec(shape, lambda *_: (0,) * n)


def _silu(x):
    return x / (1.0 + jnp.exp(-x))


def _sigmoid(x):
    return 1.0 / (1.0 + jnp.exp(-x))


def _adaln(x, g, shift, scale):
    y = x * lax.rsqrt(jnp.mean(x * x, axis=-1, keepdims=True) + EPS) * g
    return y * (1.0 + scale) + shift


def _mod_kernel(c_ref, w_ref, b_ref, o_ref):
    s = _silu(c_ref[...]).astype(BF16)
    o_ref[...] = jnp.dot(s, w_ref[...].astype(BF16), preferred_element_type=F32) + b_ref[...]


def _modulation(cvec, w_mod, b_mod):
    depth, d, n = w_mod.shape
    tn = 1536 if n % 1536 == 0 else n
    return pl.pallas_call(
        _mod_kernel,
        grid=(depth, n // tn),
        in_specs=[_full((MOD_ROWS, d)),
                  pl.BlockSpec((None, d, tn), lambda l, j: (l, 0, j)),
                  pl.BlockSpec((None, 1, tn), lambda l, j: (l, 0, j))],
        out_specs=pl.BlockSpec((None, MOD_ROWS, tn), lambda l, j: (l, 0, j)),
        out_shape=jax.ShapeDtypeStruct((depth, MOD_ROWS, n), F32),
        compiler_params=_params(("arbitrary", "arbitrary")),
        name="modulation",
    )(cvec, w_mod, b_mod.reshape(depth, 1, n))


def _rope(p, cos, sin_lo, sin_hi):
    outs = []
    for c in range(p.shape[1] // LANES):
        xc = p[:, c * LANES:(c + 1) * LANES]
        outs.append(xc * cos + pltpu.roll(xc, LANES - 16, 1) * sin_lo + pltpu.roll(xc, 16, 1) * sin_hi)
    return outs[0] if len(outs) == 1 else jnp.concatenate(outs, axis=1)


def _group_rmsnorm(y, gmat, g):
    ss = jnp.dot((y * y).astype(BF16), gmat, preferred_element_type=F32)
    return y * lax.rsqrt(ss * (1.0 / HEAD_DIM) + EPS) * g


def _inproj_kernel(x_ref, modb_ref, modc_ref, g_ref, w_ref, cos_ref, slo_ref, shi_ref, qg_ref, kg_ref, gmat_ref,
                   c64_ref, s64_ref,
                   aq_ref, akt_ref, av_ref, bq_ref, bkt_ref, bv_ref, cq_ref, ckt_ref, cv_ref,
                   fzc_ref, fzs_ref, *, d, s_lat):
    mod = _row_mod(modb_ref, modc_ref, x_ref.shape[0], s_lat, d)
    h = _adaln(x_ref[...], g_ref[...], mod(0), mod(1)).astype(BF16)
    cos, slo, shi = cos_ref[...], slo_ref[...], shi_ref[...]
    scale = HEAD_DIM ** -0.5 * LOG2E

    def proj(name, width):
        return lambda: jnp.dot(h, w_ref[:, _P[name]:_P[name] + width], preferred_element_type=F32)

    def finish_bq(p):
        bq_ref[...] = (_rope(_group_rmsnorm(p, gmat_ref[...], qg_ref[...]), cos, slo, shi) * scale).astype(BF16)

    def finish_bk(p):
        bkt_ref[...] = _rope(_group_rmsnorm(p, gmat_ref[0:256, 0:256], kg_ref[...]), cos, slo, shi).T.astype(BF16)

    def finish_fz(p):
        fz = p.astype(BF16)
        fzc_ref[...] = jnp.dot(fz, c64_ref[...], preferred_element_type=F32).astype(BF16)
        fzs_ref[...] = jnp.dot(fz, s64_ref[...], preferred_element_type=F32).astype(BF16)

    def store(ref, fn):
        def finish(p):
            ref[...] = fn(p).astype(BF16)
        return finish

    rope = lambda p: _rope(p, cos, slo, shi)
    groups = [
        (proj("aq", 512), store(aq_ref, lambda p: rope(p) * scale)),
        (proj("ak", 512), store(akt_ref, lambda p: rope(p).T)),
        (proj("av", 512), store(av_ref, lambda p: p)),
        (proj("bq", 512), finish_bq),
        (proj("bk", 256), finish_bk),
        (proj("bv", 256), store(bv_ref, lambda p: p)),
        (proj("cq", 512), store(cq_ref, lambda p: rope(p) * scale)),
        (proj("ck", 256), store(ckt_ref, lambda p: rope(p).T)),
        (proj("cv", 256), store(cv_ref, lambda p: p)),
        (proj("fz", 512), finish_fz),
    ]
    for (_, finish), p in zip(groups, _pipelined_scores([g[0] for g in groups])):
        finish(p)


def _inproj(xs, mod_l, g, w_p, tabs, qg, kg, consts, s_lat):
    b, nt, d = xs.shape
    t = _wide_tile(nt)
    row = lambda i, r: (i, r, 0)
    tr = lambda i, r: (i, 0, r)
    tab = pl.BlockSpec((t, LANES), lambda i, r: (r, 0))

    def out(width):
        return pl.BlockSpec((None, t, width), row), jax.ShapeDtypeStruct((b, nt, width), BF16)

    def out_t(width):
        return pl.BlockSpec((None, width, t), tr), jax.ShapeDtypeStruct((b, width, nt), BF16)

    outs = [out(512), out_t(512), out(512), out(512), out_t(256), out(256),
            out(512), out_t(256), out(256), out(512), out(512)]
    return pl.pallas_call(
        functools.partial(_inproj_kernel, d=d, s_lat=s_lat),
        grid=(b, nt // t),
        in_specs=[pl.BlockSpec((None, t, d), row), *_wide_mod_specs(mod_l, b),
                  _full((1, d)),
                  pl.BlockSpec(w_p.shape, lambda i, r: (0, 0), pipeline_mode=pl.Buffered(1)),
                  tab, tab, tab,
                  _full((1, 512)), _full((1, 256)), _full((512, 512)), _full((512, 512)), _full((512, 512))],
        out_specs=[o[0] for o in outs],
        out_shape=[o[1] for o in outs],
        compiler_params=_params(("arbitrary", "arbitrary")),
        name="inproj",
    )(xs, mod_l, mod_l, g, w_p, tabs[0], tabs[1], tabs[2], qg, kg, consts["gmat"], consts["c64"], consts["s64"])


def _half_masks(shape):
    lane = lax.broadcasted_iota(jnp.int32, shape, 1)
    return lane < HEAD_DIM, lane >= HEAD_DIM


def _exp2_bf16(s):
    return jnp.exp2(s - jnp.max(s, axis=-1, keepdims=True)).astype(BF16)


def _pipelined_scores(score_fns):
    nxt = score_fns[0]()
    for i in range(len(score_fns)):
        cur = nxt
        if i + 1 < len(score_fns):
            nxt = score_fns[i + 1]()
        yield cur


def _attn_a_kernel(q_ref, kt_ref, v_ref, lp_ref, g_ref, o_ref, *, n_lat_tiles, s_lat, lam_init):
    r = pl.program_id(1)
    lp = lp_ref[...]
    lam = (jnp.exp(jnp.sum(lp[0:1] * lp[1:2], axis=-1, keepdims=True))
           - jnp.exp(jnp.sum(lp[2:3] * lp[3:4], axis=-1, keepdims=True)) + lam_init)
    g = g_ref[...] * (1.0 - lam_init)
    nt = kt_ref.shape[1]

    def body(k0):
        lo, hi = _half_masks((q_ref.shape[0], LANES))
        ones = jnp.ones((nt - k0, LANES), BF16)

        def score_fn(hd, msk):
            cs = slice(hd * LANES, (hd + 1) * LANES)
            return lambda: jnp.dot(jnp.where(msk, q_ref[:, cs], 0), kt_ref[cs, k0:nt], preferred_element_type=F32)

        fns = [score_fn(hd, msk) for hd in range(A_HEADS) for msk in (lo, hi)]
        outs = []
        for i, sc in enumerate(_pipelined_scores(fns)):
            hd = i // 2
            cs = slice(hd * LANES, (hd + 1) * LANES)
            vo = jnp.concatenate([v_ref[k0:nt, cs], ones], axis=1)
            full = jnp.dot(_exp2_bf16(sc), vo, preferred_element_type=F32)
            outs.append(full[:, 0:LANES] * (1.0 / full[:, LANES:LANES + 1]))
            if len(outs) == 2:
                o = outs[0] - lam * outs[1]
                outs = []
                o = o * lax.rsqrt(jnp.mean(o * o, axis=-1, keepdims=True) + SUBLN_EPS) * g
                o_ref[:, cs] = o.astype(BF16)

    pl.when(r < n_lat_tiles)(lambda: body(0))
    pl.when(r >= n_lat_tiles)(lambda: body(s_lat))


def _attn_b_kernel(q_ref, kt_ref, v_ref, o_ref, *, n_lat_tiles, s_lat):
    r = pl.program_id(1)
    nt = kt_ref.shape[1]

    def body(k0):
        lo, hi = _half_masks((q_ref.shape[0], LANES))
        ones = jnp.ones((nt - k0, LANES), BF16)

        def score_fn(pair, msk):
            kv = pair // (B_HEADS // B_KV // 2)
            q = q_ref[:, pair * LANES:(pair + 1) * LANES]
            return lambda: jnp.dot(jnp.where(msk, q, 0), kt_ref[kv * LANES:(kv + 1) * LANES, k0:nt],
                                   preferred_element_type=F32)

        fns = [score_fn(pair, msk) for pair in range(B_HEADS // 2) for msk in (lo, hi)]
        halves = []
        for i, sc in enumerate(_pipelined_scores(fns)):
            pair = i // 2
            kv = pair // (B_HEADS // B_KV // 2)
            vo = jnp.concatenate([v_ref[k0:nt, kv * LANES:(kv + 1) * LANES], ones], axis=1)
            full = jnp.dot(_exp2_bf16(sc), vo, preferred_element_type=F32)
            halves.append(full[:, 0:LANES] * (1.0 / full[:, LANES:LANES + 1]))
            if len(halves) == 2:
                o_ref[:, pair * LANES:(pair + 1) * LANES] = jnp.where(lo, halves[0], halves[1]).astype(BF16)
                halves = []

    pl.when(r < n_lat_tiles)(lambda: body(0))
    pl.when(r >= n_lat_tiles)(lambda: body(s_lat))


def _attn_c_kernel(sink_ref, q_ref, ktl_ref, ktm_ref, ktr_ref, ktc_ref, vl_ref, vm_ref, vr_ref, vc_ref, o_ref,
                   *, n_lat_tiles, s_lat):
    r = pl.program_id(1)
    t = q_ref.shape[0]
    n_band = t + 2 * WINDOW
    n_keys = n_band + ktc_ref.shape[1]
    u = lax.broadcasted_iota(jnp.int32, (t, n_keys), 1)
    row = lax.broadcasted_iota(jnp.int32, (t, n_keys), 0)
    j = r * t - WINDOW + u
    in_band = (u >= row) & (u <= row + 2 * WINDOW) & (j >= 0) & (j < s_lat) & (r < n_lat_tiles)
    valid = in_band | (u >= n_band)
    lo, hi = _half_masks((t, LANES))
    ones = jnp.ones((n_keys, LANES), BF16)
    kts, vos = [], []
    for kv in range(C_KV):
        ks = slice(kv * LANES, (kv + 1) * LANES)
        kts.append(jnp.concatenate([ktl_ref[ks, :], ktm_ref[ks, :], ktr_ref[ks, :], ktc_ref[ks, :]], axis=1))
        v = jnp.concatenate([vl_ref[:, ks], vm_ref[:, ks], vr_ref[:, ks], vc_ref[:, ks]], axis=0)
        vos.append(jnp.concatenate([v, ones], axis=1))

    def score_fn(pair, msk):
        kv = pair // (C_HEADS // C_KV // 2)
        q = q_ref[:, pair * LANES:(pair + 1) * LANES]
        return lambda: jnp.dot(jnp.where(msk, q, 0), kts[kv], preferred_element_type=F32)

    fns = [score_fn(pair, msk) for pair in range(C_HEADS // 2) for msk in (lo, hi)]
    halves = []
    for i, sc in enumerate(_pipelined_scores(fns)):
        pair = i // 2
        kv = pair // (C_HEADS // C_KV // 2)
        sink = sink_ref[i] * LOG2E
        sc = jnp.where(valid, sc, NEG_INF)
        m = jnp.maximum(jnp.max(sc, axis=-1, keepdims=True), sink)
        full = jnp.dot(jnp.exp2(sc - m).astype(BF16), vos[kv], preferred_element_type=F32)
        l = full[:, LANES:LANES + 1] + jnp.exp2(sink - m)
        halves.append(full[:, 0:LANES] * (1.0 / l))
        if len(halves) == 2:
            o_ref[:, pair * LANES:(pair + 1) * LANES] = jnp.where(lo, halves[0], halves[1]).astype(BF16)
            halves = []


def _attention(p, lp, subln_g, sink, n_lat_tiles, n_q_tiles, s_lat, lam_init):
    (_, aq, akt, av, bq, bkt, bv, cq, ckt, cv, _, _) = p
    b, nt, _ = aq.shape
    t = ROW_TILE
    grid = (b, n_q_tiles)
    row = lambda i, r: (i, r, 0)
    whole = lambda i, r: (i, 0, 0)
    sem = _params(("arbitrary", "arbitrary"))
    o_spec = pl.BlockSpec((None, t, 512), row)
    o_shape = jax.ShapeDtypeStruct((b, n_q_tiles * t, 512), BF16)

    oa = pl.pallas_call(
        functools.partial(_attn_a_kernel, n_lat_tiles=n_lat_tiles, s_lat=s_lat, lam_init=lam_init),
        grid=grid,
        in_specs=[pl.BlockSpec((None, t, 512), row), pl.BlockSpec((None, 512, nt), whole),
                  pl.BlockSpec((None, nt, 512), whole), _full((4, HEAD_DIM)), _full((1, 2 * HEAD_DIM))],
        out_specs=o_spec, out_shape=o_shape, compiler_params=sem, name="attn_diff",
    )(aq, akt, av, lp, subln_g)

    ob = pl.pallas_call(
        functools.partial(_attn_b_kernel, n_lat_tiles=n_lat_tiles, s_lat=s_lat),
        grid=grid,
        in_specs=[pl.BlockSpec((None, t, 512), row), pl.BlockSpec((None, 256, nt), whole),
                  pl.BlockSpec((None, nt, 256), whole)],
        out_specs=o_spec, out_shape=o_shape, compiler_params=sem, name="attn_gqa",
    )(bq, bkt, bv)

    wpt = t // WINDOW
    last = s_lat // WINDOW - 1
    left = lambda r: jnp.clip(r * wpt - 1, 0, last)
    mid = lambda r: jnp.minimum(r, n_lat_tiles - 1)
    right = lambda r: jnp.clip((r + 1) * wpt, 0, last)
    n_ctx = nt - s_lat
    ctx_blk = s_lat // n_ctx
    oc = pl.pallas_call(
        functools.partial(_attn_c_kernel, n_lat_tiles=n_lat_tiles, s_lat=s_lat),
        grid=grid,
        in_specs=[pl.BlockSpec(memory_space=pltpu.SMEM),
                  pl.BlockSpec((None, t, 512), row),
                  pl.BlockSpec((None, 256, WINDOW), lambda i, r: (i, 0, left(r))),
                  pl.BlockSpec((None, 256, t), lambda i, r: (i, 0, mid(r))),
                  pl.BlockSpec((None, 256, WINDOW), lambda i, r: (i, 0, right(r))),
                  pl.BlockSpec((None, 256, n_ctx), lambda i, r: (i, 0, ctx_blk)),
                  pl.BlockSpec((None, WINDOW, 256), lambda i, r: (i, left(r), 0)),
                  pl.BlockSpec((None, t, 256), lambda i, r: (i, mid(r), 0)),
                  pl.BlockSpec((None, WINDOW, 256), lambda i, r: (i, right(r), 0)),
                  pl.BlockSpec((None, n_ctx, 256), lambda i, r: (i, ctx_blk, 0))],
        out_specs=o_spec, out_shape=o_shape, compiler_params=sem, name="attn_window",
    )(sink, cq, ckt, ckt, ckt, ckt, cv, cv, cv, cv)
    return oa, ob, oc


def _fourier_kernel(cn_ref, sn_ref, cc_ref, sc_ref, zc_ref, zs_ref, o_ref, *, s_lat, rows):
    r = pl.program_id(1)
    t = o_ref.shape[0]
    nt = zc_ref.shape[0]
    n_full = s_lat // t
    lat_rows = s_lat - n_full * t

    def latent(n):
        return (jnp.dot(cn_ref[0:n, :], zc_ref[0:s_lat, :], preferred_element_type=F32)
                - jnp.dot(sn_ref[0:n, :], zs_ref[0:s_lat, :], preferred_element_type=F32)).astype(BF16)

    @pl.when(r < n_full)
    def _():
        o_ref[...] = latent(t)

    if rows > s_lat:
        @pl.when(r >= n_full)
        def _():
            if lat_rows:
                o_ref[0:lat_rows, :] = latent(lat_rows)
            y = (jnp.dot(cc_ref[...], zc_ref[s_lat:nt, :], preferred_element_type=F32)
                 - jnp.dot(sc_ref[...], zs_ref[s_lat:nt, :], preferred_element_type=F32))
            o_ref[lat_rows:t, :] = y.astype(BF16)


def _fourier(fzc, fzs, consts, s_lat, rows):
    b, nt, w = fzc.shape
    t = _wide_tile(rows)
    n_ctx = nt - s_lat
    assert rows == s_lat or (rows == nt and nt - s_lat // t * t == t)
    tile = lambda i, r: (r, 0)
    whole = lambda i, r: (i, 0, 0)
    return pl.pallas_call(
        functools.partial(_fourier_kernel, s_lat=s_lat, rows=rows),
        grid=(b, rows // t),
        in_specs=[pl.BlockSpec((t, s_lat), tile), pl.BlockSpec((t, s_lat), tile),
                  _full((n_ctx, n_ctx)), _full((n_ctx, n_ctx)),
                  pl.BlockSpec((None, nt, w), whole), pl.BlockSpec((None, nt, w), whole)],
        out_specs=pl.BlockSpec((None, t, w), lambda i, r: (i, r, 0)),
        out_shape=jax.ShapeDtypeStruct((b, rows, w), BF16),
        compiler_params=_params(("arbitrary", "arbitrary")),
        name="fourier",
    )(consts["cn"], consts["sn"], consts["cc"], consts["sc"], fzc, fzs)


def _wide_tile(rows):
    for t in WIDE_TILES:
        if rows % t == 0:
            return t
    raise ValueError(rows)


def _wide_mod_specs(mod_l, b):
    n = mod_l.shape[-1]
    return [pl.BlockSpec((None, 1, n), lambda i, r: (i, 0, 0)), pl.BlockSpec((None, 1, n), lambda i, r: (b, 0, 0))]


def _row_mod(modb_ref, modc_ref, rows, s_lat, d):
    is_lat = pl.program_id(1) * rows + lax.broadcasted_iota(jnp.int32, (rows, 1), 0) < s_lat
    return lambda k: jnp.where(is_lat, modb_ref[:, k * d:(k + 1) * d], modc_ref[:, k * d:(k + 1) * d])


def _merge_kernel(x_ref, modb_ref, modc_ref, g_ref, wgt_ref, oa_ref, ob_ref, oc_ref, od_ref, wbr_ref, wo_ref,
                  o_ref, *, d, s_lat):
    mod = _row_mod(modb_ref, modc_ref, x_ref.shape[0], s_lat, d)
    x = x_ref[...]
    h = _adaln(x, g_ref[...], mod(0), mod(1)).astype(BF16)
    merged = None
    for i, br in enumerate((oa_ref, ob_ref, oc_ref, od_ref)):
        gate = _sigmoid(jnp.dot(h, wgt_ref[:, i * d:(i + 1) * d], preferred_element_type=F32))
        y = gate * jnp.dot(br[...], wbr_ref[i], preferred_element_type=F32)
        merged = y if merged is None else merged + y
    out = jnp.dot(merged.astype(BF16), wo_ref[...], preferred_element_type=F32)
    o_ref[...] = x + mod(2) * out


def _merge(xs, mod_l, g, w_gates, branches, w_br, w_o, s_lat, rows):
    b, _, d = xs.shape
    t = _wide_tile(rows)
    row = lambda i, r: (i, r, 0)
    br_spec = pl.BlockSpec((None, t, BRANCH_W), row)
    const = lambda shape: pl.BlockSpec(shape, lambda i, r: (0,) * len(shape), pipeline_mode=pl.Buffered(1))
    return pl.pallas_call(
        functools.partial(_merge_kernel, d=d, s_lat=s_lat),
        grid=(b, rows // t),
        in_specs=[pl.BlockSpec((None, t, d), row), *_wide_mod_specs(mod_l, b),
                  _full((1, d)), const(w_gates.shape),
                  br_spec, br_spec, br_spec, br_spec,
                  const(w_br.shape), const(w_o.shape)],
        out_specs=pl.BlockSpec((None, t, d), row),
        out_shape=jax.ShapeDtypeStruct((b, rows, d), F32),
        compiler_params=_params(("arbitrary", "arbitrary")),
        name="merge",
    )(xs, mod_l, mod_l, g, w_gates, *branches, w_br, w_o)


def _ffn_kernel(x_ref, modb_ref, modc_ref, g_ref, wg_ref, wu_ref, wd_ref, o_ref, *, d, s_lat):
    mod = _row_mod(modb_ref, modc_ref, x_ref.shape[0], s_lat, d)
    x = x_ref[...]
    h = _adaln(x, g_ref[...], mod(3), mod(4)).astype(BF16)
    dff = wg_ref.shape[1]
    y = None
    for c0 in range(0, dff, FFN_CHUNK):
        c1 = min(c0 + FFN_CHUNK, dff)
        a = jnp.dot(h, wg_ref[:, c0:c1], preferred_element_type=F32)
        u = jnp.dot(h, wu_ref[:, c0:c1], preferred_element_type=F32)
        part = jnp.dot((_silu(a) * u).astype(BF16), wd_ref[c0:c1, :], preferred_element_type=F32)
        y = part if y is None else y + part
    o_ref[...] = x + mod(5) * y


def _ffn(xs, mod_l, g, wg, wu, wd, s_lat):
    b, nt, d = xs.shape
    t = _wide_tile(nt)
    row = lambda i, r: (i, r, 0)
    const = lambda shape: pl.BlockSpec(shape, lambda i, r: (0, 0), pipeline_mode=pl.Buffered(1))
    return pl.pallas_call(
        functools.partial(_ffn_kernel, d=d, s_lat=s_lat),
        grid=(b, nt // t),
        in_specs=[pl.BlockSpec((None, t, d), row), *_wide_mod_specs(mod_l, b),
                  _full((1, d)), const(wg.shape), const(wu.shape), const(wd.shape)],
        out_specs=pl.BlockSpec((None, t, d), row),
        out_shape=jax.ShapeDtypeStruct(xs.shape, F32),
        compiler_params=_params(("arbitrary", "arbitrary")),
        name="ffn_dense",
    )(xs, mod_l, mod_l, g, wg, wu, wd)


def _store_token_major(ref, val):
    rows = val.shape[0]
    for c in range(SUBLANES):
        ref[pl.ds(c, rows, stride=SUBLANES), :] = val[:, c * LANES:(c + 1) * LANES]


def _load_token_major(ref, rows):
    return jnp.concatenate([ref[pl.ds(c, rows, stride=SUBLANES), :] for c in range(SUBLANES)], axis=1)


def _router_kernel(x_ref, modb_ref, modc_ref, g_ref, wr_ref, h_ref, sel_ref, *, d, s_lat):
    mod = _row_mod(modb_ref, modc_ref, x_ref.shape[0], s_lat, d)
    h = _adaln(x_ref[...], g_ref[...], mod(3), mod(4))
    _store_token_major(h_ref, h)
    logits = jnp.dot(h, wr_ref[...], preferred_element_type=F32, precision=lax.Precision.HIGHEST)
    lane = lax.broadcasted_iota(jnp.int32, logits.shape, 1)
    logits = jnp.where(lane < N_EXPERTS, logits, -jnp.inf)
    m1 = jnp.max(logits, axis=-1, keepdims=True)
    i1 = jnp.min(jnp.where(logits == m1, lane, LANES), axis=-1, keepdims=True)
    rest = jnp.where(lane == i1, -jnp.inf, logits)
    m2 = jnp.max(rest, axis=-1, keepdims=True)
    i2 = jnp.min(jnp.where(rest == m2, lane, LANES), axis=-1, keepdims=True)
    e2 = jnp.exp(m2 - m1)
    w1 = 1.0 / (1.0 + e2)
    sel_ref[...] = (jnp.where(lane == 0, w1, 0.0) + jnp.where(lane == 1, e2 * w1, 0.0)
                    + jnp.where(lane == 2, i1.astype(F32), 0.0) + jnp.where(lane == 3, i2.astype(F32), 0.0))


def _router(xs, mod_l, g, w_r, s_lat, rows):
    b, _, d = xs.shape
    t = _wide_tile(rows)
    row = lambda i, r: (i, r, 0)
    return pl.pallas_call(
        functools.partial(_router_kernel, d=d, s_lat=s_lat),
        grid=(b, rows // t),
        in_specs=[pl.BlockSpec((None, t, d), row), *_wide_mod_specs(mod_l, b),
                  _full((1, d)), _full(w_r.shape)],
        out_specs=[pl.BlockSpec((None, t * SUBLANES, LANES), row), pl.BlockSpec((None, t, LANES), row)],
        out_shape=[jax.ShapeDtypeStruct((b, rows * SUBLANES, LANES), F32),
                   jax.ShapeDtypeStruct((b, rows, LANES), F32)],
        compiler_params=_params(("arbitrary", "arbitrary")),
        name="moe_router",
    )(xs, mod_l, mod_l, g, w_r)


def _dispatch(sel, tm):
    m = sel.shape[0]
    i32 = jnp.int32
    experts = jnp.arange(N_EXPERTS, dtype=i32)

    def lookup(table, idx):
        return jnp.sum(jnp.where(idx[:, None] == experts[None, :], table[None, :], 0), axis=1)

    idx = sel[:, 2:4].astype(i32)
    e_flat = jnp.concatenate([idx[:, 0], idx[:, 1]])
    assignment = jnp.arange(2 * m, dtype=i32)
    _, order = lax.sort_key_val(e_flat, assignment)
    _, rank = lax.sort_key_val(order, assignment)
    counts = jnp.sum((e_flat[:, None] == experts[None, :]).astype(i32), axis=0)
    padded = (counts + tm - 1) // tm * tm
    pad_end = jnp.cumsum(padded)
    pad_start = pad_end - padded
    start = jnp.cumsum(counts) - counts
    n_tiles = 2 * m // tm + N_EXPERTS
    n_used = pad_end[-1] // tm
    tile = jnp.minimum(jnp.arange(n_tiles, dtype=i32), n_used - 1)
    tile_e = jnp.sum((tile[:, None] * tm >= pad_end[None, :]).astype(i32), axis=1)
    slot = jnp.arange(n_tiles * tm, dtype=i32)
    slot_e = jnp.sum((jnp.minimum(slot, pad_end[-1] - 1)[:, None] >= pad_end[None, :]).astype(i32), axis=1)
    pos = slot - lookup(pad_start, slot_e)
    valid = (pos < lookup(counts, slot_e)) & (slot < pad_end[-1])
    src = jnp.where(valid, order[jnp.clip(lookup(start, slot_e) + pos, 0, 2 * m - 1)] % m, 0)
    slot_of = lookup(pad_start - start, e_flat) + rank
    return (tile_e, n_used.reshape(1).astype(i32), (src * SUBLANES).astype(i32), (slot_of * SUBLANES).astype(i32))


def _gather_kernel(src_ref, nu_ref, h_hbm, o_ref, xg_ref, sems, *, rows, n_steps, tm):
    step = pl.program_id(0)
    buf = step % 2
    half = rows // 2
    used_rows = nu_ref[0] * tm

    def start_step(s, dst_buf):
        base = s * rows

        def issue(k, carry):
            for q in range(2):
                r = 2 * k + q
                src = pl.multiple_of(src_ref[base + r], SUBLANES)
                dst = pl.multiple_of(r * SUBLANES, SUBLANES)
                pltpu.make_async_copy(h_hbm.at[pl.ds(src, SUBLANES)], xg_ref.at[dst_buf, pl.ds(dst, SUBLANES)],
                                      sems.at[dst_buf, q]).start(priority=q)
            return carry

        lax.fori_loop(0, half, issue, 0, unroll=8)

    @pl.when(step == 0)
    def _():
        start_step(0, 0)

    @pl.when(step * rows < used_rows)
    def _():
        for q in range(2):
            part = pl.ds(q * half * SUBLANES, half * SUBLANES)
            pltpu.make_async_copy(h_hbm.at[pl.ds(0, half * SUBLANES)], xg_ref.at[buf, part], sems.at[buf, q]).wait()

    @pl.when((step + 1 < n_steps) & ((step + 1) * rows < used_rows))
    def _():
        start_step(step + 1, 1 - buf)

    @pl.when(step * rows < used_rows)
    def _():
        o_ref[...] = _load_token_major(xg_ref.at[buf], rows).astype(BF16)

    @pl.when(step * rows >= used_rows)
    def _():
        o_ref[...] = jnp.zeros_like(o_ref)


def _gather_rows(h2, tables, tm):
    tile_e, n_used, src, _ = tables
    n_slots = src.shape[0]
    rows = tm
    assert rows % (2 * SUBLANES) == 0
    n_steps = n_slots // rows
    d = SUBLANES * LANES
    return pl.pallas_call(
        functools.partial(_gather_kernel, rows=rows, n_steps=n_steps, tm=tm),
        grid_spec=pltpu.PrefetchScalarGridSpec(
            num_scalar_prefetch=2,
            grid=(n_steps,),
            in_specs=[pl.BlockSpec(memory_space=pl.ANY)],
            out_specs=pl.BlockSpec((rows, d), lambda s, src, nu: (s, 0)),
            scratch_shapes=[pltpu.VMEM((2, rows * SUBLANES, LANES), F32), pltpu.SemaphoreType.DMA((2, 2))]),
        out_shape=jax.ShapeDtypeStruct((n_slots, d), BF16),
        compiler_params=_params(("arbitrary",)),
        name="moe_gather",
    )(src, n_used, h2)


def _experts_kernel(te_ref, nu_ref, h_ref, wg_ref, wu_ref, wd_ref, o_ref, acc_ref, *, n_steps):
    t, j = pl.program_id(0), pl.program_id(1)

    @pl.when(j == 0)
    def _():
        acc_ref[...] = jnp.zeros_like(acc_ref)

    @pl.when(t < nu_ref[0])
    def _():
        h = h_ref[...]
        a = jnp.dot(h, wg_ref[...].astype(BF16), preferred_element_type=F32)
        u = jnp.dot(h, wu_ref[...].astype(BF16), preferred_element_type=F32)
        act = (_silu(a) * u).astype(BF16)
        acc_ref[...] += jnp.dot(act, wd_ref[...].astype(BF16), preferred_element_type=F32)

    @pl.when(j == n_steps - 1)
    def _():
        _store_token_major(o_ref, acc_ref[...])


def _moe_ff_tile(dff):
    return MOE_FF_TILE if dff % MOE_FF_TILE == 0 else dff


def _moe_tile_rows(m, dff):
    per_expert = 2.0 * m / N_EXPERTS
    unit = 2 * SUBLANES
    rows = (per_expert + 3.0 * math.sqrt(per_expert)) / MOE_TILES_PER_EXPERT
    return int(math.ceil(rows / unit)) * unit


def _experts(xg, tables, wg, wu, wd, layer, tm):
    d, dff = wg.shape[-2:]
    assert d == SUBLANES * LANES
    tile_e, n_used, _, _ = tables
    n_tiles = tile_e.shape[0]
    tf = _moe_ff_tile(dff)
    nj = dff // tf
    col = lambda t, j, te, nu: (layer, te[t], 0, jnp.where(t < nu[0], j, nj - 1))
    rowb = lambda t, j, te, nu: (layer, te[t], jnp.where(t < nu[0], j, nj - 1), 0)
    return pl.pallas_call(
        functools.partial(_experts_kernel, n_steps=nj),
        grid_spec=pltpu.PrefetchScalarGridSpec(
            num_scalar_prefetch=2,
            grid=(n_tiles, nj),
            in_specs=[pl.BlockSpec((tm, d), lambda t, j, te, nu: (t, 0)),
                      pl.BlockSpec((None, None, d, tf), col),
                      pl.BlockSpec((None, None, d, tf), col),
                      pl.BlockSpec((None, None, tf, d), rowb)],
            out_specs=pl.BlockSpec((tm * SUBLANES, LANES), lambda t, j, te, nu: (t, 0)),
            scratch_shapes=[pltpu.VMEM((tm, d), F32)]),
        out_shape=jax.ShapeDtypeStruct((n_tiles * tm * SUBLANES, LANES), F32),
        compiler_params=_params(("arbitrary", "arbitrary")),
        name="moe_experts",
    )(tile_e, n_used, xg, wg, wu, wd)


def _combine_kernel(slot_ref, x_ref, sel_ref, mod_ref, g_ref, y_hbm, o_ref, r1_ref, r2_ref, sems,
                    *, d, m, rows_per_sample, n_steps, final):
    t = x_ref.shape[0]
    n_r = rows_per_sample // t
    step = pl.program_id(0) * n_r + pl.program_id(1)
    buf = step % 2

    def start_tile(tile, dst_buf):
        base = tile * t

        def issue(k, carry):
            dst = pl.ds(pl.multiple_of(k * SUBLANES, SUBLANES), SUBLANES)
            s1 = pl.multiple_of(slot_ref[base + k], SUBLANES)
            s2 = pl.multiple_of(slot_ref[m + base + k], SUBLANES)
            pltpu.make_async_copy(y_hbm.at[pl.ds(s1, SUBLANES)], r1_ref.at[dst_buf, dst],
                                  sems.at[dst_buf, 0]).start(priority=0)
            pltpu.make_async_copy(y_hbm.at[pl.ds(s2, SUBLANES)], r2_ref.at[dst_buf, dst],
                                  sems.at[dst_buf, 1]).start(priority=1)
            return carry

        lax.fori_loop(0, t, issue, 0, unroll=8)

    @pl.when(step == 0)
    def _():
        start_tile(0, 0)

    pltpu.make_async_copy(y_hbm.at[pl.ds(0, t * SUBLANES)], r1_ref.at[buf], sems.at[buf, 0]).wait()
    pltpu.make_async_copy(y_hbm.at[pl.ds(0, t * SUBLANES)], r2_ref.at[buf], sems.at[buf, 1]).wait()

    @pl.when(step + 1 < n_steps)
    def _():
        start_tile(step + 1, 1 - buf)

    y = (sel_ref[:, 0:1] * _load_token_major(r1_ref.at[buf], t)
         + sel_ref[:, 1:2] * _load_token_major(r2_ref.at[buf], t))
    x = x_ref[...] + mod_ref[:, 5 * d:6 * d] * y
    if final:
        x = x * lax.rsqrt(jnp.mean(x * x, axis=-1, keepdims=True) + EPS) * g_ref[...]
    o_ref[...] = x


def _combine(xs, sel, mod_l, y_sorted, slot_of, final_g, n_lat_tiles, n_row_tiles, final):
    b, _, d = xs.shape
    t = ROW_TILE
    rows = n_row_tiles * t
    row = lambda i, r, s: (i, r, 0)
    return pl.pallas_call(
        functools.partial(_combine_kernel, d=d, m=b * rows, rows_per_sample=rows, n_steps=b * n_row_tiles,
                          final=final),
        grid_spec=pltpu.PrefetchScalarGridSpec(
            num_scalar_prefetch=1,
            grid=(b, n_row_tiles),
            in_specs=[pl.BlockSpec((None, t, d), row),
                      pl.BlockSpec((None, t, LANES), row),
                      pl.BlockSpec((None, 1, mod_l.shape[-1]),
                                   lambda i, r, s: (jnp.where(r < n_lat_tiles, i, b), 0, 0)),
                      pl.BlockSpec((1, d), lambda i, r, s: (0, 0)),
                      pl.BlockSpec(memory_space=pl.ANY)],
            out_specs=pl.BlockSpec((None, t, d), row),
            scratch_shapes=[pltpu.VMEM((2, t * SUBLANES, LANES), F32), pltpu.VMEM((2, t * SUBLANES, LANES), F32),
                            pltpu.SemaphoreType.DMA((2, 2))]),
        out_shape=jax.ShapeDtypeStruct((b, rows, d), F32),
        compiler_params=_params(("arbitrary", "arbitrary")),
        name="moe_combine",
    )(slot_of, xs, sel, mod_l, final_g, y_sorted)


def _rope_tables(s_lat, n_ctx):
    rows = s_lat // GRID_W
    pos_row = jnp.repeat(jnp.arange(rows, dtype=jnp.int32), GRID_W).astype(F32)
    pos_col = jnp.tile(jnp.arange(GRID_W, dtype=jnp.int32), rows).astype(F32)
    quarter = HEAD_DIM // 4
    inv_freq = ROPE_THETA ** (-jnp.arange(quarter, dtype=F32) / quarter)
    ang_r = pos_row[:, None] * inv_freq
    ang_c = pos_col[:, None] * inv_freq
    ang = jnp.concatenate([ang_r, ang_r, ang_c, ang_c], axis=-1)
    cos, sin = jnp.cos(ang), jnp.sin(ang)
    first = (np.arange(HEAD_DIM) % (2 * quarter)) < quarter
    sin_lo = jnp.where(first, -sin, 0.0)
    sin_hi = jnp.where(first, 0.0, sin)

    def widen(tab, ctx_fill):
        tab = jnp.concatenate([tab, jnp.full((n_ctx, HEAD_DIM), ctx_fill, F32)], axis=0)
        return jnp.tile(tab, (1, LANES // HEAD_DIM))

    return widen(cos, 1.0), widen(sin_lo, 0.0), widen(sin_hi, 0.0)


def _dft_tables(n):
    k = np.arange(n, dtype=np.int64)
    ang = 2.0 * np.pi * ((k[:, None] * k[None, :]) % n).astype(np.float64) / n
    return np.cos(ang), np.sin(ang)


def _constants(s_lat, n_ctx):
    c64, s64 = _dft_tables(F_GROUP_DIM)
    eye = np.eye(F_GROUPS)
    cn, sn = _dft_tables(s_lat)
    cc, sc = _dft_tables(n_ctx)
    bf = lambda a: jnp.asarray(a, dtype=F32).astype(BF16)
    pad = np.zeros((n_ctx, s_lat))
    cn, sn = np.concatenate([cn, pad]), np.concatenate([sn, pad])
    return dict(
        gmat=bf(np.kron(np.eye(512 // HEAD_DIM), np.ones((HEAD_DIM, HEAD_DIM)))),
        c64=bf(np.kron(eye, c64) / math.sqrt(F_GROUP_DIM)),
        s64=bf(np.kron(eye, s64) / math.sqrt(F_GROUP_DIM)),
        cn=bf(cn / math.sqrt(s_lat)), sn=bf(sn / math.sqrt(s_lat)),
        cc=bf(cc / math.sqrt(n_ctx)), sc=bf(sc / math.sqrt(n_ctx)),
    )


def _dup_heads(w, off, n_heads):
    cols = []
    for hd in range(n_heads):
        blk = w[:, off + hd * HEAD_DIM: off + (hd + 1) * HEAD_DIM]
        cols += [blk, blk]
    return cols


def _relayout_w_in(w):
    cols = [w[:, _OFF["aq"]:_OFF["bq"]], w[:, _OFF["bq"]:_OFF["bk"]],
            w[:, _OFF["cq"]:_OFF["ck"]], w[:, _OFF["fz"]:_OFF["gt"]]]
    cols += _dup_heads(w, _OFF["bk"], B_KV) + _dup_heads(w, _OFF["bv"], B_KV)
    cols += _dup_heads(w, _OFF["ck"], C_KV) + _dup_heads(w, _OFF["cv"], C_KV)
    return jnp.concatenate(cols, axis=1).astype(BF16)


def kernel(x, c, ctx, c_ctx, w_mod, b_mod, norm1_g, norm2_g, w_in, diff_lambda, diff_subln_g, qk_norm_g,
           sink_logits, w_branch, w_out, ffn_w_gate, ffn_w_up, ffn_w_down, router_w, moe_w_gate, moe_w_up,
           moe_w_down, final_g):
    b, s_lat, d = x.shape
    n_ctx = ctx.shape[1]
    depth = w_mod.shape[0]
    nt = s_lat + n_ctx
    t = ROW_TILE
    assert s_lat % t == 0 and n_ctx % t == 0 and s_lat % n_ctx == 0 and b < MOD_ROWS
    assert w_in.shape[-1] == _OFF["end"] and d * N_BRANCH == _OFF["end"] - _OFF["gt"]
    n_lat_tiles = s_lat // t

    xs = jnp.concatenate([x, ctx], axis=1)
    cvec = jnp.concatenate([c, c_ctx[None], jnp.zeros((MOD_ROWS - b - 1, d), F32)], axis=0)
    mod = _modulation(cvec, w_mod, b_mod).reshape(depth, MOD_ROWS, 1, 6 * d)
    tabs = _rope_tables(s_lat, n_ctx)
    consts = _constants(s_lat, n_ctx)

    out = None
    for l in range(depth):
        last = l == depth - 1
        mod_l = mod[l]
        p = _inproj(xs, mod_l, norm1_g[l][None], _relayout_w_in(w_in[l]), tabs,
                    jnp.tile(qk_norm_g[l, 0], 512 // HEAD_DIM)[None], jnp.tile(qk_norm_g[l, 1], 256 // HEAD_DIM)[None],
                    consts, s_lat)
        p = (None, *p)
        w_gates = w_in[l][:, _OFF["gt"]:_OFF["end"]].astype(BF16)
        lam_init = 0.8 - 0.6 * math.exp(-0.3 * l)
        n_q_tiles = n_lat_tiles if last else nt // t
        oa, ob, oc = _attention(p, diff_lambda[l], diff_subln_g[l][None], sink_logits[l],
                                n_lat_tiles, n_q_tiles, s_lat, lam_init)
        od = _fourier(p[10], p[11], consts, s_lat, n_q_tiles * t)
        xs = _merge(xs, mod_l, norm1_g[l][None], w_gates, (oa, ob, oc, od), w_branch[l].astype(BF16),
                    w_out[l].astype(BF16), s_lat, n_q_tiles * t)

        i = l // 2
        if l % 2 == 0:
            assert not last
            xs = _ffn(xs, mod_l, norm2_g[l][None], ffn_w_gate[i].astype(BF16), ffn_w_up[i].astype(BF16),
                      ffn_w_down[i].astype(BF16), s_lat)
        else:
            w_r = jnp.pad(router_w[i], ((0, 0), (0, LANES - N_EXPERTS)))
            n_row_tiles = n_q_tiles
            m = b * n_row_tiles * t
            h2, sel = _router(xs, mod_l, norm2_g[l][None], w_r, s_lat, n_row_tiles * t)
            tm = _moe_tile_rows(m, moe_w_gate.shape[-1])
            tables = _dispatch(sel.reshape(m, LANES), tm)
            xg = _gather_rows(h2.reshape(m * SUBLANES, LANES), tables, tm)
            y_sorted = _experts(xg, tables, moe_w_gate, moe_w_up, moe_w_down, i, tm)
            res = _combine(xs, sel, mod_l, y_sorted, tables[3], final_g[None], n_lat_tiles, n_row_tiles, last)
            if last:
                out = res
            else:
                xs = res
    if out is None:
        raise NotImplementedError("final normalisation is fused into the last (mixture-of-experts) layer")
    return out
```

```python
import functools
import math

import numpy as np
import jax
import jax.numpy as jnp
from jax import lax
from jax.experimental import pallas as pl
from jax.experimental.pallas import tpu as pltpu

F32 = jnp.float32
BF16 = jnp.bfloat16

GRID_W = 64
HEAD_DIM = 64
ROPE_THETA = 10000.0
WINDOW = 128
A_HEADS = 4
B_HEADS = 8
B_KV = 2
C_HEADS = 8
C_KV = 2
F_GROUPS = 8
F_GROUP_DIM = 64
BRANCH_W = 512
N_BRANCH = 4
N_EXPERTS = 8
EPS = 1e-6
SUBLN_EPS = 1e-5
NEG_INF = -1e30
LOG2E = math.log2(math.e)

LANES = 128
SUBLANES = 8
ROW_TILE = 256
WIDE_TILES = (768, 512, 256)
FFN_CHUNK = 768
MOE_TILES_PER_EXPERT = 4
MOE_FF_TILE = 512
MOD_ROWS = 16
VMEM_LIMIT = 56 * 1024 * 1024

_OFF = dict(aq=0, ak=512, av=1024, bq=1536, bk=2048, bv=2176, cq=2304, ck=2816, cv=2944,
            fz=3072, gt=3584, end=7680)
_P = dict(aq=0, ak=512, av=1024, bq=1536, cq=2048, fz=2560, bk=3072, bv=3328, ck=3584, cv=3840, end=4096)


def _params(sem):
    return pltpu.CompilerParams(dimension_semantics=sem, vmem_limit_bytes=VMEM_LIMIT)


def _full(shape):
    n = len(shape)
    return pl.BlockSpec(shape, lambda *_: (0,) * n)


def _silu(x):
    return x / (1.0 + jnp.exp(-x))


def _sigmoid(x):
    return 1.0 / (1.0 + jnp.exp(-x))


def _adaln(x, g, shift, scale):
    y = x * lax.rsqrt(jnp.mean(x * x, axis=-1, keepdims=True) + EPS) * g
    return y * (1.0 + scale) + shift


def _mod_kernel(c_ref, w_ref, b_ref, o_ref):
    s = _silu(c_ref[...]).astype(BF16)
    o_ref[...] = jnp.dot(s, w_ref[...].astype(BF16), preferred_element_type=F32) + b_ref[...]


def _modulation(cvec, w_mod, b_mod):
    depth, d, n = w_mod.shape
    tn = 1536 if n % 1536 == 0 else n
    return pl.pallas_call(
        _mod_kernel,
        grid=(depth, n // tn),
        in_specs=[_full((MOD_ROWS, d)),
                  pl.BlockSpec((None, d, tn), lambda l, j: (l, 0, j)),
                  pl.BlockSpec((None, 1, tn), lambda l, j: (l, 0, j))],
        out_specs=pl.BlockSpec((None, MOD_ROWS, tn), lambda l, j: (l, 0, j)),
        out_shape=jax.ShapeDtypeStruct((depth, MOD_ROWS, n), F32),
        compiler_params=_params(("arbitrary", "arbitrary")),
        name="modulation",
    )(cvec, w_mod, b_mod.reshape(depth, 1, n))


def _rope(p, cos, sin_lo, sin_hi):
    outs = []
    for c in range(p.shape[1] // LANES):
        xc = p[:, c * LANES:(c + 1) * LANES]
        outs.append(xc * cos + pltpu.roll(xc, LANES - 16, 1) * sin_lo + pltpu.roll(xc, 16, 1) * sin_hi)
    return outs[0] if len(outs) == 1 else jnp.concatenate(outs, axis=1)


def _group_rmsnorm(y, gmat, g):
    ss = jnp.dot((y * y).astype(BF16), gmat, preferred_element_type=F32)
    return y * lax.rsqrt(ss * (1.0 / HEAD_DIM) + EPS) * g


def _inproj_kernel(x_ref, modb_ref, modc_ref, g_ref, w_ref, cos_ref, slo_ref, shi_ref, qg_ref, kg_ref, gmat_ref,
                   c64_ref, s64_ref,
                   aq_ref, akt_ref, av_ref, bq_ref, bkt_ref, bv_ref, cq_ref, ckt_ref, cv_ref,
                   fzc_ref, fzs_ref, *, d, s_lat):
    mod = _row_mod(modb_ref, modc_ref, x_ref.shape[0], s_lat, d)
    h = _adaln(x_ref[...], g_ref[...], mod(0), mod(1)).astype(BF16)
    cos, slo, shi = cos_ref[...], slo_ref[...], shi_ref[...]
    scale = HEAD_DIM ** -0.5 * LOG2E

    def proj(name, width):
        return lambda: jnp.dot(h, w_ref[:, _P[name]:_P[name] + width], preferred_element_type=F32)

    def finish_bq(p):
        bq_ref[...] = (_rope(_group_rmsnorm(p, gmat_ref[...], qg_ref[...]), cos, slo, shi) * scale).astype(BF16)

    def finish_bk(p):
        bkt_ref[...] = _rope(_group_rmsnorm(p, gmat_ref[0:256, 0:256], kg_ref[...]), cos, slo, shi).T.astype(BF16)

    def finish_fz(p):
        fz = p.astype(BF16)
        fzc_ref[...] = jnp.dot(fz, c64_ref[...], preferred_element_type=F32).astype(BF16)
        fzs_ref[...] = jnp.dot(fz, s64_ref[...], preferred_element_type=F32).astype(BF16)

    def store(ref, fn):
        def finish(p):
            ref[...] = fn(p).astype(BF16)
        return finish

    rope = lambda p: _rope(p, cos, slo, shi)
    groups = [
        (proj("aq", 512), store(aq_ref, lambda p: rope(p) * scale)),
        (proj("ak", 512), store(akt_ref, lambda p: rope(p).T)),
        (proj("av", 512), store(av_ref, lambda p: p)),
        (proj("bq", 512), finish_bq),
        (proj("bk", 256), finish_bk),
        (proj("bv", 256), store(bv_ref, lambda p: p)),
        (proj("cq", 512), store(cq_ref, lambda p: rope(p) * scale)),
        (proj("ck", 256), store(ckt_ref, lambda p: rope(p).T)),
        (proj("cv", 256), store(cv_ref, lambda p: p)),
        (proj("fz", 512), finish_fz),
    ]
    for (_, finish), p in zip(groups, _pipelined_scores([g[0] for g in groups])):
        finish(p)


def _inproj(xs, mod_l, g, w_p, tabs, qg, kg, consts, s_lat):
    b, nt, d = xs.shape
    t = _wide_tile(nt)
    row = lambda i, r: (i, r, 0)
    tr = lambda i, r: (i, 0, r)
    tab = pl.BlockSpec((t, LANES), lambda i, r: (r, 0))

    def out(width):
        return pl.BlockSpec((None, t, width), row), jax.ShapeDtypeStruct((b, nt, width), BF16)

    def out_t(width):
        return pl.BlockSpec((None, width, t), tr), jax.ShapeDtypeStruct((b, width, nt), BF16)

    outs = [out(512), out_t(512), out(512), out(512), out_t(256), out(256),
            out(512), out_t(256), out(256), out(512), out(512)]
    return pl.pallas_call(
        functools.partial(_inproj_kernel, d=d, s_lat=s_lat),
        grid=(b, nt // t),
        in_specs=[pl.BlockSpec((None, t, d), row), *_wide_mod_specs(mod_l, b),
                  _full((1, d)),
                  pl.BlockSpec(w_p.shape, lambda i, r: (0, 0), pipeline_mode=pl.Buffered(1)),
                  tab, tab, tab,
                  _full((1, 512)), _full((1, 256)), _full((512, 512)), _full((512, 512)), _full((512, 512))],
        out_specs=[o[0] for o in outs],
        out_shape=[o[1] for o in outs],
        compiler_params=_params(("arbitrary", "arbitrary")),
        name="inproj",
    )(xs, mod_l, mod_l, g, w_p, tabs[0], tabs[1], tabs[2], qg, kg, consts["gmat"], consts["c64"], consts["s64"])


def _half_masks(shape):
    lane = lax.broadcasted_iota(jnp.int32, shape, 1)
    return lane < HEAD_DIM, lane >= HEAD_DIM


def _exp2_bf16(s):
    return jnp.exp2(s - jnp.max(s, axis=-1, keepdims=True)).astype(BF16)


def _pipelined_scores(score_fns):
    nxt = score_fns[0]()
    for i in range(len(score_fns)):
        cur = nxt
        if i + 1 < len(score_fns):
            nxt = score_fns[i + 1]()
        yield cur


def _attn_a_kernel(q_ref, kt_ref, v_ref, lp_ref, g_ref, o_ref, *, n_lat_tiles, s_lat, lam_init):
    r = pl.program_id(1)
    lp = lp_ref[...]
    lam = (jnp.exp(jnp.sum(lp[0:1] * lp[1:2], axis=-1, keepdims=True))
           - jnp.exp(jnp.sum(lp[2:3] * lp[3:4], axis=-1, keepdims=True)) + lam_init)
    g = g_ref[...] * (1.0 - lam_init)
    nt = kt_ref.shape[1]

    def body(k0):
        lo, hi = _half_masks((q_ref.shape[0], LANES))
        ones = jnp.ones((nt - k0, LANES), BF16)

        def score_fn(hd, msk):
            cs = slice(hd * LANES, (hd + 1) * LANES)
            return lambda: jnp.dot(jnp.where(msk, q_ref[:, cs], 0), kt_ref[cs, k0:nt], preferred_element_type=F32)

        fns = [score_fn(hd, msk) for hd in range(A_HEADS) for msk in (lo, hi)]
        outs = []
        for i, sc in enumerate(_pipelined_scores(fns)):
            hd = i // 2
            cs = slice(hd * LANES, (hd + 1) * LANES)
            vo = jnp.concatenate([v_ref[k0:nt, cs], ones], axis=1)
            full = jnp.dot(_exp2_bf16(sc), vo, preferred_element_type=F32)
            outs.append(full[:, 0:LANES] * (1.0 / full[:, LANES:LANES + 1]))
            if len(outs) == 2:
                o = outs[0] - lam * outs[1]
                outs = []
                o = o * lax.rsqrt(jnp.mean(o * o, axis=-1, keepdims=True) + SUBLN_EPS) * g
                o_ref[:, cs] = o.astype(BF16)

    pl.when(r < n_lat_tiles)(lambda: body(0))
    pl.when(r >= n_lat_tiles)(lambda: body(s_lat))


def _attn_b_kernel(q_ref, kt_ref, v_ref, o_ref, *, n_lat_tiles, s_lat):
    r = pl.program_id(1)
    nt = kt_ref.shape[1]

    def body(k0):
        lo, hi = _half_masks((q_ref.shape[0], LANES))
        ones = jnp.ones((nt - k0, LANES), BF16)

        def score_fn(pair, msk):
            kv = pair // (B_HEADS // B_KV // 2)
            q = q_ref[:, pair * LANES:(pair + 1) * LANES]
            return lambda: jnp.dot(jnp.where(msk, q, 0), kt_ref[kv * LANES:(kv + 1) * LANES, k0:nt],
                                   preferred_element_type=F32)

        fns = [score_fn(pair, msk) for pair in range(B_HEADS // 2) for msk in (lo, hi)]
        halves = []
        for i, sc in enumerate(_pipelined_scores(fns)):
            pair = i // 2
            kv = pair // (B_HEADS // B_KV // 2)
            vo = jnp.concatenate([v_ref[k0:nt, kv * LANES:(kv + 1) * LANES], ones], axis=1)
            full = jnp.dot(_exp2_bf16(sc), vo, preferred_element_type=F32)
            halves.append(full[:, 0:LANES] * (1.0 / full[:, LANES:LANES + 1]))
            if len(halves) == 2:
                o_ref[:, pair * LANES:(pair + 1) * LANES] = jnp.where(lo, halves[0], halves[1]).astype(BF16)
                halves = []

    pl.when(r < n_lat_tiles)(lambda: body(0))
    pl.when(r >= n_lat_tiles)(lambda: body(s_lat))


def _attn_c_kernel(sink_ref, q_ref, ktl_ref, ktm_ref, ktr_ref, ktc_ref, vl_ref, vm_ref, vr_ref, vc_ref, o_ref,
                   *, n_lat_tiles, s_lat):
    r = pl.program_id(1)
    t = q_ref.shape[0]
    n_band = t + 2 * WINDOW
    n_keys = n_band + ktc_ref.shape[1]
    u = lax.broadcasted_iota(jnp.int32, (t, n_keys), 1)
    row = lax.broadcasted_iota(jnp.int32, (t, n_keys), 0)
    j = r * t - WINDOW + u
    in_band = (u >= row) & (u <= row + 2 * WINDOW) & (j >= 0) & (j < s_lat) & (r < n_lat_tiles)
    valid = in_band | (u >= n_band)
    lo, hi = _half_masks((t, LANES))
    ones = jnp.ones((n_keys, LANES), BF16)
    kts, vos = [], []
    for kv in range(C_KV):
        ks = slice(kv * LANES, (kv + 1) * LANES)
        kts.append(jnp.concatenate([ktl_ref[ks, :], ktm_ref[ks, :], ktr_ref[ks, :], ktc_ref[ks, :]], axis=1))
        v = jnp.concatenate([vl_ref[:, ks], vm_ref[:, ks], vr_ref[:, ks], vc_ref[:, ks]], axis=0)
        vos.append(jnp.concatenate([v, ones], axis=1))

    def score_fn(pair, msk):
        kv = pair // (C_HEADS // C_KV // 2)
        q = q_ref[:, pair * LANES:(pair + 1) * LANES]
        return lambda: jnp.dot(jnp.where(msk, q, 0), kts[kv], preferred_element_type=F32)

    fns = [score_fn(pair, msk) for pair in range(C_HEADS // 2) for msk in (lo, hi)]
    halves = []
    for i, sc in enumerate(_pipelined_scores(fns)):
        pair = i // 2
        kv = pair // (C_HEADS // C_KV // 2)
        sink = sink_ref[i] * LOG2E
        sc = jnp.where(valid, sc, NEG_INF)
        m = jnp.maximum(jnp.max(sc, axis=-1, keepdims=True), sink)
        full = jnp.dot(jnp.exp2(sc - m).astype(BF16), vos[kv], preferred_element_type=F32)
        l = full[:, LANES:LANES + 1] + jnp.exp2(sink - m)
        halves.append(full[:, 0:LANES] * (1.0 / l))
        if len(halves) == 2:
            o_ref[:, pair * LANES:(pair + 1) * LANES] = jnp.where(lo, halves[0], halves[1]).astype(BF16)
            halves = []


def _attention(p, lp, subln_g, sink, n_lat_tiles, n_q_tiles, s_lat, lam_init):
    (_, aq, akt, av, bq, bkt, bv, cq, ckt, cv, _, _) = p
    b, nt, _ = aq.shape
    t = ROW_TILE
    grid = (b, n_q_tiles)
    row = lambda i, r: (i, r, 0)
    whole = lambda i, r: (i, 0, 0)
    sem = _params(("arbitrary", "arbitrary"))
    o_spec = pl.BlockSpec((None, t, 512), row)
    o_shape = jax.ShapeDtypeStruct((b, n_q_tiles * t, 512), BF16)

    oa = pl.pallas_call(
        functools.partial(_attn_a_kernel, n_lat_tiles=n_lat_tiles, s_lat=s_lat, lam_init=lam_init),
        grid=grid,
        in_specs=[pl.BlockSpec((None, t, 512), row), pl.BlockSpec((None, 512, nt), whole),
                  pl.BlockSpec((None, nt, 512), whole), _full((4, HEAD_DIM)), _full((1, 2 * HEAD_DIM))],
        out_specs=o_spec, out_shape=o_shape, compiler_params=sem, name="attn_diff",
    )(aq, akt, av, lp, subln_g)

    ob = pl.pallas_call(
        functools.partial(_attn_b_kernel, n_lat_tiles=n_lat_tiles, s_lat=s_lat),
        grid=grid,
        in_specs=[pl.BlockSpec((None, t, 512), row), pl.BlockSpec((None, 256, nt), whole),
                  pl.BlockSpec((None, nt, 256), whole)],
        out_specs=o_spec, out_shape=o_shape, compiler_params=sem, name="attn_gqa",
    )(bq, bkt, bv)

    wpt = t // WINDOW
    last = s_lat // WINDOW - 1
    left = lambda r: jnp.clip(r * wpt - 1, 0, last)
    mid = lambda r: jnp.minimum(r, n_lat_tiles - 1)
    right = lambda r: jnp.clip((r + 1) * wpt, 0, last)
    n_ctx = nt - s_lat
    ctx_blk = s_lat // n_ctx
    oc = pl.pallas_call(
        functools.partial(_attn_c_kernel, n_lat_tiles=n_lat_tiles, s_lat=s_lat),
        grid=grid,
        in_specs=[pl.BlockSpec(memory_space=pltpu.SMEM),
                  pl.BlockSpec((None, t, 512), row),
                  pl.BlockSpec((None, 256, WINDOW), lambda i, r: (i, 0, left(r))),
                  pl.BlockSpec((None, 256, t), lambda i, r: (i, 0, mid(r))),
                  pl.BlockSpec((None, 256, WINDOW), lambda i, r: (i, 0, right(r))),
                  pl.BlockSpec((None, 256, n_ctx), lambda i, r: (i, 0, ctx_blk)),
                  pl.BlockSpec((None, WINDOW, 256), lambda i, r: (i, left(r), 0)),
                  pl.BlockSpec((None, t, 256), lambda i, r: (i, mid(r), 0)),
                  pl.BlockSpec((None, WINDOW, 256), lambda i, r: (i, right(r), 0)),
                  pl.BlockSpec((None, n_ctx, 256), lambda i, r: (i, ctx_blk, 0))],
        out_specs=o_spec, out_shape=o_shape, compiler_params=sem, name="attn_window",
    )(sink, cq, ckt, ckt, ckt, ckt, cv, cv, cv, cv)
    return oa, ob, oc


def _fourier_kernel(cn_ref, sn_ref, cc_ref, sc_ref, zc_ref, zs_ref, o_ref, *, s_lat, rows):
    r = pl.program_id(1)
    t = o_ref.shape[0]
    nt = zc_ref.shape[0]
    n_full = s_lat // t
    lat_rows = s_lat - n_full * t

    def latent(n):
        return (jnp.dot(cn_ref[0:n, :], zc_ref[0:s_lat, :], preferred_element_type=F32)
                - jnp.dot(sn_ref[0:n, :], zs_ref[0:s_lat, :], preferred_element_type=F32)).astype(BF16)

    @pl.when(r < n_full)
    def _():
        o_ref[...] = latent(t)

    if rows > s_lat:
        @pl.when(r >= n_full)
        def _():
            if lat_rows:
                o_ref[0:lat_rows, :] = latent(lat_rows)
            y = (jnp.dot(cc_ref[...], zc_ref[s_lat:nt, :], preferred_element_type=F32)
                 - jnp.dot(sc_ref[...], zs_ref[s_lat:nt, :], preferred_element_type=F32))
            o_ref[lat_rows:t, :] = y.astype(BF16)


def _fourier(fzc, fzs, consts, s_lat, rows):
    b, nt, w = fzc.shape
    t = _wide_tile(rows)
    n_ctx = nt - s_lat
    assert rows == s_lat or (rows == nt and nt - s_lat // t * t == t)
    tile = lambda i, r: (r, 0)
    whole = lambda i, r: (i, 0, 0)
    return pl.pallas_call(
        functools.partial(_fourier_kernel, s_lat=s_lat, rows=rows),
        grid=(b, rows // t),
        in_specs=[pl.BlockSpec((t, s_lat), tile), pl.BlockSpec((t, s_lat), tile),
                  _full((n_ctx, n_ctx)), _full((n_ctx, n_ctx)),
                  pl.BlockSpec((None, nt, w), whole), pl.BlockSpec((None, nt, w), whole)],
        out_specs=pl.BlockSpec((None, t, w), lambda i, r: (i, r, 0)),
        out_shape=jax.ShapeDtypeStruct((b, rows, w), BF16),
        compiler_params=_params(("arbitrary", "arbitrary")),
        name="fourier",
    )(consts["cn"], consts["sn"], consts["cc"], consts["sc"], fzc, fzs)


def _wide_tile(rows):
    for t in WIDE_TILES:
        if rows % t == 0:
            return t
    raise ValueError(rows)


def _wide_mod_specs(mod_l, b):
    n = mod_l.shape[-1]
    return [pl.BlockSpec((None, 1, n), lambda i, r: (i, 0, 0)), pl.BlockSpec((None, 1, n), lambda i, r: (b, 0, 0))]


def _row_mod(modb_ref, modc_ref, rows, s_lat, d):
    is_lat = pl.program_id(1) * rows + lax.broadcasted_iota(jnp.int32, (rows, 1), 0) < s_lat
    return lambda k: jnp.where(is_lat, modb_ref[:, k * d:(k + 1) * d], modc_ref[:, k * d:(k + 1) * d])


def _merge_kernel(x_ref, modb_ref, modc_ref, g_ref, wgt_ref, oa_ref, ob_ref, oc_ref, od_ref, wbr_ref, wo_ref,
                  o_ref, *, d, s_lat):
    mod = _row_mod(modb_ref, modc_ref, x_ref.shape[0], s_lat, d)
    x = x_ref[...]
    h = _adaln(x, g_ref[...], mod(0), mod(1)).astype(BF16)
    merged = None
    for i, br in enumerate((oa_ref, ob_ref, oc_ref, od_ref)):
        gate = _sigmoid(jnp.dot(h, wgt_ref[:, i * d:(i + 1) * d], preferred_element_type=F32))
        y = gate * jnp.dot(br[...], wbr_ref[i], preferred_element_type=F32)
        merged = y if merged is None else merged + y
    out = jnp.dot(merged.astype(BF16), wo_ref[...], preferred_element_type=F32)
    o_ref[...] = x + mod(2) * out


def _merge(xs, mod_l, g, w_gates, branches, w_br, w_o, s_lat, rows):
    b, _, d = xs.shape
    t = _wide_tile(rows)
    row = lambda i, r: (i, r, 0)
    br_spec = pl.BlockSpec((None, t, BRANCH_W), row)
    const = lambda shape: pl.BlockSpec(shape, lambda i, r: (0,) * len(shape), pipeline_mode=pl.Buffered(1))
    return pl.pallas_call(
        functools.partial(_merge_kernel, d=d, s_lat=s_lat),
        grid=(b, rows // t),
        in_specs=[pl.BlockSpec((None, t, d), row), *_wide_mod_specs(mod_l, b),
                  _full((1, d)), const(w_gates.shape),
                  br_spec, br_spec, br_spec, br_spec,
                  const(w_br.shape), const(w_o.shape)],
        out_specs=pl.BlockSpec((None, t, d), row),
        out_shape=jax.ShapeDtypeStruct((b, rows, d), F32),
        compiler_params=_params(("arbitrary", "arbitrary")),
        name="merge",
    )(xs, mod_l, mod_l, g, w_gates, *branches, w_br, w_o)


def _ffn_kernel(x_ref, modb_ref, modc_ref, g_ref, wg_ref, wu_ref, wd_ref, o_ref, *, d, s_lat):
    mod = _row_mod(modb_ref, modc_ref, x_ref.shape[0], s_lat, d)
    x = x_ref[...]
    h = _adaln(x, g_ref[...], mod(3), mod(4)).astype(BF16)
    dff = wg_ref.shape[1]
    y = None
    for c0 in range(0, dff, FFN_CHUNK):
        c1 = min(c0 + FFN_CHUNK, dff)
        a = jnp.dot(h, wg_ref[:, c0:c1], preferred_element_type=F32)
        u = jnp.dot(h, wu_ref[:, c0:c1], preferred_element_type=F32)
        part = jnp.dot((_silu(a) * u).astype(BF16), wd_ref[c0:c1, :], preferred_element_type=F32)
        y = part if y is None else y + part
    o_ref[...] = x + mod(5) * y


def _ffn(xs, mod_l, g, wg, wu, wd, s_lat):
    b, nt, d = xs.shape
    t = _wide_tile(nt)
    row = lambda i, r: (i, r, 0)
    const = lambda shape: pl.BlockSpec(shape, lambda i, r: (0, 0), pipeline_mode=pl.Buffered(1))
    return pl.pallas_call(
        functools.partial(_ffn_kernel, d=d, s_lat=s_lat),
        grid=(b, nt // t),
        in_specs=[pl.BlockSpec((None, t, d), row), *_wide_mod_specs(mod_l, b),
                  _full((1, d)), const(wg.shape), const(wu.shape), const(wd.shape)],
        out_specs=pl.BlockSpec((None, t, d), row),
        out_shape=jax.ShapeDtypeStruct(xs.shape, F32),
        compiler_params=_params(("arbitrary", "arbitrary")),
        name="ffn_dense",
    )(xs, mod_l, mod_l, g, wg, wu, wd)


def _store_token_major(ref, val):
    rows = val.shape[0]
    for c in range(SUBLANES):
        ref[pl.ds(c, rows, stride=SUBLANES), :] = val[:, c * LANES:(c + 1) * LANES]


def _load_token_major(ref, rows):
    return jnp.concatenate([ref[pl.ds(c, rows, stride=SUBLANES), :] for c in range(SUBLANES)], axis=1)


def _router_kernel(x_ref, modb_ref, modc_ref, g_ref, wr_ref, h_ref, sel_ref, *, d, s_lat):
    mod = _row_mod(modb_ref, modc_ref, x_ref.shape[0], s_lat, d)
    h = _adaln(x_ref[...], g_ref[...], mod(3), mod(4))
    _store_token_major(h_ref, h)
    logits = jnp.dot(h, wr_ref[...], preferred_element_type=F32, precision=lax.Precision.HIGHEST)
    lane = lax.broadcasted_iota(jnp.int32, logits.shape, 1)
    logits = jnp.where(lane < N_EXPERTS, logits, -jnp.inf)
    m1 = jnp.max(logits, axis=-1, keepdims=True)
    i1 = jnp.min(jnp.where(logits == m1, lane, LANES), axis=-1, keepdims=True)
    rest = jnp.where(lane == i1, -jnp.inf, logits)
    m2 = jnp.max(rest, axis=-1, keepdims=True)
    i2 = jnp.min(jnp.where(rest == m2, lane, LANES), axis=-1, keepdims=True)
    e2 = jnp.exp(m2 - m1)
    w1 = 1.0 / (1.0 + e2)
    sel_ref[...] = (jnp.where(lane == 0, w1, 0.0) + jnp.where(lane == 1, e2 * w1, 0.0)
                    + jnp.where(lane == 2, i1.astype(F32), 0.0) + jnp.where(lane == 3, i2.astype(F32), 0.0))


def _router(xs, mod_l, g, w_r, s_lat, rows):
    b, _, d = xs.shape
    t = _wide_tile(rows)
    row = lambda i, r: (i, r, 0)
    return pl.pallas_call(
        functools.partial(_router_kernel, d=d, s_lat=s_lat),
        grid=(b, rows // t),
        in_specs=[pl.BlockSpec((None, t, d), row), *_wide_mod_specs(mod_l, b),
                  _full((1, d)), _full(w_r.shape)],
        out_specs=[pl.BlockSpec((None, t * SUBLANES, LANES), row), pl.BlockSpec((None, t, LANES), row)],
        out_shape=[jax.ShapeDtypeStruct((b, rows * SUBLANES, LANES), F32),
                   jax.ShapeDtypeStruct((b, rows, LANES), F32)],
        compiler_params=_params(("arbitrary", "arbitrary")),
        name="moe_router",
    )(xs, mod_l, mod_l, g, w_r)


def _dispatch(sel, tm):
    m = sel.shape[0]
    i32 = jnp.int32
    experts = jnp.arange(N_EXPERTS, dtype=i32)

    def lookup(table, idx):
        return jnp.sum(jnp.where(idx[:, None] == experts[None, :], table[None, :], 0), axis=1)

    idx = sel[:, 2:4].astype(i32)
    e_flat = jnp.concatenate([idx[:, 0], idx[:, 1]])
    assignment = jnp.arange(2 * m, dtype=i32)
    _, order = lax.sort_key_val(e_flat, assignment)
    _, rank = lax.sort_key_val(order, assignment)
    counts = jnp.sum((e_flat[:, None] == experts[None, :]).astype(i32), axis=0)
    padded = (counts + tm - 1) // tm * tm
    pad_end = jnp.cumsum(padded)
    pad_start = pad_end - padded
    start = jnp.cumsum(counts) - counts
    n_tiles = 2 * m // tm + N_EXPERTS
    n_used = pad_end[-1] // tm
    tile = jnp.minimum(jnp.arange(n_tiles, dtype=i32), n_used - 1)
    tile_e = jnp.sum((tile[:, None] * tm >= pad_end[None, :]).astype(i32), axis=1)
    slot = jnp.arange(n_tiles * tm, dtype=i32)
    slot_e = jnp.sum((jnp.minimum(slot, pad_end[-1] - 1)[:, None] >= pad_end[None, :]).astype(i32), axis=1)
    pos = slot - lookup(pad_start, slot_e)
    valid = (pos < lookup(counts, slot_e)) & (slot < pad_end[-1])
    src = jnp.where(valid, order[jnp.clip(lookup(start, slot_e) + pos, 0, 2 * m - 1)] % m, 0)
    slot_of = lookup(pad_start - start, e_flat) + rank
    return (tile_e, n_used.reshape(1).astype(i32), (src * SUBLANES).astype(i32), (slot_of * SUBLANES).astype(i32))


def _experts_kernel(te_ref, nu_ref, src_ref, h_hbm, wg_ref, wu_ref, wd_ref, o_ref, xg_ref, xb_ref, acc_ref, sems,
                    *, n_tiles, n_steps):
    t, j = pl.program_id(0), pl.program_id(1)
    tm = xb_ref.shape[0]
    rows_per_step = tm // n_steps
    n_used = nu_ref[0]
    used = t < n_used
    buf = t % 2

    def start_row(tile, dst_buf, r):
        src = pl.multiple_of(src_ref[tile * tm + r], SUBLANES)
        dst = pl.multiple_of(r * SUBLANES, SUBLANES)
        pltpu.make_async_copy(h_hbm.at[pl.ds(src, SUBLANES)], xg_ref.at[dst_buf, pl.ds(dst, SUBLANES)],
                              sems.at[dst_buf]).start()

    def wait_rows(b_):
        pltpu.make_async_copy(h_hbm.at[pl.ds(0, tm * SUBLANES)], xg_ref.at[b_], sems.at[b_]).wait()

    @pl.when((t == 0) & (j == 0))
    def _():
        def issue(r, carry):
            start_row(0, 0, r)
            return carry

        lax.fori_loop(0, tm, issue, 0, unroll=8)

    @pl.when((j == 0) & ((t == 0) | (t - 1 < n_used)))
    def _():
        wait_rows(buf)
        xb_ref[...] = _load_token_major(xg_ref.at[buf], tm).astype(BF16)

    @pl.when(j == 0)
    def _():
        acc_ref[...] = jnp.zeros_like(acc_ref)

    @pl.when(used)
    def _():
        nxt = jnp.minimum(t + 1, n_tiles - 1)
        for r in range(rows_per_step):
            start_row(nxt, 1 - buf, j * rows_per_step + r)
        h = xb_ref[...]
        a = jnp.dot(h, wg_ref[...].astype(BF16), preferred_element_type=F32)
        u = jnp.dot(h, wu_ref[...].astype(BF16), preferred_element_type=F32)
        act = (_silu(a) * u).astype(BF16)
        acc_ref[...] += jnp.dot(act, wd_ref[...].astype(BF16), preferred_element_type=F32)

    @pl.when(j == n_steps - 1)
    def _():
        _store_token_major(o_ref, acc_ref[...])

    @pl.when(used & (t == n_tiles - 1) & (j == n_steps - 1))
    def _():
        wait_rows(1 - buf)


def _moe_ff_tile(dff):
    return MOE_FF_TILE if dff % MOE_FF_TILE == 0 else dff


def _moe_tile_rows(m, dff):
    per_expert = 2.0 * m / N_EXPERTS
    unit = (dff // _moe_ff_tile(dff)) * 2 * SUBLANES
    rows = (per_expert + 3.0 * math.sqrt(per_expert)) / MOE_TILES_PER_EXPERT
    return int(math.ceil(rows / unit)) * unit


def _experts(h2, tables, wg, wu, wd, layer, tm):
    d, dff = wg.shape[-2:]
    assert d == SUBLANES * LANES
    tile_e, n_used, src, _ = tables
    n_tiles = tile_e.shape[0]
    tf = MOE_FF_TILE if dff % MOE_FF_TILE == 0 else dff
    nj = dff // tf
    assert tm % nj == 0
    col = lambda t, j, te, nu, src: (layer, te[t], 0, jnp.where(t < nu[0], j, nj - 1))
    rowb = lambda t, j, te, nu, src: (layer, te[t], jnp.where(t < nu[0], j, nj - 1), 0)
    return pl.pallas_call(
        functools.partial(_experts_kernel, n_tiles=n_tiles, n_steps=nj),
        grid_spec=pltpu.PrefetchScalarGridSpec(
            num_scalar_prefetch=3,
            grid=(n_tiles, nj),
            in_specs=[pl.BlockSpec(memory_space=pl.ANY),
                      pl.BlockSpec((None, None, d, tf), col),
                      pl.BlockSpec((None, None, d, tf), col),
                      pl.BlockSpec((None, None, tf, d), rowb)],
            out_specs=pl.BlockSpec((tm * SUBLANES, LANES), lambda t, j, te, nu, src: (t, 0)),
            scratch_shapes=[pltpu.VMEM((2, tm * SUBLANES, LANES), F32), pltpu.VMEM((tm, d), BF16),
                            pltpu.VMEM((tm, d), F32), pltpu.SemaphoreType.DMA((2,))]),
        out_shape=jax.ShapeDtypeStruct((n_tiles * tm * SUBLANES, LANES), F32),
        compiler_params=_params(("arbitrary", "arbitrary")),
        name="moe_experts",
    )(tile_e, n_used, src, h2, wg, wu, wd)


def _combine_kernel(slot_ref, x_ref, sel_ref, mod_ref, g_ref, y_hbm, o_ref, r1_ref, r2_ref, sems,
                    *, d, m, rows_per_sample, n_steps, final):
    t = x_ref.shape[0]
    n_r = rows_per_sample // t
    step = pl.program_id(0) * n_r + pl.program_id(1)
    buf = step % 2

    def start_tile(tile, dst_buf):
        base = tile * t

        def issue(k, carry):
            dst = pl.ds(pl.multiple_of(k * SUBLANES, SUBLANES), SUBLANES)
            s1 = pl.multiple_of(slot_ref[base + k], SUBLANES)
            s2 = pl.multiple_of(slot_ref[m + base + k], SUBLANES)
            pltpu.make_async_copy(y_hbm.at[pl.ds(s1, SUBLANES)], r1_ref.at[dst_buf, dst],
                                  sems.at[dst_buf, 0]).start(priority=0)
            pltpu.make_async_copy(y_hbm.at[pl.ds(s2, SUBLANES)], r2_ref.at[dst_buf, dst],
                                  sems.at[dst_buf, 1]).start(priority=1)
            return carry

        lax.fori_loop(0, t, issue, 0, unroll=8)

    @pl.when(step == 0)
    def _():
        start_tile(0, 0)

    pltpu.make_async_copy(y_hbm.at[pl.ds(0, t * SUBLANES)], r1_ref.at[buf], sems.at[buf, 0]).wait()
    pltpu.make_async_copy(y_hbm.at[pl.ds(0, t * SUBLANES)], r2_ref.at[buf], sems.at[buf, 1]).wait()

    @pl.when(step + 1 < n_steps)
    def _():
        start_tile(step + 1, 1 - buf)

    y = (sel_ref[:, 0:1] * _load_token_major(r1_ref.at[buf], t)
         + sel_ref[:, 1:2] * _load_token_major(r2_ref.at[buf], t))
    x = x_ref[...] + mod_ref[:, 5 * d:6 * d] * y
    if final:
        x = x * lax.rsqrt(jnp.mean(x * x, axis=-1, keepdims=True) + EPS) * g_ref[...]
    o_ref[...] = x


def _combine(xs, sel, mod_l, y_sorted, slot_of, final_g, n_lat_tiles, n_row_tiles, final):
    b, _, d = xs.shape
    t = ROW_TILE
    rows = n_row_tiles * t
    row = lambda i, r, s: (i, r, 0)
    return pl.pallas_call(
        functools.partial(_combine_kernel, d=d, m=b * rows, rows_per_sample=rows, n_steps=b * n_row_tiles,
                          final=final),
        grid_spec=pltpu.PrefetchScalarGridSpec(
            num_scalar_prefetch=1,
            grid=(b, n_row_tiles),
            in_specs=[pl.BlockSpec((None, t, d), row),
                      pl.BlockSpec((None, t, LANES), row),
                      pl.BlockSpec((None, 1, mod_l.shape[-1]),
                                   lambda i, r, s: (jnp.where(r < n_lat_tiles, i, b), 0, 0)),
                      pl.BlockSpec((1, d), lambda i, r, s: (0, 0)),
                      pl.BlockSpec(memory_space=pl.ANY)],
            out_specs=pl.BlockSpec((None, t, d), row),
            scratch_shapes=[pltpu.VMEM((2, t * SUBLANES, LANES), F32), pltpu.VMEM((2, t * SUBLANES, LANES), F32),
                            pltpu.SemaphoreType.DMA((2, 2))]),
        out_shape=jax.ShapeDtypeStruct((b, rows, d), F32),
        compiler_params=_params(("arbitrary", "arbitrary")),
        name="moe_combine",
    )(slot_of, xs, sel, mod_l, final_g, y_sorted)


def _rope_tables(s_lat, n_ctx):
    rows = s_lat // GRID_W
    pos_row = jnp.repeat(jnp.arange(rows, dtype=jnp.int32), GRID_W).astype(F32)
    pos_col = jnp.tile(jnp.arange(GRID_W, dtype=jnp.int32), rows).astype(F32)
    quarter = HEAD_DIM // 4
    inv_freq = ROPE_THETA ** (-jnp.arange(quarter, dtype=F32) / quarter)
    ang_r = pos_row[:, None] * inv_freq
    ang_c = pos_col[:, None] * inv_freq
    ang = jnp.concatenate([ang_r, ang_r, ang_c, ang_c], axis=-1)
    cos, sin = jnp.cos(ang), jnp.sin(ang)
    first = (np.arange(HEAD_DIM) % (2 * quarter)) < quarter
    sin_lo = jnp.where(first, -sin, 0.0)
    sin_hi = jnp.where(first, 0.0, sin)

    def widen(tab, ctx_fill):
        tab = jnp.concatenate([tab, jnp.full((n_ctx, HEAD_DIM), ctx_fill, F32)], axis=0)
        return jnp.tile(tab, (1, LANES // HEAD_DIM))

    return widen(cos, 1.0), widen(sin_lo, 0.0), widen(sin_hi, 0.0)


def _dft_tables(n):
    k = np.arange(n, dtype=np.int64)
    ang = 2.0 * np.pi * ((k[:, None] * k[None, :]) % n).astype(np.float64) / n
    return np.cos(ang), np.sin(ang)


def _constants(s_lat, n_ctx):
    c64, s64 = _dft_tables(F_GROUP_DIM)
    eye = np.eye(F_GROUPS)
    cn, sn = _dft_tables(s_lat)
    cc, sc = _dft_tables(n_ctx)
    bf = lambda a: jnp.asarray(a, dtype=F32).astype(BF16)
    pad = np.zeros((n_ctx, s_lat))
    cn, sn = np.concatenate([cn, pad]), np.concatenate([sn, pad])
    return dict(
        gmat=bf(np.kron(np.eye(512 // HEAD_DIM), np.ones((HEAD_DIM, HEAD_DIM)))),
        c64=bf(np.kron(eye, c64) / math.sqrt(F_GROUP_DIM)),
        s64=bf(np.kron(eye, s64) / math.sqrt(F_GROUP_DIM)),
        cn=bf(cn / math.sqrt(s_lat)), sn=bf(sn / math.sqrt(s_lat)),
        cc=bf(cc / math.sqrt(n_ctx)), sc=bf(sc / math.sqrt(n_ctx)),
    )


def _dup_heads(w, off, n_heads):
    cols = []
    for hd in range(n_heads):
        blk = w[:, off + hd * HEAD_DIM: off + (hd + 1) * HEAD_DIM]
        cols += [blk, blk]
    return cols


def _relayout_w_in(w):
    cols = [w[:, _OFF["aq"]:_OFF["bq"]], w[:, _OFF["bq"]:_OFF["bk"]],
            w[:, _OFF["cq"]:_OFF["ck"]], w[:, _OFF["fz"]:_OFF["gt"]]]
    cols += _dup_heads(w, _OFF["bk"], B_KV) + _dup_heads(w, _OFF["bv"], B_KV)
    cols += _dup_heads(w, _OFF["ck"], C_KV) + _dup_heads(w, _OFF["cv"], C_KV)
    return jnp.concatenate(cols, axis=1).astype(BF16)


def kernel(x, c, ctx, c_ctx, w_mod, b_mod, norm1_g, norm2_g, w_in, diff_lambda, diff_subln_g, qk_norm_g,
           sink_logits, w_branch, w_out, ffn_w_gate, ffn_w_up, ffn_w_down, router_w, moe_w_gate, moe_w_up,
           moe_w_down, final_g):
    b, s_lat, d = x.shape
    n_ctx = ctx.shape[1]
    depth = w_mod.shape[0]
    nt = s_lat + n_ctx
    t = ROW_TILE
    assert s_lat % t == 0 and n_ctx % t == 0 and s_lat % n_ctx == 0 and b < MOD_ROWS
    assert w_in.shape[-1] == _OFF["end"] and d * N_BRANCH == _OFF["end"] - _OFF["gt"]
    n_lat_tiles = s_lat // t

    xs = jnp.concatenate([x, ctx], axis=1)
    cvec = jnp.concatenate([c, c_ctx[None], jnp.zeros((MOD_ROWS - b - 1, d), F32)], axis=0)
    mod = _modulation(cvec, w_mod, b_mod).reshape(depth, MOD_ROWS, 1, 6 * d)
    tabs = _rope_tables(s_lat, n_ctx)
    consts = _constants(s_lat, n_ctx)

    out = None
    for l in range(depth):
        last = l == depth - 1
        mod_l = mod[l]
        p = _inproj(xs, mod_l, norm1_g[l][None], _relayout_w_in(w_in[l]), tabs,
                    jnp.tile(qk_norm_g[l, 0], 512 // HEAD_DIM)[None], jnp.tile(qk_norm_g[l, 1], 256 // HEAD_DIM)[None],
                    consts, s_lat)
        p = (None, *p)
        w_gates = w_in[l][:, _OFF["gt"]:_OFF["end"]].astype(BF16)
        lam_init = 0.8 - 0.6 * math.exp(-0.3 * l)
        n_q_tiles = n_lat_tiles if last else nt // t
        oa, ob, oc = _attention(p, diff_lambda[l], diff_subln_g[l][None], sink_logits[l],
                                n_lat_tiles, n_q_tiles, s_lat, lam_init)
        od = _fourier(p[10], p[11], consts, s_lat, n_q_tiles * t)
        xs = _merge(xs, mod_l, norm1_g[l][None], w_gates, (oa, ob, oc, od), w_branch[l].astype(BF16),
                    w_out[l].astype(BF16), s_lat, n_q_tiles * t)

        i = l // 2
        if l % 2 == 0:
            assert not last
            xs = _ffn(xs, mod_l, norm2_g[l][None], ffn_w_gate[i].astype(BF16), ffn_w_up[i].astype(BF16),
                      ffn_w_down[i].astype(BF16), s_lat)
        else:
            w_r = jnp.pad(router_w[i], ((0, 0), (0, LANES - N_EXPERTS)))
            n_row_tiles = n_q_tiles
            m = b * n_row_tiles * t
            h2, sel = _router(xs, mod_l, norm2_g[l][None], w_r, s_lat, n_row_tiles * t)
            tm = _moe_tile_rows(m, moe_w_gate.shape[-1])
            tables = _dispatch(sel.reshape(m, LANES), tm)
            y_sorted = _experts(h2.reshape(m * SUBLANES, LANES), tables, moe_w_gate, moe_w_up, moe_w_down, i, tm)
            res = _combine(xs, sel, mod_l, y_sorted, tables[3], final_g[None], n_lat_tiles, n_row_tiles, last)
            if last:
                out = res
            else:
                xs = res
    if out is None:
        raise NotImplementedError("final normalisation is fused into the last (mixture-of-experts) layer")
    return out
```

```python
import functools
import math

import numpy as np
import jax
import jax.numpy as jnp
from jax import lax
from jax.experimental import pallas as pl
from jax.experimental.pallas import tpu as pltpu

F32 = jnp.float32
BF16 = jnp.bfloat16

GRID_W = 64
HEAD_DIM = 64
ROPE_THETA = 10000.0
WINDOW = 128
A_HEADS = 4
B_HEADS = 8
B_KV = 2
C_HEADS = 8
C_KV = 2
F_GROUPS = 8
F_GROUP_DIM = 64
BRANCH_W = 512
N_BRANCH = 4
N_EXPERTS = 8
EPS = 1e-6
SUBLN_EPS = 1e-5
NEG_INF = -1e30
LOG2E = math.log2(math.e)

LANES = 128
SUBLANES = 8
ROW_TILE = 256
WIDE_TILES = (768, 512, 256)
FFN_CHUNK = 768
MOE_TILES_PER_EXPERT = 4
MOE_FF_TILE = 512
MOD_ROWS = 16
VMEM_LIMIT = 56 * 1024 * 1024

_OFF = dict(aq=0, ak=512, av=1024, bq=1536, bk=2048, bv=2176, cq=2304, ck=2816, cv=2944,
            fz=3072, gt=3584, end=7680)
_P = dict(aq=0, ak=512, av=1024, bq=1536, cq=2048, fz=2560, bk=3072, bv=3328, ck=3584, cv=3840, end=4096)


def _params(sem):
    return pltpu.CompilerParams(dimension_semantics=sem, vmem_limit_bytes=VMEM_LIMIT)


def _full(shape):
    n = len(shape)
    return pl.BlockSpec(shape, lambda *_: (0,) * n)


def _silu(x):
    return x / (1.0 + jnp.exp(-x))


def _sigmoid(x):
    return 1.0 / (1.0 + jnp.exp(-x))


def _adaln(x, g, shift, scale):
    y = x * lax.rsqrt(jnp.mean(x * x, axis=-1, keepdims=True) + EPS) * g
    return y * (1.0 + scale) + shift


def _mod_kernel(c_ref, w_ref, b_ref, o_ref):
    s = _silu(c_ref[...]).astype(BF16)
    o_ref[...] = jnp.dot(s, w_ref[...].astype(BF16), preferred_element_type=F32) + b_ref[...]


def _modulation(cvec, w_mod, b_mod):
    depth, d, n = w_mod.shape
    tn = 1536 if n % 1536 == 0 else n
    return pl.pallas_call(
        _mod_kernel,
        grid=(depth, n // tn),
        in_specs=[_full((MOD_ROWS, d)),
                  pl.BlockSpec((None, d, tn), lambda l, j: (l, 0, j)),
                  pl.BlockSpec((None, 1, tn), lambda l, j: (l, 0, j))],
        out_specs=pl.BlockSpec((None, MOD_ROWS, tn), lambda l, j: (l, 0, j)),
        out_shape=jax.ShapeDtypeStruct((depth, MOD_ROWS, n), F32),
        compiler_params=_params(("arbitrary", "arbitrary")),
        name="modulation",
    )(cvec, w_mod, b_mod.reshape(depth, 1, n))


def _rope(p, cos, sin_lo, sin_hi):
    outs = []
    for c in range(p.shape[1] // LANES):
        xc = p[:, c * LANES:(c + 1) * LANES]
        outs.append(xc * cos + pltpu.roll(xc, LANES - 16, 1) * sin_lo + pltpu.roll(xc, 16, 1) * sin_hi)
    return outs[0] if len(outs) == 1 else jnp.concatenate(outs, axis=1)


def _group_rmsnorm(y, gmat, g):
    ss = jnp.dot((y * y).astype(BF16), gmat, preferred_element_type=F32)
    return y * lax.rsqrt(ss * (1.0 / HEAD_DIM) + EPS) * g


def _inproj_kernel(x_ref, modb_ref, modc_ref, g_ref, w_ref, cos_ref, slo_ref, shi_ref, qg_ref, kg_ref, gmat_ref,
                   c64_ref, s64_ref,
                   aq_ref, akt_ref, av_ref, bq_ref, bkt_ref, bv_ref, cq_ref, ckt_ref, cv_ref,
                   fzc_ref, fzs_ref, *, d, s_lat):
    mod = _row_mod(modb_ref, modc_ref, x_ref.shape[0], s_lat, d)
    h = _adaln(x_ref[...], g_ref[...], mod(0), mod(1)).astype(BF16)
    cos, slo, shi = cos_ref[...], slo_ref[...], shi_ref[...]
    scale = HEAD_DIM ** -0.5 * LOG2E

    def proj(name, width):
        return lambda: jnp.dot(h, w_ref[:, _P[name]:_P[name] + width], preferred_element_type=F32)

    def finish_bq(p):
        bq_ref[...] = (_rope(_group_rmsnorm(p, gmat_ref[...], qg_ref[...]), cos, slo, shi) * scale).astype(BF16)

    def finish_bk(p):
        bkt_ref[...] = _rope(_group_rmsnorm(p, gmat_ref[0:256, 0:256], kg_ref[...]), cos, slo, shi).T.astype(BF16)

    def finish_fz(p):
        fz = p.astype(BF16)
        fzc_ref[...] = jnp.dot(fz, c64_ref[...], preferred_element_type=F32).astype(BF16)
        fzs_ref[...] = jnp.dot(fz, s64_ref[...], preferred_element_type=F32).astype(BF16)

    def store(ref, fn):
        def finish(p):
            ref[...] = fn(p).astype(BF16)
        return finish

    rope = lambda p: _rope(p, cos, slo, shi)
    groups = [
        (proj("aq", 512), store(aq_ref, lambda p: rope(p) * scale)),
        (proj("ak", 512), store(akt_ref, lambda p: rope(p).T)),
        (proj("av", 512), store(av_ref, lambda p: p)),
        (proj("bq", 512), finish_bq),
        (proj("bk", 256), finish_bk),
        (proj("bv", 256), store(bv_ref, lambda p: p)),
        (proj("cq", 512), store(cq_ref, lambda p: rope(p) * scale)),
        (proj("ck", 256), store(ckt_ref, lambda p: rope(p).T)),
        (proj("cv", 256), store(cv_ref, lambda p: p)),
        (proj("fz", 512), finish_fz),
    ]
    for (_, finish), p in zip(groups, _pipelined_scores([g[0] for g in groups])):
        finish(p)


def _inproj(xs, mod_l, g, w_p, tabs, qg, kg, consts, s_lat):
    b, nt, d = xs.shape
    t = _wide_tile(nt)
    row = lambda i, r: (i, r, 0)
    tr = lambda i, r: (i, 0, r)
    tab = pl.BlockSpec((t, LANES), lambda i, r: (r, 0))

    def out(width):
        return pl.BlockSpec((None, t, width), row), jax.ShapeDtypeStruct((b, nt, width), BF16)

    def out_t(width):
        return pl.BlockSpec((None, width, t), tr), jax.ShapeDtypeStruct((b, width, nt), BF16)

    outs = [out(512), out_t(512), out(512), out(512), out_t(256), out(256),
            out(512), out_t(256), out(256), out(512), out(512)]
    return pl.pallas_call(
        functools.partial(_inproj_kernel, d=d, s_lat=s_lat),
        grid=(b, nt // t),
        in_specs=[pl.BlockSpec((None, t, d), row), *_wide_mod_specs(mod_l, b),
                  _full((1, d)),
                  pl.BlockSpec(w_p.shape, lambda i, r: (0, 0), pipeline_mode=pl.Buffered(1)),
                  tab, tab, tab,
                  _full((1, 512)), _full((1, 256)), _full((512, 512)), _full((512, 512)), _full((512, 512))],
        out_specs=[o[0] for o in outs],
        out_shape=[o[1] for o in outs],
        compiler_params=_params(("arbitrary", "arbitrary")),
        name="inproj",
    )(xs, mod_l, mod_l, g, w_p, tabs[0], tabs[1], tabs[2], qg, kg, consts["gmat"], consts["c64"], consts["s64"])


def _half_masks(shape):
    lane = lax.broadcasted_iota(jnp.int32, shape, 1)
    return lane < HEAD_DIM, lane >= HEAD_DIM


def _exp2_bf16(s):
    return jnp.exp2(s - jnp.max(s, axis=-1, keepdims=True)).astype(BF16)


def _pipelined_scores(score_fns):
    nxt = score_fns[0]()
    for i in range(len(score_fns)):
        cur = nxt
        if i + 1 < len(score_fns):
            nxt = score_fns[i + 1]()
        yield cur


def _attn_a_kernel(q_ref, kt_ref, v_ref, lp_ref, g_ref, o_ref, *, n_lat_tiles, s_lat, lam_init):
    r = pl.program_id(1)
    lp = lp_ref[...]
    lam = (jnp.exp(jnp.sum(lp[0:1] * lp[1:2], axis=-1, keepdims=True))
           - jnp.exp(jnp.sum(lp[2:3] * lp[3:4], axis=-1, keepdims=True)) + lam_init)
    g = g_ref[...] * (1.0 - lam_init)
    nt = kt_ref.shape[1]

    def body(k0):
        lo, hi = _half_masks((q_ref.shape[0], LANES))
        ones = jnp.ones((nt - k0, LANES), BF16)

        def score_fn(hd, msk):
            cs = slice(hd * LANES, (hd + 1) * LANES)
            return lambda: jnp.dot(jnp.where(msk, q_ref[:, cs], 0), kt_ref[cs, k0:nt], preferred_element_type=F32)

        fns = [score_fn(hd, msk) for hd in range(A_HEADS) for msk in (lo, hi)]
        outs = []
        for i, sc in enumerate(_pipelined_scores(fns)):
            hd = i // 2
            cs = slice(hd * LANES, (hd + 1) * LANES)
            vo = jnp.concatenate([v_ref[k0:nt, cs], ones], axis=1)
            full = jnp.dot(_exp2_bf16(sc), vo, preferred_element_type=F32)
            outs.append(full[:, 0:LANES] * (1.0 / full[:, LANES:LANES + 1]))
            if len(outs) == 2:
                o = outs[0] - lam * outs[1]
                outs = []
                o = o * lax.rsqrt(jnp.mean(o * o, axis=-1, keepdims=True) + SUBLN_EPS) * g
                o_ref[:, cs] = o.astype(BF16)

    pl.when(r < n_lat_tiles)(lambda: body(0))
    pl.when(r >= n_lat_tiles)(lambda: body(s_lat))


def _attn_b_kernel(q_ref, kt_ref, v_ref, o_ref, *, n_lat_tiles, s_lat):
    r = pl.program_id(1)
    nt = kt_ref.shape[1]

    def body(k0):
        lo, hi = _half_masks((q_ref.shape[0], LANES))
        ones = jnp.ones((nt - k0, LANES), BF16)

        def score_fn(pair, msk):
            kv = pair // (B_HEADS // B_KV // 2)
            q = q_ref[:, pair * LANES:(pair + 1) * LANES]
            return lambda: jnp.dot(jnp.where(msk, q, 0), kt_ref[kv * LANES:(kv + 1) * LANES, k0:nt],
                                   preferred_element_type=F32)

        fns = [score_fn(pair, msk) for pair in range(B_HEADS // 2) for msk in (lo, hi)]
        halves = []
        for i, sc in enumerate(_pipelined_scores(fns)):
            pair = i // 2
            kv = pair // (B_HEADS // B_KV // 2)
            vo = jnp.concatenate([v_ref[k0:nt, kv * LANES:(kv + 1) * LANES], ones], axis=1)
            full = jnp.dot(_exp2_bf16(sc), vo, preferred_element_type=F32)
            halves.append(full[:, 0:LANES] * (1.0 / full[:, LANES:LANES + 1]))
            if len(halves) == 2:
                o_ref[:, pair * LANES:(pair + 1) * LANES] = jnp.where(lo, halves[0], halves[1]).astype(BF16)
                halves = []

    pl.when(r < n_lat_tiles)(lambda: body(0))
    pl.when(r >= n_lat_tiles)(lambda: body(s_lat))


def _attn_c_kernel(sink_ref, q_ref, ktl_ref, ktm_ref, ktr_ref, ktc_ref, vl_ref, vm_ref, vr_ref, vc_ref, o_ref,
                   *, n_lat_tiles, s_lat):
    r = pl.program_id(1)
    t = q_ref.shape[0]
    n_band = t + 2 * WINDOW
    n_keys = n_band + ktc_ref.shape[1]
    u = lax.broadcasted_iota(jnp.int32, (t, n_keys), 1)
    row = lax.broadcasted_iota(jnp.int32, (t, n_keys), 0)
    j = r * t - WINDOW + u
    in_band = (u >= row) & (u <= row + 2 * WINDOW) & (j >= 0) & (j < s_lat) & (r < n_lat_tiles)
    valid = in_band | (u >= n_band)
    lo, hi = _half_masks((t, LANES))
    ones = jnp.ones((n_keys, LANES), BF16)
    kts, vos = [], []
    for kv in range(C_KV):
        ks = slice(kv * LANES, (kv + 1) * LANES)
        kts.append(jnp.concatenate([ktl_ref[ks, :], ktm_ref[ks, :], ktr_ref[ks, :], ktc_ref[ks, :]], axis=1))
        v = jnp.concatenate([vl_ref[:, ks], vm_ref[:, ks], vr_ref[:, ks], vc_ref[:, ks]], axis=0)
        vos.append(jnp.concatenate([v, ones], axis=1))

    def score_fn(pair, msk):
        kv = pair // (C_HEADS // C_KV // 2)
        q = q_ref[:, pair * LANES:(pair + 1) * LANES]
        return lambda: jnp.dot(jnp.where(msk, q, 0), kts[kv], preferred_element_type=F32)

    fns = [score_fn(pair, msk) for pair in range(C_HEADS // 2) for msk in (lo, hi)]
    halves = []
    for i, sc in enumerate(_pipelined_scores(fns)):
        pair = i // 2
        kv = pair // (C_HEADS // C_KV // 2)
        sink = sink_ref[i] * LOG2E
        sc = jnp.where(valid, sc, NEG_INF)
        m = jnp.maximum(jnp.max(sc, axis=-1, keepdims=True), sink)
        full = jnp.dot(jnp.exp2(sc - m).astype(BF16), vos[kv], preferred_element_type=F32)
        l = full[:, LANES:LANES + 1] + jnp.exp2(sink - m)
        halves.append(full[:, 0:LANES] * (1.0 / l))
        if len(halves) == 2:
            o_ref[:, pair * LANES:(pair + 1) * LANES] = jnp.where(lo, halves[0], halves[1]).astype(BF16)
            halves = []


def _attention(p, lp, subln_g, sink, n_lat_tiles, n_q_tiles, s_lat, lam_init):
    (_, aq, akt, av, bq, bkt, bv, cq, ckt, cv, _, _) = p
    b, nt, _ = aq.shape
    t = ROW_TILE
    grid = (b, n_q_tiles)
    row = lambda i, r: (i, r, 0)
    whole = lambda i, r: (i, 0, 0)
    sem = _params(("arbitrary", "arbitrary"))
    o_spec = pl.BlockSpec((None, t, 512), row)
    o_shape = jax.ShapeDtypeStruct((b, n_q_tiles * t, 512), BF16)

    oa = pl.pallas_call(
        functools.partial(_attn_a_kernel, n_lat_tiles=n_lat_tiles, s_lat=s_lat, lam_init=lam_init),
        grid=grid,
        in_specs=[pl.BlockSpec((None, t, 512), row), pl.BlockSpec((None, 512, nt), whole),
                  pl.BlockSpec((None, nt, 512), whole), _full((4, HEAD_DIM)), _full((1, 2 * HEAD_DIM))],
        out_specs=o_spec, out_shape=o_shape, compiler_params=sem, name="attn_diff",
    )(aq, akt, av, lp, subln_g)

    ob = pl.pallas_call(
        functools.partial(_attn_b_kernel, n_lat_tiles=n_lat_tiles, s_lat=s_lat),
        grid=grid,
        in_specs=[pl.BlockSpec((None, t, 512), row), pl.BlockSpec((None, 256, nt), whole),
                  pl.BlockSpec((None, nt, 256), whole)],
        out_specs=o_spec, out_shape=o_shape, compiler_params=sem, name="attn_gqa",
    )(bq, bkt, bv)

    wpt = t // WINDOW
    last = s_lat // WINDOW - 1
    left = lambda r: jnp.clip(r * wpt - 1, 0, last)
    mid = lambda r: jnp.minimum(r, n_lat_tiles - 1)
    right = lambda r: jnp.clip((r + 1) * wpt, 0, last)
    n_ctx = nt - s_lat
    ctx_blk = s_lat // n_ctx
    oc = pl.pallas_call(
        functools.partial(_attn_c_kernel, n_lat_tiles=n_lat_tiles, s_lat=s_lat),
        grid=grid,
        in_specs=[pl.BlockSpec(memory_space=pltpu.SMEM),
                  pl.BlockSpec((None, t, 512), row),
                  pl.BlockSpec((None, 256, WINDOW), lambda i, r: (i, 0, left(r))),
                  pl.BlockSpec((None, 256, t), lambda i, r: (i, 0, mid(r))),
                  pl.BlockSpec((None, 256, WINDOW), lambda i, r: (i, 0, right(r))),
                  pl.BlockSpec((None, 256, n_ctx), lambda i, r: (i, 0, ctx_blk)),
                  pl.BlockSpec((None, WINDOW, 256), lambda i, r: (i, left(r), 0)),
                  pl.BlockSpec((None, t, 256), lambda i, r: (i, mid(r), 0)),
                  pl.BlockSpec((None, WINDOW, 256), lambda i, r: (i, right(r), 0)),
                  pl.BlockSpec((None, n_ctx, 256), lambda i, r: (i, ctx_blk, 0))],
        out_specs=o_spec, out_shape=o_shape, compiler_params=sem, name="attn_window",
    )(sink, cq, ckt, ckt, ckt, ckt, cv, cv, cv, cv)
    return oa, ob, oc


def _fourier_kernel(cn_ref, sn_ref, cc_ref, sc_ref, zc_ref, zs_ref, o_ref, *, s_lat, rows):
    r = pl.program_id(1)
    t = o_ref.shape[0]
    nt = zc_ref.shape[0]
    n_full = s_lat // t
    lat_rows = s_lat - n_full * t

    def latent(n):
        return (jnp.dot(cn_ref[0:n, :], zc_ref[0:s_lat, :], preferred_element_type=F32)
                - jnp.dot(sn_ref[0:n, :], zs_ref[0:s_lat, :], preferred_element_type=F32)).astype(BF16)

    @pl.when(r < n_full)
    def _():
        o_ref[...] = latent(t)

    if rows > s_lat:
        @pl.when(r >= n_full)
        def _():
            if lat_rows:
                o_ref[0:lat_rows, :] = latent(lat_rows)
            y = (jnp.dot(cc_ref[...], zc_ref[s_lat:nt, :], preferred_element_type=F32)
                 - jnp.dot(sc_ref[...], zs_ref[s_lat:nt, :], preferred_element_type=F32))
            o_ref[lat_rows:t, :] = y.astype(BF16)


def _fourier(fzc, fzs, consts, s_lat, rows):
    b, nt, w = fzc.shape
    t = _wide_tile(rows)
    n_ctx = nt - s_lat
    assert rows == s_lat or (rows == nt and nt - s_lat // t * t == t)
    tile = lambda i, r: (r, 0)
    whole = lambda i, r: (i, 0, 0)
    return pl.pallas_call(
        functools.partial(_fourier_kernel, s_lat=s_lat, rows=rows),
        grid=(b, rows // t),
        in_specs=[pl.BlockSpec((t, s_lat), tile), pl.BlockSpec((t, s_lat), tile),
                  _full((n_ctx, n_ctx)), _full((n_ctx, n_ctx)),
                  pl.BlockSpec((None, nt, w), whole), pl.BlockSpec((None, nt, w), whole)],
        out_specs=pl.BlockSpec((None, t, w), lambda i, r: (i, r, 0)),
        out_shape=jax.ShapeDtypeStruct((b, rows, w), BF16),
        compiler_params=_params(("arbitrary", "arbitrary")),
        name="fourier",
    )(consts["cn"], consts["sn"], consts["cc"], consts["sc"], fzc, fzs)


def _wide_tile(rows):
    for t in WIDE_TILES:
        if rows % t == 0:
            return t
    raise ValueError(rows)


def _wide_mod_specs(mod_l, b):
    n = mod_l.shape[-1]
    return [pl.BlockSpec((None, 1, n), lambda i, r: (i, 0, 0)), pl.BlockSpec((None, 1, n), lambda i, r: (b, 0, 0))]


def _row_mod(modb_ref, modc_ref, rows, s_lat, d):
    is_lat = pl.program_id(1) * rows + lax.broadcasted_iota(jnp.int32, (rows, 1), 0) < s_lat
    return lambda k: jnp.where(is_lat, modb_ref[:, k * d:(k + 1) * d], modc_ref[:, k * d:(k + 1) * d])


def _merge_kernel(x_ref, modb_ref, modc_ref, g_ref, wgt_ref, oa_ref, ob_ref, oc_ref, od_ref, wbr_ref, wo_ref,
                  o_ref, *, d, s_lat):
    mod = _row_mod(modb_ref, modc_ref, x_ref.shape[0], s_lat, d)
    x = x_ref[...]
    h = _adaln(x, g_ref[...], mod(0), mod(1)).astype(BF16)
    merged = None
    for i, br in enumerate((oa_ref, ob_ref, oc_ref, od_ref)):
        gate = _sigmoid(jnp.dot(h, wgt_ref[:, i * d:(i + 1) * d], preferred_element_type=F32))
        y = gate * jnp.dot(br[...], wbr_ref[i], preferred_element_type=F32)
        merged = y if merged is None else merged + y
    out = jnp.dot(merged.astype(BF16), wo_ref[...], preferred_element_type=F32)
    o_ref[...] = x + mod(2) * out


def _merge(xs, mod_l, g, w_gates, branches, w_br, w_o, s_lat, rows):
    b, _, d = xs.shape
    t = _wide_tile(rows)
    row = lambda i, r: (i, r, 0)
    br_spec = pl.BlockSpec((None, t, BRANCH_W), row)
    const = lambda shape: pl.BlockSpec(shape, lambda i, r: (0,) * len(shape), pipeline_mode=pl.Buffered(1))
    return pl.pallas_call(
        functools.partial(_merge_kernel, d=d, s_lat=s_lat),
        grid=(b, rows // t),
        in_specs=[pl.BlockSpec((None, t, d), row), *_wide_mod_specs(mod_l, b),
                  _full((1, d)), const(w_gates.shape),
                  br_spec, br_spec, br_spec, br_spec,
                  const(w_br.shape), const(w_o.shape)],
        out_specs=pl.BlockSpec((None, t, d), row),
        out_shape=jax.ShapeDtypeStruct((b, rows, d), F32),
        compiler_params=_params(("arbitrary", "arbitrary")),
        name="merge",
    )(xs, mod_l, mod_l, g, w_gates, *branches, w_br, w_o)


def _ffn_kernel(x_ref, modb_ref, modc_ref, g_ref, wg_ref, wu_ref, wd_ref, o_ref, *, d, s_lat):
    mod = _row_mod(modb_ref, modc_ref, x_ref.shape[0], s_lat, d)
    x = x_ref[...]
    h = _adaln(x, g_ref[...], mod(3), mod(4)).astype(BF16)
    dff = wg_ref.shape[1]
    y = None
    for c0 in range(0, dff, FFN_CHUNK):
        c1 = min(c0 + FFN_CHUNK, dff)
        a = jnp.dot(h, wg_ref[:, c0:c1], preferred_element_type=F32)
        u = jnp.dot(h, wu_ref[:, c0:c1], preferred_element_type=F32)
        part = jnp.dot((_silu(a) * u).astype(BF16), wd_ref[c0:c1, :], preferred_element_type=F32)
        y = part if y is None else y + part
    o_ref[...] = x + mod(5) * y


def _ffn(xs, mod_l, g, wg, wu, wd, s_lat):
    b, nt, d = xs.shape
    t = _wide_tile(nt)
    row = lambda i, r: (i, r, 0)
    const = lambda shape: pl.BlockSpec(shape, lambda i, r: (0, 0), pipeline_mode=pl.Buffered(1))
    return pl.pallas_call(
        functools.partial(_ffn_kernel, d=d, s_lat=s_lat),
        grid=(b, nt // t),
        in_specs=[pl.BlockSpec((None, t, d), row), *_wide_mod_specs(mod_l, b),
                  _full((1, d)), const(wg.shape), const(wu.shape), const(wd.shape)],
        out_specs=pl.BlockSpec((None, t, d), row),
        out_shape=jax.ShapeDtypeStruct(xs.shape, F32),
        compiler_params=_params(("arbitrary", "arbitrary")),
        name="ffn_dense",
    )(xs, mod_l, mod_l, g, wg, wu, wd)


def _store_token_major(ref, val):
    rows = val.shape[0]
    for c in range(SUBLANES):
        ref[pl.ds(c, rows, stride=SUBLANES), :] = val[:, c * LANES:(c + 1) * LANES]


def _load_token_major(ref, rows):
    return jnp.concatenate([ref[pl.ds(c, rows, stride=SUBLANES), :] for c in range(SUBLANES)], axis=1)


def _router_kernel(x_ref, modb_ref, modc_ref, g_ref, wr_ref, h_ref, sel_ref, *, d, s_lat):
    mod = _row_mod(modb_ref, modc_ref, x_ref.shape[0], s_lat, d)
    h = _adaln(x_ref[...], g_ref[...], mod(3), mod(4))
    _store_token_major(h_ref, h)
    logits = jnp.dot(h, wr_ref[...], preferred_element_type=F32, precision=lax.Precision.HIGHEST)
    lane = lax.broadcasted_iota(jnp.int32, logits.shape, 1)
    logits = jnp.where(lane < N_EXPERTS, logits, -jnp.inf)
    m1 = jnp.max(logits, axis=-1, keepdims=True)
    i1 = jnp.min(jnp.where(logits == m1, lane, LANES), axis=-1, keepdims=True)
    rest = jnp.where(lane == i1, -jnp.inf, logits)
    m2 = jnp.max(rest, axis=-1, keepdims=True)
    i2 = jnp.min(jnp.where(rest == m2, lane, LANES), axis=-1, keepdims=True)
    e2 = jnp.exp(m2 - m1)
    w1 = 1.0 / (1.0 + e2)
    sel_ref[...] = (jnp.where(lane == 0, w1, 0.0) + jnp.where(lane == 1, e2 * w1, 0.0)
                    + jnp.where(lane == 2, i1.astype(F32), 0.0) + jnp.where(lane == 3, i2.astype(F32), 0.0))


def _router(xs, mod_l, g, w_r, s_lat, rows):
    b, _, d = xs.shape
    t = _wide_tile(rows)
    row = lambda i, r: (i, r, 0)
    return pl.pallas_call(
        functools.partial(_router_kernel, d=d, s_lat=s_lat),
        grid=(b, rows // t),
        in_specs=[pl.BlockSpec((None, t, d), row), *_wide_mod_specs(mod_l, b),
                  _full((1, d)), _full(w_r.shape)],
        out_specs=[pl.BlockSpec((None, t * SUBLANES, LANES), row), pl.BlockSpec((None, t, LANES), row)],
        out_shape=[jax.ShapeDtypeStruct((b, rows * SUBLANES, LANES), F32),
                   jax.ShapeDtypeStruct((b, rows, LANES), F32)],
        compiler_params=_params(("arbitrary", "arbitrary")),
        name="moe_router",
    )(xs, mod_l, mod_l, g, w_r)


def _dispatch(sel, tm):
    m = sel.shape[0]
    i32 = jnp.int32
    experts = jnp.arange(N_EXPERTS, dtype=i32)

    def lookup(table, idx):
        return jnp.sum(jnp.where(idx[:, None] == experts[None, :], table[None, :], 0), axis=1)

    idx = sel[:, 2:4].astype(i32)
    e_flat = jnp.concatenate([idx[:, 0], idx[:, 1]])
    assignment = jnp.arange(2 * m, dtype=i32)
    _, order = lax.sort_key_val(e_flat, assignment)
    _, rank = lax.sort_key_val(order, assignment)
    counts = jnp.sum((e_flat[:, None] == experts[None, :]).astype(i32), axis=0)
    padded = (counts + tm - 1) // tm * tm
    pad_end = jnp.cumsum(padded)
    pad_start = pad_end - padded
    start = jnp.cumsum(counts) - counts
    n_tiles = 2 * m // tm + N_EXPERTS
    n_used = pad_end[-1] // tm
    tile = jnp.minimum(jnp.arange(n_tiles, dtype=i32), n_used - 1)
    tile_e = jnp.sum((tile[:, None] * tm >= pad_end[None, :]).astype(i32), axis=1)
    slot = jnp.arange(n_tiles * tm, dtype=i32)
    slot_e = jnp.sum((jnp.minimum(slot, pad_end[-1] - 1)[:, None] >= pad_end[None, :]).astype(i32), axis=1)
    pos = slot - lookup(pad_start, slot_e)
    valid = (pos < lookup(counts, slot_e)) & (slot < pad_end[-1])
    src = jnp.where(valid, order[jnp.clip(lookup(start, slot_e) + pos, 0, 2 * m - 1)] % m, 0)
    slot_of = lookup(pad_start - start, e_flat) + rank
    return (tile_e, n_used.reshape(1).astype(i32), (src * SUBLANES).astype(i32), (slot_of * SUBLANES).astype(i32))


def _experts_kernel(te_ref, nu_ref, src_ref, h_hbm, wg_ref, wu_ref, wd_ref, o_ref, xg_ref, xb_ref, acc_ref, sems,
                    *, n_tiles, n_steps):
    t, j = pl.program_id(0), pl.program_id(1)
    tm = xb_ref.shape[0]
    rows_per_step = tm // n_steps
    n_used = nu_ref[0]
    used = t < n_used
    buf = t % 2

    def start_row(tile, dst_buf, r):
        src = pl.multiple_of(src_ref[tile * tm + r], SUBLANES)
        dst = pl.multiple_of(r * SUBLANES, SUBLANES)
        pltpu.make_async_copy(h_hbm.at[pl.ds(src, SUBLANES)], xg_ref.at[dst_buf, pl.ds(dst, SUBLANES)],
                              sems.at[dst_buf]).start()

    def wait_rows(b_):
        pltpu.make_async_copy(h_hbm.at[pl.ds(0, tm * SUBLANES)], xg_ref.at[b_], sems.at[b_]).wait()

    @pl.when((t == 0) & (j == 0))
    def _():
        def issue(r, carry):
            start_row(0, 0, r)
            return carry

        lax.fori_loop(0, tm, issue, 0, unroll=8)

    @pl.when((j == 0) & ((t == 0) | (t - 1 < n_used)))
    def _():
        wait_rows(buf)
        xb_ref[...] = _load_token_major(xg_ref.at[buf], tm).astype(BF16)

    @pl.when(j == 0)
    def _():
        acc_ref[...] = jnp.zeros_like(acc_ref)

    nxt = jnp.minimum(t + 1, n_tiles - 1)

    @pl.when(used & (j == 0))
    def _():
        for r in range(rows_per_step):
            start_row(nxt, 1 - buf, r)

    @pl.when(used & (j < n_steps - 1))
    def _():
        for r in range(rows_per_step):
            start_row(nxt, 1 - buf, (j + 1) * rows_per_step + r)

    @pl.when(used)
    def _():
        h = xb_ref[...]
        a = jnp.dot(h, wg_ref[...].astype(BF16), preferred_element_type=F32)
        u = jnp.dot(h, wu_ref[...].astype(BF16), preferred_element_type=F32)
        act = (_silu(a) * u).astype(BF16)
        acc_ref[...] += jnp.dot(act, wd_ref[...].astype(BF16), preferred_element_type=F32)

    @pl.when(j == n_steps - 1)
    def _():
        _store_token_major(o_ref, acc_ref[...])

    @pl.when(used & (t == n_tiles - 1) & (j == n_steps - 1))
    def _():
        wait_rows(1 - buf)


def _moe_ff_tile(dff):
    return MOE_FF_TILE if dff % MOE_FF_TILE == 0 else dff


def _moe_tile_rows(m, dff):
    per_expert = 2.0 * m / N_EXPERTS
    unit = (dff // _moe_ff_tile(dff)) * 2 * SUBLANES
    rows = (per_expert + 3.0 * math.sqrt(per_expert)) / MOE_TILES_PER_EXPERT
    return int(math.ceil(rows / unit)) * unit


def _experts(h2, tables, wg, wu, wd, layer, tm):
    d, dff = wg.shape[-2:]
    assert d == SUBLANES * LANES
    tile_e, n_used, src, _ = tables
    n_tiles = tile_e.shape[0]
    tf = MOE_FF_TILE if dff % MOE_FF_TILE == 0 else dff
    nj = dff // tf
    assert tm % nj == 0
    col = lambda t, j, te, nu, src: (layer, te[t], 0, jnp.where(t < nu[0], j, nj - 1))
    rowb = lambda t, j, te, nu, src: (layer, te[t], jnp.where(t < nu[0], j, nj - 1), 0)
    return pl.pallas_call(
        functools.partial(_experts_kernel, n_tiles=n_tiles, n_steps=nj),
        grid_spec=pltpu.PrefetchScalarGridSpec(
            num_scalar_prefetch=3,
            grid=(n_tiles, nj),
            in_specs=[pl.BlockSpec(memory_space=pl.ANY),
                      pl.BlockSpec((None, None, d, tf), col),
                      pl.BlockSpec((None, None, d, tf), col),
                      pl.BlockSpec((None, None, tf, d), rowb)],
            out_specs=pl.BlockSpec((tm * SUBLANES, LANES), lambda t, j, te, nu, src: (t, 0)),
            scratch_shapes=[pltpu.VMEM((2, tm * SUBLANES, LANES), F32), pltpu.VMEM((tm, d), BF16),
                            pltpu.VMEM((tm, d), F32), pltpu.SemaphoreType.DMA((2,))]),
        out_shape=jax.ShapeDtypeStruct((n_tiles * tm * SUBLANES, LANES), F32),
        compiler_params=_params(("arbitrary", "arbitrary")),
        name="moe_experts",
    )(tile_e, n_used, src, h2, wg, wu, wd)


def _combine_kernel(slot_ref, x_ref, sel_ref, mod_ref, g_ref, y_hbm, o_ref, r1_ref, r2_ref, sems,
                    *, d, m, rows_per_sample, n_steps, final):
    t = x_ref.shape[0]
    n_r = rows_per_sample // t
    step = pl.program_id(0) * n_r + pl.program_id(1)
    buf = step % 2

    def start_tile(tile, dst_buf):
        base = tile * t

        def issue(k, carry):
            dst = pl.ds(pl.multiple_of(k * SUBLANES, SUBLANES), SUBLANES)
            s1 = pl.multiple_of(slot_ref[base + k], SUBLANES)
            s2 = pl.multiple_of(slot_ref[m + base + k], SUBLANES)
            pltpu.make_async_copy(y_hbm.at[pl.ds(s1, SUBLANES)], r1_ref.at[dst_buf, dst],
                                  sems.at[dst_buf, 0]).start(priority=0)
            pltpu.make_async_copy(y_hbm.at[pl.ds(s2, SUBLANES)], r2_ref.at[dst_buf, dst],
                                  sems.at[dst_buf, 1]).start(priority=1)
            return carry

        lax.fori_loop(0, t, issue, 0, unroll=8)

    @pl.when(step == 0)
    def _():
        start_tile(0, 0)

    pltpu.make_async_copy(y_hbm.at[pl.ds(0, t * SUBLANES)], r1_ref.at[buf], sems.at[buf, 0]).wait()
    pltpu.make_async_copy(y_hbm.at[pl.ds(0, t * SUBLANES)], r2_ref.at[buf], sems.at[buf, 1]).wait()

    @pl.when(step + 1 < n_steps)
    def _():
        start_tile(step + 1, 1 - buf)

    y = (sel_ref[:, 0:1] * _load_token_major(r1_ref.at[buf], t)
         + sel_ref[:, 1:2] * _load_token_major(r2_ref.at[buf], t))
    x = x_ref[...] + mod_ref[:, 5 * d:6 * d] * y
    if final:
        x = x * lax.rsqrt(jnp.mean(x * x, axis=-1, keepdims=True) + EPS) * g_ref[...]
    o_ref[...] = x


def _combine(xs, sel, mod_l, y_sorted, slot_of, final_g, n_lat_tiles, n_row_tiles, final):
    b, _, d = xs.shape
    t = ROW_TILE
    rows = n_row_tiles * t
    row = lambda i, r, s: (i, r, 0)
    return pl.pallas_call(
        functools.partial(_combine_kernel, d=d, m=b * rows, rows_per_sample=rows, n_steps=b * n_row_tiles,
                          final=final),
        grid_spec=pltpu.PrefetchScalarGridSpec(
            num_scalar_prefetch=1,
            grid=(b, n_row_tiles),
            in_specs=[pl.BlockSpec((None, t, d), row),
                      pl.BlockSpec((None, t, LANES), row),
                      pl.BlockSpec((None, 1, mod_l.shape[-1]),
                                   lambda i, r, s: (jnp.where(r < n_lat_tiles, i, b), 0, 0)),
                      pl.BlockSpec((1, d), lambda i, r, s: (0, 0)),
                      pl.BlockSpec(memory_space=pl.ANY)],
            out_specs=pl.BlockSpec((None, t, d), row),
            scratch_shapes=[pltpu.VMEM((2, t * SUBLANES, LANES), F32), pltpu.VMEM((2, t * SUBLANES, LANES), F32),
                            pltpu.SemaphoreType.DMA((2, 2))]),
        out_shape=jax.ShapeDtypeStruct((b, rows, d), F32),
        compiler_params=_params(("arbitrary", "arbitrary")),
        name="moe_combine",
    )(slot_of, xs, sel, mod_l, final_g, y_sorted)


def _rope_tables(s_lat, n_ctx):
    rows = s_lat // GRID_W
    pos_row = jnp.repeat(jnp.arange(rows, dtype=jnp.int32), GRID_W).astype(F32)
    pos_col = jnp.tile(jnp.arange(GRID_W, dtype=jnp.int32), rows).astype(F32)
    quarter = HEAD_DIM // 4
    inv_freq = ROPE_THETA ** (-jnp.arange(quarter, dtype=F32) / quarter)
    ang_r = pos_row[:, None] * inv_freq
    ang_c = pos_col[:, None] * inv_freq
    ang = jnp.concatenate([ang_r, ang_r, ang_c, ang_c], axis=-1)
    cos, sin = jnp.cos(ang), jnp.sin(ang)
    first = (np.arange(HEAD_DIM) % (2 * quarter)) < quarter
    sin_lo = jnp.where(first, -sin, 0.0)
    sin_hi = jnp.where(first, 0.0, sin)

    def widen(tab, ctx_fill):
        tab = jnp.concatenate([tab, jnp.full((n_ctx, HEAD_DIM), ctx_fill, F32)], axis=0)
        return jnp.tile(tab, (1, LANES // HEAD_DIM))

    return widen(cos, 1.0), widen(sin_lo, 0.0), widen(sin_hi, 0.0)


def _dft_tables(n):
    k = np.arange(n, dtype=np.int64)
    ang = 2.0 * np.pi * ((k[:, None] * k[None, :]) % n).astype(np.float64) / n
    return np.cos(ang), np.sin(ang)


def _constants(s_lat, n_ctx):
    c64, s64 = _dft_tables(F_GROUP_DIM)
    eye = np.eye(F_GROUPS)
    cn, sn = _dft_tables(s_lat)
    cc, sc = _dft_tables(n_ctx)
    bf = lambda a: jnp.asarray(a, dtype=F32).astype(BF16)
    pad = np.zeros((n_ctx, s_lat))
    cn, sn = np.concatenate([cn, pad]), np.concatenate([sn, pad])
    return dict(
        gmat=bf(np.kron(np.eye(512 // HEAD_DIM), np.ones((HEAD_DIM, HEAD_DIM)))),
        c64=bf(np.kron(eye, c64) / math.sqrt(F_GROUP_DIM)),
        s64=bf(np.kron(eye, s64) / math.sqrt(F_GROUP_DIM)),
        cn=bf(cn / math.sqrt(s_lat)), sn=bf(sn / math.sqrt(s_lat)),
        cc=bf(cc / math.sqrt(n_ctx)), sc=bf(sc / math.sqrt(n_ctx)),
    )


def _dup_heads(w, off, n_heads):
    cols = []
    for hd in range(n_heads):
        blk = w[:, off + hd * HEAD_DIM: off + (hd + 1) * HEAD_DIM]
        cols += [blk, blk]
    return cols


def _relayout_w_in(w):
    cols = [w[:, _OFF["aq"]:_OFF["bq"]], w[:, _OFF["bq"]:_OFF["bk"]],
            w[:, _OFF["cq"]:_OFF["ck"]], w[:, _OFF["fz"]:_OFF["gt"]]]
    cols += _dup_heads(w, _OFF["bk"], B_KV) + _dup_heads(w, _OFF["bv"], B_KV)
    cols += _dup_heads(w, _OFF["ck"], C_KV) + _dup_heads(w, _OFF["cv"], C_KV)
    return jnp.concatenate(cols, axis=1).astype(BF16)


def kernel(x, c, ctx, c_ctx, w_mod, b_mod, norm1_g, norm2_g, w_in, diff_lambda, diff_subln_g, qk_norm_g,
           sink_logits, w_branch, w_out, ffn_w_gate, ffn_w_up, ffn_w_down, router_w, moe_w_gate, moe_w_up,
           moe_w_down, final_g):
    b, s_lat, d = x.shape
    n_ctx = ctx.shape[1]
    depth = w_mod.shape[0]
    nt = s_lat + n_ctx
    t = ROW_TILE
    assert s_lat % t == 0 and n_ctx % t == 0 and s_lat % n_ctx == 0 and b < MOD_ROWS
    assert w_in.shape[-1] == _OFF["end"] and d * N_BRANCH == _OFF["end"] - _OFF["gt"]
    n_lat_tiles = s_lat // t

    xs = jnp.concatenate([x, ctx], axis=1)
    cvec = jnp.concatenate([c, c_ctx[None], jnp.zeros((MOD_ROWS - b - 1, d), F32)], axis=0)
    mod = _modulation(cvec, w_mod, b_mod).reshape(depth, MOD_ROWS, 1, 6 * d)
    tabs = _rope_tables(s_lat, n_ctx)
    consts = _constants(s_lat, n_ctx)

    out = None
    for l in range(depth):
        last = l == depth - 1
        mod_l = mod[l]
        p = _inproj(xs, mod_l, norm1_g[l][None], _relayout_w_in(w_in[l]), tabs,
                    jnp.tile(qk_norm_g[l, 0], 512 // HEAD_DIM)[None], jnp.tile(qk_norm_g[l, 1], 256 // HEAD_DIM)[None],
                    consts, s_lat)
        p = (None, *p)
        w_gates = w_in[l][:, _OFF["gt"]:_OFF["end"]].astype(BF16)
        lam_init = 0.8 - 0.6 * math.exp(-0.3 * l)
        n_q_tiles = n_lat_tiles if last else nt // t
        oa, ob, oc = _attention(p, diff_lambda[l], diff_subln_g[l][None], sink_logits[l],
                                n_lat_tiles, n_q_tiles, s_lat, lam_init)
        od = _fourier(p[10], p[11], consts, s_lat, n_q_tiles * t)
        xs = _merge(xs, mod_l, norm1_g[l][None], w_gates, (oa, ob, oc, od), w_branch[l].astype(BF16),
                    w_out[l].astype(BF16), s_lat, n_q_tiles * t)

        i = l // 2
        if l % 2 == 0:
            assert not last
            xs = _ffn(xs, mod_l, norm2_g[l][None], ffn_w_gate[i].astype(BF16), ffn_w_up[i].astype(BF16),
                      ffn_w_down[i].astype(BF16), s_lat)
        else:
            w_r = jnp.pad(router_w[i], ((0, 0), (0, LANES - N_EXPERTS)))
            n_row_tiles = n_q_tiles
            m = b * n_row_tiles * t
            h2, sel = _router(xs, mod_l, norm2_g[l][None], w_r, s_lat, n_row_tiles * t)
            tm = _moe_tile_rows(m, moe_w_gate.shape[-1])
            tables = _dispatch(sel.reshape(m, LANES), tm)
            y_sorted = _experts(h2.reshape(m * SUBLANES, LANES), tables, moe_w_gate, moe_w_up, moe_w_down, i, tm)
            res = _combine(xs, sel, mod_l, y_sorted, tables[3], final_g[None], n_lat_tiles, n_row_tiles, last)
            if last:
                out = res
            else:
                xs = res
    if out is None:
        raise NotImplementedError("final normalisation is fused into the last (mixture-of-experts) layer")
    return out
```
